```python
import jax, jax.numpy as jnp
from jax import lax
import numpy as np

D_MODEL = 1024
BATCH = 4
SEQ = 8192
DEPTH = 2

GRID_W = 64
CTX_LEN = 256
D_MIX = 2 * D_MODEL
EPS = 1e-6

M_HEADS = 8
M_DV = D_MODEL // 8
M_DQK = M_DV // 2
M_WIDTH = M_HEADS * M_DV
QK_W = M_HEADS * M_DQK
M_CHUNK = 128
N_GATES = 4 * M_HEADS

S_WIDTH = D_MODEL // 2
S_GROUPS = 4
S_GC = S_WIDTH // S_GROUPS
S_CHUNK = 128
ROWS_PER_CHUNK = S_CHUNK // GRID_W

F_WIDTH = D_MODEL // 2
F_GROUPS = 4
F_GC = F_WIDTH // F_GROUPS

A_SPLITS = (QK_W, QK_W, M_WIDTH, N_GATES)
REST_SPLITS = (M_WIDTH, M_WIDTH, S_WIDTH, S_WIDTH, S_WIDTH, F_WIDTH, F_WIDTH)
A_IN = int(sum(A_SPLITS))
P_IN = A_IN + int(sum(REST_SPLITS))
A_SPLIT_IDX = tuple(int(i) for i in np.cumsum(A_SPLITS)[:-1])
REST_SPLIT_IDX = tuple(int(i) for i in np.cumsum(REST_SPLITS)[:-1])

kernel_name = 'hybrid_mlstm_sgu_fourier_dit_block'


def rms_norm(x, g):
    xf = x.astype(jnp.float32)
    y = xf * lax.rsqrt(jnp.mean(xf * xf, axis=-1, keepdims=True) + EPS)
    return (y * g.astype(jnp.float32)).astype(x.dtype)


def zero_state(bsz):
    return (jnp.zeros((bsz, M_HEADS, M_DQK, M_DV), jnp.float32),
            jnp.zeros((bsz, M_HEADS, M_DQK), jnp.float32),
            jnp.zeros((bsz, M_HEADS), jnp.float32))


def mlstm_scan(q, k, v, ig, lf, state):
    bsz, nh, t = ig.shape
    nc = t // M_CHUNK

    def chunks(a):
        return jnp.moveaxis(a.reshape(bsz, nh, nc, M_CHUNK, *a.shape[3:]), 2, 0)

    lower = jnp.tril(jnp.ones((M_CHUNK, M_CHUNK), dtype=bool))

    def step(carry, inp):
        C, n, m = carry
        qc, kc, vc, igc, lfc = inp
        b = jnp.cumsum(lfc, axis=-1)
        d_log = jnp.where(lower, b[..., :, None] - b[..., None, :] + igc[..., None, :], -jnp.inf)
        inter = b + m[..., None]
        m_t = jnp.maximum(inter, jnp.max(d_log, axis=-1))
        s = jnp.einsum('bhtk,bhsk->bhts', qc, kc) * jnp.exp(d_log - m_t[..., None])
        w_inter = jnp.exp(inter - m_t)
        num = jnp.einsum('bhts,bhsv->bhtv', s, vc) + w_inter[..., None] * jnp.einsum('bhtk,bhkv->bhtv', qc, C)
        den = jnp.sum(s, axis=-1) + w_inter * jnp.einsum('bhtk,bhk->bht', qc, n)
        hc = num / jnp.maximum(jnp.abs(den), jnp.exp(-m_t))[..., None]
        b_last = b[..., -1]
        w_log = b_last[..., None] - b + igc
        m_new = jnp.maximum(b_last + m, jnp.max(w_log, axis=-1))
        decay = jnp.exp(b_last + m - m_new)
        w_s = jnp.exp(w_log - m_new[..., None])
        C_new = decay[..., None, None] * C + jnp.einsum('bhs,bhsk,bhsv->bhkv', w_s, kc, vc)
        n_new = decay[..., None] * n + jnp.einsum('bhs,bhsk->bhk', w_s, kc)
        return (C_new, n_new, m_new), hc

    state, h = lax.scan(step, state, (chunks(q), chunks(k), chunks(v), chunks(ig), chunks(lf)))
    h = jnp.moveaxis(h, 0, 2).reshape(bsz, nh, t, M_DV)
    return h, state


def mlstm_from_proj(pa, b_gate, g_hnorm, init):
    bsz, t = pa.shape[:2]
    q, k, v, gates = jnp.split(pa, A_SPLIT_IDX, axis=-1)

    def heads(a, d):
        return a.astype(jnp.float32).reshape(bsz, t, M_HEADS, d).transpose(0, 2, 1, 3)

    qh = heads(q, M_DQK) * (M_DQK ** -0.5)
    kh = heads(k, M_DQK)
    vh = heads(v, M_DV)
    g = (gates.astype(jnp.float32) + b_gate.astype(jnp.float32)).reshape(bsz, t, 4, M_HEADS).transpose(2, 0, 3, 1)
    ig_f, lf_f = g[0], jax.nn.log_sigmoid(g[1])
    ig_b, lf_b = g[2], jax.nn.log_sigmoid(g[3])
    init_f, init_b = init
    h_f, st_f = mlstm_scan(qh, kh, vh, ig_f, lf_f, init_f)

    def rev(a):
        return jnp.flip(a, axis=2)

    h_b, st_b = mlstm_scan(rev(qh), rev(kh), rev(vh), rev(ig_b), rev(lf_b), init_b)
    h = h_f + rev(h_b)
    h = h * lax.rsqrt(jnp.mean(h * h, axis=-1, keepdims=True) + EPS)
    h = h * g_hnorm.astype(jnp.float32).reshape(M_HEADS, 1, M_DV)
    return h.transpose(0, 2, 1, 3).reshape(bsz, t, M_WIDTH), (st_f, st_b)


def spatial_gating(u, vs, g_sgu, w_sp, b_sp, n_chunks):
    bsz, t = u.shape[:2]
    vn = rms_norm(vs, g_sgu).reshape(bsz, n_chunks, S_CHUNK, S_GROUPS, S_GC)
    mixed = jnp.einsum('gts,bnsgc->bntgc', w_sp, vn) + b_sp.T[:, :, None]
    return u * mixed.reshape(bsz, t, S_WIDTH)


def fourier_mix(f, w_fno, b_fno):
    bsz, t = f.shape[:2]
    fg = f.astype(jnp.float32).reshape(bsz, t, F_GROUPS, F_GC)
    y = jnp.real(jnp.fft.fft2(fg, axes=(1, 3), norm='ortho'))
    y = jnp.einsum('btgc,gcd->btgd', y, w_fno.astype(jnp.float32)) + b_fno.astype(jnp.float32)
    return y.reshape(bsz, t, F_WIDTH).astype(f.dtype)


def mixer(h, n_chunks, init, w_in, b_gate, g_hnorm, g_sgu, w_sp, b_sp, w_fno, b_fno, w_out):
    p = h @ w_in
    h_a, states = mlstm_from_proj(p[..., :A_IN], b_gate, g_hnorm, init)
    o, z_a, u, vs, z_b, f, z_c = jnp.split(p[..., A_IN:], REST_SPLIT_IDX, axis=-1)
    y_a = h_a.astype(h.dtype) * jax.nn.sigmoid(o) * jax.nn.silu(z_a)
    y_b = spatial_gating(u, vs, g_sgu, w_sp, b_sp, n_chunks) * jax.nn.silu(z_b)
    y_c = fourier_mix(f, w_fno, b_fno) * jax.nn.silu(z_c)
    return jnp.concatenate([y_a, y_b, y_c], axis=-1) @ w_out, states


def setup_inputs(seed: int = 0) -> dict:
    key = jax.random.key(seed)
    ks = jax.random.split(key, 20)

    def nrm(k, shape, s):
        return jax.random.normal(k, shape, jnp.float32) * s

    x = nrm(ks[0], (BATCH, SEQ, D_MODEL), 1.0)
    c = nrm(ks[1], (BATCH, D_MODEL), 1.0)
    ctx = nrm(ks[2], (BATCH, CTX_LEN, D_MODEL), 1.0)
    c_ctx = nrm(ks[3], (D_MODEL,), 1.0)
    w_mod = nrm(ks[4], (DEPTH, D_MODEL, 3 * D_MODEL), 0.5 * D_MODEL ** -0.5)
    b_mod = nrm(ks[5], (DEPTH, 3 * D_MODEL), 0.01)
    g_pre = 1.0 + nrm(ks[6], (DEPTH, D_MODEL), 0.01)
    g_post = 1.0 + nrm(ks[7], (DEPTH, D_MODEL), 0.01)
    w_in = nrm(ks[8], (DEPTH, D_MODEL, P_IN), D_MODEL ** -0.5)
    ig_bias = nrm(ks[9], (DEPTH, 2, M_HEADS), 0.1)
    fg_bias = jnp.linspace(3.0, 6.0, M_HEADS) + nrm(ks[10], (DEPTH, 2, M_HEADS), 0.1)
    b_gate = jnp.stack([ig_bias, fg_bias], axis=2).reshape(DEPTH, N_GATES)
    g_hnorm = 1.0 + nrm(ks[11], (DEPTH, M_WIDTH), 0.01)
    g_sgu = 1.0 + nrm(ks[12], (DEPTH, S_WIDTH), 0.01)
    w_sp = nrm(ks[13], (DEPTH, S_GROUPS, S_CHUNK, S_CHUNK), S_CHUNK ** -0.5)
    b_sp = 1.0 + nrm(ks[14], (DEPTH, S_GROUPS, S_CHUNK), 0.01)
    w_fno = nrm(ks[15], (DEPTH, F_GROUPS, F_GC, F_GC), F_GC ** -0.5)
    b_fno = nrm(ks[16], (DEPTH, F_GROUPS, F_GC), 0.01)
    w_out = nrm(ks[17], (DEPTH, D_MIX, D_MODEL), D_MIX ** -0.5)
    return {'x': x, 'c': c, 'ctx': ctx, 'c_ctx': c_ctx, 'w_mod': w_mod, 'b_mod': b_mod,
            'g_pre': g_pre, 'g_post': g_post, 'w_in': w_in, 'b_gate': b_gate, 'g_hnorm': g_hnorm,
            'g_sgu': g_sgu, 'w_sp': w_sp, 'b_sp': b_sp, 'w_fno': w_fno, 'b_fno': b_fno, 'w_out': w_out}


def reference(x, c, ctx, c_ctx, w_mod, b_mod, g_pre, g_post, w_in, b_gate, g_hnorm,
              g_sgu, w_sp, b_sp, w_fno, b_fno, w_out):
    bsz = x.shape[0]
    rows = x.shape[1] // GRID_W
    lat_chunks = rows // ROWS_PER_CHUNK
    ctx_chunks = ctx.shape[1] // S_CHUNK
    xc = ctx
    for l in range(DEPTH):
        mod_lat = jax.nn.silu(c) @ w_mod[l] + b_mod[l]
        mod_ctx = jax.nn.silu(c_ctx) @ w_mod[l] + b_mod[l]
        sh_l, sc_l, gt_l = jnp.split(mod_lat, 3, axis=-1)
        sh_c, sc_c, gt_c = jnp.split(mod_ctx, 3, axis=-1)
        params = (w_in[l], b_gate[l], g_hnorm[l], g_sgu[l], w_sp[l], b_sp[l], w_fno[l], b_fno[l], w_out[l])
        hc = rms_norm(xc, g_pre[l]) * (1.0 + sc_c) + sh_c
        init0 = (zero_state(bsz), zero_state(bsz))
        if l < DEPTH - 1:
            yc, ctx_states = mixer(hc, ctx_chunks, init0, *params)
        else:
            _, ctx_states = mlstm_from_proj(hc @ w_in[l][:, :A_IN], b_gate[l], g_hnorm[l], init0)
        h = rms_norm(x, g_pre[l]) * (1.0 + sc_l[:, None, :]) + sh_l[:, None, :]
        y, _ = mixer(h, lat_chunks, ctx_states, *params)
        x = x + gt_l[:, None, :] * rms_norm(y, g_post[l])
        if l < DEPTH - 1:
            xc = xc + gt_c * rms_norm(yc, g_post[l])
    return x
```

```python
import functools

import numpy as np
import jax
import jax.numpy as jnp
from jax.experimental import pallas as pl
from jax.experimental.pallas import tpu as pltpu

EPS = 1e-6
D_MODEL = 1024
HEADS = 8
DV = 128
DQK = 64
QK_W = HEADS * DQK
M_WIDTH = HEADS * DV
CHUNK = 128
S_WIDTH = 512
F_WIDTH = 512
GROUPS = 4
GC = 128
UNITS = 2 * HEADS
LANES = 128

VMEM_LIMIT = 56 * 1024 * 1024

F32 = jnp.float32
BF16 = jnp.bfloat16


def _cparams(*sem):
    return pltpu.CompilerParams(dimension_semantics=sem, vmem_limit_bytes=VMEM_LIMIT)


def _sigmoid(x):
    return 1.0 / (1.0 + jnp.exp(-x))


def _silu(x):
    return x * _sigmoid(x)


def _mod_kernel(cc_ref, w_ref, b_ref, o_ref):
    s = _silu(cc_ref[...]).astype(BF16)
    w = w_ref[0].astype(BF16)
    o_ref[0] = jnp.dot(s, w, preferred_element_type=F32) + b_ref[0]


def _modulation(cc, w_mod, b_mod):
    depth, d, d3 = w_mod.shape
    return pl.pallas_call(
        _mod_kernel,
        grid=(depth,),
        in_specs=[pl.BlockSpec((8, d), lambda l: (0, 0)),
                  pl.BlockSpec((1, d, d3), lambda l: (l, 0, 0)),
                  pl.BlockSpec((1, 1, d3), lambda l: (l, 0, 0))],
        out_specs=pl.BlockSpec((1, 8, d3), lambda l: (l, 0, 0)),
        out_shape=jax.ShapeDtypeStruct((depth, 8, d3), F32),
        compiler_params=_cparams("arbitrary"),
        name="modulation",
    )(cc, w_mod, b_mod.reshape(depth, 1, d3))


def _inproj_kernel(x_ref, sc_ref, sh_ref, g_ref, wn_ref, wkt_ref, cs_ref, *out_refs, pieces):
    x = x_ref[0]
    y = x * jax.lax.rsqrt(jnp.mean(x * x, axis=-1, keepdims=True) + EPS)
    h = (y * g_ref[...]) * (1.0 + sc_ref[0]) + sh_ref[0]
    hb = h.astype(BF16)
    oi = 0
    for name, off, width in pieces:
        if name == "kT":
            r = jax.lax.dot_general(wkt_ref[...], hb, (((1,), (1,)), ((), ())),
                                    preferred_element_type=F32)
            out_refs[oi][0] = r.astype(BF16)
            oi += 1
            continue
        r = jnp.dot(hb, wn_ref[:, off:off + width], preferred_element_type=F32)
        if name == "f":
            fb = r.astype(BF16)
            xr_ref, xi_ref = out_refs[oi], out_refs[oi + 1]
            for g in range(GROUPS):
                z = jnp.dot(fb[:, g * GC:(g + 1) * GC], cs_ref[...], preferred_element_type=F32)
                xr_ref[0, :, g * GC:(g + 1) * GC] = z[:, :GC].astype(BF16)
                xi_ref[0, :, g * GC:(g + 1) * GC] = z[:, GC:].astype(BF16)
            oi += 2
        else:
            out_refs[oi][0] = r.astype(out_refs[oi].dtype)
            oi += 1


_FULL_PIECES = (("q", QK_W), ("v", M_WIDTH), ("gI", LANES), ("gF", LANES), ("o", M_WIDTH), ("za", M_WIDTH),
                ("u", S_WIDTH), ("vs", S_WIDTH), ("zb", S_WIDTH), ("f", F_WIDTH), ("zc", F_WIDTH))
_STATE_PIECES = _FULL_PIECES[:4]


def _inproj(x, sc, sh, g_pre, w_nat, w_kt, cs, *, full, tm):
    bsz, t, d = x.shape
    names = _FULL_PIECES if full else _STATE_PIECES
    pieces, off = [], 0
    out_shapes, out_specs = [], []
    for name, width in names:
        pieces.append((name, off, width))
        off += width
        if name == "f":
            for _ in range(2):
                out_shapes.append(jax.ShapeDtypeStruct((bsz, t, width), BF16))
                out_specs.append(pl.BlockSpec((1, tm, width), lambda b, i: (b, i, 0)))
        else:
            dt = F32 if name in ("gI", "gF") else BF16
            out_shapes.append(jax.ShapeDtypeStruct((bsz, t, width), dt))
            out_specs.append(pl.BlockSpec((1, tm, width), lambda b, i: (b, i, 0)))
    pieces.append(("kT", 0, QK_W))
    out_shapes.append(jax.ShapeDtypeStruct((bsz, QK_W, t), BF16))
    out_specs.append(pl.BlockSpec((1, QK_W, tm), lambda b, i: (b, 0, i)))
    n_nat = w_nat.shape[1]
    outs = pl.pallas_call(
        functools.partial(_inproj_kernel, pieces=tuple(pieces)),
        grid=(bsz, t // tm),
        in_specs=[pl.BlockSpec((1, tm, d), lambda b, i: (b, i, 0)),
                  pl.BlockSpec((1, 1, d), lambda b, i: (b, 0, 0)),
                  pl.BlockSpec((1, 1, d), lambda b, i: (b, 0, 0)),
                  pl.BlockSpec((1, d), lambda b, i: (0, 0)),
                  pl.BlockSpec((d, n_nat), lambda b, i: (0, 0), pipeline_mode=pl.Buffered(1)),
                  pl.BlockSpec((QK_W, d), lambda b, i: (0, 0), pipeline_mode=pl.Buffered(1)),
                  pl.BlockSpec((GC, 2 * GC), lambda b, i: (0, 0))],
        out_specs=out_specs,
        out_shape=out_shapes,
        compiler_params=_cparams("parallel", "parallel"),
        name="inproj_full" if full else "inproj_state",
    )(x, sc, sh, g_pre, w_nat, w_kt, cs)
    res, oi = {}, 0
    for name, _, _ in pieces:
        if name == "f":
            res["xr"], res["xi"] = outs[oi], outs[oi + 1]
            oi += 2
        else:
            res[name] = outs[oi]
            oi += 1
    return res


def _log_sigmoid(x):
    return jnp.minimum(x, 0.0) - jnp.log1p(jnp.exp(-jnp.abs(x)))


def _gateprep_kernel(gi_ref, gf_ref, bi_ref, bf_ref, cola_ref, colb_ref, rowa_ref, lasta_ref, lastb_ref):
    row = jax.lax.broadcasted_iota(jnp.int32, (CHUNK, LANES), 0)
    lane = jax.lax.broadcasted_iota(jnp.int32, (CHUNK, LANES), 1)
    fwd = lane < HEADS
    gi = gi_ref[0] + bi_ref[...]
    lf = _log_sigmoid(gf_ref[0] + bf_ref[...])

    def scan(x, op, fill):
        pre, suf = x, x
        k = 1
        while k < CHUNK:
            sh = pltpu.roll(pre, k, axis=0)
            pre = jnp.where(row >= k, op(pre, sh), pre)
            sh = pltpu.roll(suf, CHUNK - k, axis=0)
            suf = jnp.where(row < CHUNK - k, op(suf, sh), suf)
            k *= 2
        return jnp.where(fwd, pre, suf)

    b = scan(lf, jnp.add, 0.0)
    a = gi - b
    amax = scan(a, jnp.maximum, -jnp.inf)
    cola_ref[0] = amax
    colb_ref[0] = b
    rowa_ref[0] = a.T[0:UNITS, :]
    fwd_rows = jax.lax.broadcasted_iota(jnp.int32, (UNITS, LANES), 0) < HEADS
    at = amax.T[0:UNITS, :]
    bt = b.T[0:UNITS, :]
    lasta_ref[0, 0] = jnp.where(fwd_rows, jnp.broadcast_to(at[:, CHUNK - 1:CHUNK], (UNITS, LANES)),
                                jnp.broadcast_to(at[:, 0:1], (UNITS, LANES)))
    lastb_ref[0, 0] = jnp.where(fwd_rows, jnp.broadcast_to(bt[:, CHUNK - 1:CHUNK], (UNITS, LANES)),
                                jnp.broadcast_to(bt[:, 0:1], (UNITS, LANES)))


def _gateprep(gi, gf, bi, bf):
    bsz, t, _ = gi.shape
    nc = t // CHUNK
    blk = pl.BlockSpec((1, CHUNK, LANES), lambda b, j: (b, j, 0))
    vec = pl.BlockSpec((1, LANES), lambda b, j: (0, 0))
    return pl.pallas_call(
        _gateprep_kernel,
        grid=(bsz, nc),
        in_specs=[blk, blk, vec, vec],
        out_specs=[blk, blk,
                   pl.BlockSpec((1, UNITS, CHUNK), lambda b, j: (b, 0, j)),
                   pl.BlockSpec((1, 1, UNITS, LANES), lambda b, j: (b, j, 0, 0)),
                   pl.BlockSpec((1, 1, UNITS, LANES), lambda b, j: (b, j, 0, 0))],
        out_shape=[jax.ShapeDtypeStruct((bsz, t, LANES), F32),
                   jax.ShapeDtypeStruct((bsz, t, LANES), F32),
                   jax.ShapeDtypeStruct((bsz, UNITS, t), F32),
                   jax.ShapeDtypeStruct((bsz, nc, UNITS, LANES), F32),
                   jax.ShapeDtypeStruct((bsz, nc, UNITS, LANES), F32)],
        compiler_params=_cparams("parallel", "parallel"),
        name="gateprep",
    )(gi, gf, bi, bf)


def _mlstm_unit(h, d, q_ref, kt_ref, v_ref, cola_ref, colb_ref, rowa_ref, lasta_ref, lastb_ref,
                hout_ref, cn_ref, m_ref, mask):
    i = d * HEADS + h
    pair = h // 2
    q_pair = q_ref[0, :, pair * LANES:(pair + 1) * LANES]
    kt_h = kt_ref[0, h * DQK:(h + 1) * DQK, :]
    zk = jnp.zeros((DQK, CHUNK), BF16)
    kt_ext = jnp.concatenate([kt_h, zk] if h % 2 == 0 else [zk, kt_h], axis=0)
    v_h = v_ref[0, :, h * DV:(h + 1) * DV]
    vaug = jnp.concatenate([v_h, jnp.ones((CHUNK, DV), BF16)], axis=1)
    a_col = jnp.broadcast_to(cola_ref[0, :, i:i + 1], (CHUNK, CHUNK))
    b_col = jnp.broadcast_to(colb_ref[0, :, i:i + 1], (CHUNK, CHUNK))
    a_row = rowa_ref[0, h:h + 1, :]
    a_last = lasta_ref[0, 0, h:h + 1, :]
    b_last = lastb_ref[0, 0, h:h + 1, :]
    m_row = m_ref[0, i:i + 1, :]
    cn = cn_ref[0, i]

    g = jnp.maximum(m_row, a_col)
    dmat = jnp.where(mask, jnp.exp(a_row - g), 0.0)
    s = jnp.dot(q_pair, kt_ext, preferred_element_type=F32)
    p = (s * dmat).astype(BF16)
    qs = (q_pair.astype(F32) * jnp.exp(m_row - g)).astype(BF16)
    lhs = jnp.concatenate([p, qs], axis=1)
    zc = jnp.zeros((DQK, 2 * DV), BF16)
    cnb = cn.astype(BF16)
    cn_ext = jnp.concatenate([cnb, zc] if h % 2 == 0 else [zc, cnb], axis=0)
    rhs = jnp.concatenate([vaug, cn_ext], axis=0)
    out = jnp.dot(lhs, rhs, preferred_element_type=F32)
    num, den = out[:, :DV], out[:, DV:]
    hout_ref[0, :, h * DV:(h + 1) * DV] = num / jnp.maximum(jnp.abs(den), jnp.exp(-(b_col + g)))

    g_end = jnp.maximum(m_row, a_last)
    kts = (kt_h.astype(F32) * jnp.exp(a_row - g_end)).astype(BF16)
    upd = jnp.dot(kts, vaug, preferred_element_type=F32)
    decay = jnp.exp(m_row - g_end)
    cn_ref[0, i] = jnp.concatenate([decay, decay], axis=1) * cn + upd
    m_ref[0, i:i + 1, :] = b_last + g_end


def _mlstm_kernel(qf_ref, ktf_ref, vf_ref, caf_ref, cbf_ref, raf_ref, laf_ref, lbf_ref,
                  qb_ref, ktb_ref, vb_ref, cab_ref, cbb_ref, rab_ref, lab_ref, lbb_ref,
                  cn0_ref, m0_ref, hf_ref, hb_ref, cn_ref, m_ref):
    @pl.when(pl.program_id(1) == 0)
    def _():
        cn_ref[...] = cn0_ref[...]
        m_ref[...] = m0_ref[...]

    t_idx = jax.lax.broadcasted_iota(jnp.int32, (CHUNK, CHUNK), 0)
    s_idx = jax.lax.broadcasted_iota(jnp.int32, (CHUNK, CHUNK), 1)
    for h in range(HEADS):
        _mlstm_unit(h, 0, qf_ref, ktf_ref, vf_ref, caf_ref, cbf_ref, raf_ref, laf_ref, lbf_ref,
                    hf_ref, cn_ref, m_ref, s_idx <= t_idx)
        _mlstm_unit(h, 1, qb_ref, ktb_ref, vb_ref, cab_ref, cbb_ref, rab_ref, lab_ref, lbb_ref,
                    hb_ref, cn_ref, m_ref, s_idx >= t_idx)


def _mlstm(q, kt, v, cola, colb, rowa, lasta, lastb, cn0, m0):
    bsz, t, _ = q.shape
    nc = t // CHUNK

    def specs(rev):
        cj = (lambda j: nc - 1 - j) if rev else (lambda j: j)
        d = 1 if rev else 0
        return [pl.BlockSpec((1, CHUNK, QK_W), lambda b, j: (b, cj(j), 0)),
                pl.BlockSpec((1, QK_W, CHUNK), lambda b, j: (b, 0, cj(j))),
                pl.BlockSpec((1, CHUNK, M_WIDTH), lambda b, j: (b, cj(j), 0)),
                pl.BlockSpec((1, CHUNK, LANES), lambda b, j: (b, cj(j), 0)),
                pl.BlockSpec((1, CHUNK, LANES), lambda b, j: (b, cj(j), 0)),
                pl.BlockSpec((1, HEADS, CHUNK), lambda b, j: (b, d, cj(j))),
                pl.BlockSpec((1, 1, HEADS, LANES), lambda b, j: (b, cj(j), d, 0)),
                pl.BlockSpec((1, 1, HEADS, LANES), lambda b, j: (b, cj(j), d, 0))]

    cn_spec = pl.BlockSpec((1, UNITS, DQK, 2 * DV), lambda b, j: (b, 0, 0, 0))
    m_spec = pl.BlockSpec((1, UNITS, LANES), lambda b, j: (b, 0, 0))
    args = (q, kt, v, cola, colb, rowa, lasta, lastb)
    return pl.pallas_call(
        _mlstm_kernel,
        grid=(bsz, nc),
        in_specs=specs(False) + specs(True) + [cn_spec, m_spec],
        out_specs=[pl.BlockSpec((1, CHUNK, M_WIDTH), lambda b, j: (b, j, 0)),
                   pl.BlockSpec((1, CHUNK, M_WIDTH), lambda b, j: (b, nc - 1 - j, 0)),
                   cn_spec, m_spec],
        out_shape=[jax.ShapeDtypeStruct((bsz, t, M_WIDTH), F32),
                   jax.ShapeDtypeStruct((bsz, t, M_WIDTH), F32),
                   jax.ShapeDtypeStruct((bsz, UNITS, DQK, 2 * DV), F32),
                   jax.ShapeDtypeStruct((bsz, UNITS, LANES), F32)],
        compiler_params=_cparams("parallel", "arbitrary"),
        name="mlstm",
    )(*args, *args, cn0, m0)


def _group_linear(yr, wf_ref, bf_ref):
    outs = []
    for g in range(GROUPS):
        z = jnp.dot(yr[:, g * GC:(g + 1) * GC].astype(BF16), wf_ref[g], preferred_element_type=F32)
        outs.append(z + bf_ref[g:g + 1, :])
    return jnp.concatenate(outs, axis=1)


def _fourier_dense_kernel(xr_ref, xi_ref, cst_ref, wf_ref, bf_ref, o_ref):
    xx = jnp.concatenate([xr_ref[0], xi_ref[0]], axis=0)
    yr = jnp.dot(cst_ref[...], xx, preferred_element_type=F32)
    o_ref[0] = _group_linear(yr, wf_ref, bf_ref).astype(BF16)


def _fourier_dense(xr, xi, cst, wf, bfno):
    bsz, t, w = xr.shape
    blk = pl.BlockSpec((1, t, w), lambda b: (b, 0, 0))
    return pl.pallas_call(
        _fourier_dense_kernel,
        grid=(bsz,),
        in_specs=[blk, blk,
                  pl.BlockSpec((t, 2 * t), lambda b: (0, 0)),
                  pl.BlockSpec((GROUPS, GC, GC), lambda b: (0, 0, 0)),
                  pl.BlockSpec((GROUPS, GC), lambda b: (0, 0))],
        out_specs=blk,
        out_shape=jax.ShapeDtypeStruct((bsz, t, w), BF16),
        compiler_params=_cparams("parallel"),
        name="fourier_dense",
    )(xr, xi, cst, wf, bfno)


def _fourier_s1_kernel(xr_ref, xi_ref, f1_ref, gr_ref, gi_ref):
    n1 = xr_ref.shape[1]
    xx = jnp.concatenate([xr_ref[0], xi_ref[0]], axis=0)
    g = jnp.dot(f1_ref[...], xx, preferred_element_type=F32)
    gr_ref[0] = g[:n1].astype(BF16)
    gi_ref[0] = g[n1:].astype(BF16)


def _fourier_s3_kernel(gr_ref, gi_ref, m_ref, wf_ref, bf_ref, o_ref):
    kb = gr_ref.shape[1]
    for k in range(kb):
        gg = jnp.concatenate([gr_ref[0, k], gi_ref[0, k]], axis=0)
        yr = jnp.dot(m_ref[k], gg, preferred_element_type=F32)
        o_ref[0, k] = _group_linear(yr, wf_ref, bf_ref).astype(BF16)


def _fourier_two_stage(xr, xi, f1, mtab, wf, bfno, *, n1, cb, kb):
    bsz, t, w = xr.shape
    n2 = t // n1
    cb = min(cb, n2 * w)
    xr2, xi2 = xr.reshape(bsz, n1, n2 * w), xi.reshape(bsz, n1, n2 * w)
    blk = pl.BlockSpec((1, n1, cb), lambda b, c: (b, 0, c))
    gr, gi = pl.pallas_call(
        _fourier_s1_kernel,
        grid=(bsz, n2 * w // cb),
        in_specs=[blk, blk, pl.BlockSpec((2 * n1, 2 * n1), lambda b, c: (0, 0))],
        out_specs=[blk, blk],
        out_shape=[jax.ShapeDtypeStruct((bsz, n1, n2 * w), BF16)] * 2,
        compiler_params=_cparams("parallel", "parallel"),
        name="fourier_s1",
    )(xr2, xi2, f1)
    gr4, gi4 = gr.reshape(bsz, n1, n2, w), gi.reshape(bsz, n1, n2, w)
    gblk = pl.BlockSpec((1, kb, n2, w), lambda b, k: (b, k, 0, 0))
    o = pl.pallas_call(
        _fourier_s3_kernel,
        grid=(bsz, n1 // kb),
        in_specs=[gblk, gblk,
                  pl.BlockSpec((kb, n2, 2 * n2), lambda b, k: (k, 0, 0)),
                  pl.BlockSpec((GROUPS, GC, GC), lambda b, k: (0, 0, 0)),
                  pl.BlockSpec((GROUPS, GC), lambda b, k: (0, 0))],
        out_specs=gblk,
        out_shape=jax.ShapeDtypeStruct((bsz, n1, n2, w), BF16),
        compiler_params=_cparams("parallel", "parallel"),
        name="fourier_s3",
    )(gr4, gi4, mtab, wf, bfno)
    return jnp.swapaxes(o, 1, 2).reshape(bsz, t, w)


def _dft_consts(t):
    ang = 2.0 * np.pi / GC * np.outer(np.arange(GC), np.arange(GC))
    cs = np.concatenate([np.cos(ang), -np.sin(ang)], axis=1) / np.sqrt(GC)
    out = {"cs": cs.astype(np.float32)}
    if t <= 2 * CHUNK:
        ang = 2.0 * np.pi / t * np.mod(np.outer(np.arange(t), np.arange(t)), t)
        out["dense"] = (np.concatenate([np.cos(ang), np.sin(ang)], axis=1) / np.sqrt(t)).astype(np.float32)
    else:
        n1 = CHUNK
        n2 = t // n1
        ang = 2.0 * np.pi / n1 * np.mod(np.outer(np.arange(n1), np.arange(n1)), n1)
        c, s = np.cos(ang), np.sin(ang)
        out["f1"] = (np.block([[c, s], [-s, c]]) / np.sqrt(n1)).astype(np.float32)
        k = np.arange(n1)[:, None, None] + n1 * np.arange(n2)[None, :, None]
        ang = 2.0 * np.pi / t * np.mod(k * np.arange(n2)[None, None, :], t)
        out["mtab"] = (np.concatenate([np.cos(ang), np.sin(ang)], axis=2) / np.sqrt(n2)).astype(np.float32)
    return out


def _outproj_kernel(hf_ref, hb_ref, o_ref, za_ref, u_ref, vs_ref, zb_ref, ym_ref, zc_ref, x_ref, gt_ref,
                    wout_ref, ghn_ref, gsgu_ref, gpost_ref, wsp_ref, bsp_ref, xo_ref, y_scr):
    tm = x_ref.shape[1]
    for h in range(HEADS):
        sl = slice(h * DV, (h + 1) * DV)
        hh = hf_ref[0, :, sl] + hb_ref[0, :, sl]
        hn = hh * jax.lax.rsqrt(jnp.mean(hh * hh, axis=-1, keepdims=True) + EPS) * ghn_ref[:, sl]
        ya = hn * _sigmoid(o_ref[0, :, sl].astype(F32)) * _silu(za_ref[0, :, sl].astype(F32))
        y_scr[:, sl] = ya.astype(BF16)
    vs = vs_ref[0].astype(F32)
    vn = (vs * jax.lax.rsqrt(jnp.mean(vs * vs, axis=-1, keepdims=True) + EPS) * gsgu_ref[...]).astype(BF16)
    for c in range(tm // CHUNK):
        rows = slice(c * CHUNK, (c + 1) * CHUNK)
        for g in range(GROUPS):
            cols = slice(g * GC, (g + 1) * GC)
            mixed = jnp.dot(wsp_ref[g], vn[rows, cols], preferred_element_type=F32) + bsp_ref[:, cols]
            yb = u_ref[0, rows, cols].astype(F32) * mixed * _silu(zb_ref[0, rows, cols].astype(F32))
            y_scr[rows, M_WIDTH + g * GC:M_WIDTH + (g + 1) * GC] = yb.astype(BF16)
    yc = ym_ref[0].astype(F32) * _silu(zc_ref[0].astype(F32))
    y_scr[:, M_WIDTH + S_WIDTH:] = yc.astype(BF16)
    out = jnp.dot(y_scr[...], wout_ref[...], preferred_element_type=F32)
    on = out * jax.lax.rsqrt(jnp.mean(out * out, axis=-1, keepdims=True) + EPS) * gpost_ref[...]
    xo_ref[0] = x_ref[0] + gt_ref[0] * on


def _outproj(hf, hb, o, za, u, vs, zb, ym, zc, x, gt, w_out, ghn, gsgu, gpost, wsp, bsp, *, tm):
    bsz, t, d = x.shape
    wide = pl.BlockSpec((1, tm, M_WIDTH), lambda b, i: (b, i, 0))
    half = pl.BlockSpec((1, tm, S_WIDTH), lambda b, i: (b, i, 0))
    const2 = lambda shape: pl.BlockSpec(shape, lambda b, i: (0,) * len(shape))
    return pl.pallas_call(
        _outproj_kernel,
        grid=(bsz, t // tm),
        in_specs=[wide, wide, wide, wide, half, half, half, half, half, wide,
                  pl.BlockSpec((1, 1, d), lambda b, i: (b, 0, 0)),
                  const2((2 * D_MODEL, d)), const2((1, M_WIDTH)), const2((1, S_WIDTH)), const2((1, d)),
                  const2((GROUPS, CHUNK, CHUNK)), const2((CHUNK, S_WIDTH))],
        out_specs=wide,
        out_shape=jax.ShapeDtypeStruct((bsz, t, d), F32),
        scratch_shapes=[pltpu.VMEM((tm, 2 * D_MODEL), BF16)],
        compiler_params=_cparams("parallel", "parallel"),
        name="outproj",
    )(hf, hb, o, za, u, vs, zb, ym, zc, x, gt, w_out, ghn, gsgu, gpost, wsp, bsp)


def _layer_weights(w_in_l, b_gate_l):
    c = np.cumsum([0, QK_W, QK_W, M_WIDTH, 4 * HEADS, M_WIDTH, M_WIDTH, S_WIDTH, S_WIDTH, S_WIDTH, F_WIDTH, F_WIDTH])
    q, k, v, gates, o, za, u, vs, zb, f, zc = (w_in_l[:, c[i]:c[i + 1]] for i in range(11))
    pad = jnp.zeros((w_in_l.shape[0], LANES - UNITS), w_in_l.dtype)
    g_i = jnp.concatenate([gates[:, 0:8], gates[:, 16:24], pad], axis=1)
    g_f = jnp.concatenate([gates[:, 8:16], gates[:, 24:32], pad], axis=1)
    w_nat = jnp.concatenate([q * (DQK ** -0.5), v, g_i, g_f, o, za, u, vs, zb, f, zc], axis=1).astype(BF16)
    w_kt = k.T.astype(BF16)
    bpad = jnp.zeros((LANES - UNITS,), b_gate_l.dtype)
    b_i = jnp.concatenate([b_gate_l[0:8], b_gate_l[16:24], bpad]).reshape(1, LANES)
    b_f = jnp.concatenate([b_gate_l[8:16], b_gate_l[24:32], bpad]).reshape(1, LANES)
    return w_nat, w_kt, b_i, b_f


def kernel(x, c, ctx, c_ctx, w_mod, b_mod, g_pre, g_post, w_in, b_gate, g_hnorm, g_sgu, w_sp, b_sp, w_fno, b_fno, w_out):
    bsz, t_lat, d = x.shape
    t_ctx = ctx.shape[1]
    depth = w_mod.shape[0]
    assert d == D_MODEL and t_lat % (2 * CHUNK) == 0 and t_ctx % (2 * CHUNK) == 0 and bsz + 1 <= 8

    cc = jnp.concatenate([c, c_ctx[None, :], jnp.zeros((8 - bsz - 1, d), c.dtype)], axis=0)
    mod = _modulation(cc, w_mod, b_mod)

    consts_lat, consts_ctx = _dft_consts(t_lat), _dft_consts(t_ctx)
    cs = jnp.asarray(consts_lat["cs"]).astype(BF16)

    def fourier(xr, xi, consts, wf, bfno):
        if "dense" in consts:
            return _fourier_dense(xr, xi, jnp.asarray(consts["dense"]).astype(BF16), wf, bfno)
        return _fourier_two_stage(xr, xi, jnp.asarray(consts["f1"]).astype(BF16),
                                  jnp.asarray(consts["mtab"]).astype(BF16), wf, bfno,
                                  n1=CHUNK, cb=8 * F_WIDTH, kb=8)

    cn_zero = jnp.zeros((bsz, UNITS, DQK, 2 * DV), F32)
    m_zero = jnp.zeros((bsz, UNITS, LANES), F32)
    xc = ctx
    for l in range(depth):
        sh_l, sc_l, gt_l = (mod[l, :bsz, i * d:(i + 1) * d].reshape(bsz, 1, d) for i in range(3))
        sh_c, sc_c, gt_c = (jnp.broadcast_to(mod[l, bsz, i * d:(i + 1) * d].reshape(1, 1, d), (bsz, 1, d))
                            for i in range(3))
        w_nat, w_kt, b_i, b_f = _layer_weights(w_in[l], b_gate[l])
        gpre = g_pre[l].reshape(1, d)
        wf = w_fno[l].astype(BF16)
        wsp = w_sp[l].astype(BF16)
        bsp = jnp.broadcast_to(b_sp[l].T[:, :, None], (CHUNK, GROUPS, GC)).reshape(CHUNK, S_WIDTH)
        wo = w_out[l].astype(BF16)
        tail = (wo, g_hnorm[l].reshape(1, M_WIDTH), g_sgu[l].reshape(1, S_WIDTH), g_post[l].reshape(1, d), wsp, bsp)

        def mixer_front(xs, sc, sh, full, tm):
            n_nat = (sum(w for _, w in _FULL_PIECES) if full else sum(w for _, w in _STATE_PIECES))
            p = _inproj(xs, sc, sh, gpre, w_nat[:, :n_nat], w_kt, cs, full=full, tm=tm)
            gp = _gateprep(p["gI"], p["gF"], b_i, b_f)
            return p, gp

        last = l == depth - 1
        pc, gpc = mixer_front(xc, sc_c, sh_c, not last, 256)
        hf_c, hb_c, cn_c, m_c = _mlstm(pc["q"], pc["kT"], pc["v"], *gpc, cn_zero, m_zero)
        p, gp = mixer_front(x, sc_l, sh_l, True, 512)
        hf, hb, _, _ = _mlstm(p["q"], p["kT"], p["v"], *gp, cn_c, m_c)
        ym = fourier(p["xr"], p["xi"], consts_lat, wf, b_fno[l])
        x = _outproj(hf, hb, p["o"], p["za"], p["u"], p["vs"], p["zb"], ym, p["zc"], x, gt_l, *tail, tm=256)
        if not last:
            ymc = fourier(pc["xr"], pc["xi"], consts_ctx, wf, b_fno[l])
            xc = _outproj(hf_c, hb_c, pc["o"], pc["za"], pc["u"], pc["vs"], pc["zb"], ymc, pc["zc"], xc, gt_c,
                          *tail, tm=256)
    return x
```

```python
import functools

import numpy as np
import jax
import jax.numpy as jnp
from jax.experimental import pallas as pl
from jax.experimental.pallas import tpu as pltpu

EPS = 1e-6
D_MODEL = 1024
HEADS = 8
DV = 128
DQK = 64
QK_W = HEADS * DQK
M_WIDTH = HEADS * DV
CHUNK = 128
S_WIDTH = 512
F_WIDTH = 512
GROUPS = 4
GC = 128
UNITS = 2 * HEADS
LANES = 128

VMEM_LIMIT = 56 * 1024 * 1024

F32 = jnp.float32
BF16 = jnp.bfloat16


def _cparams(*sem):
    return pltpu.CompilerParams(dimension_semantics=sem, vmem_limit_bytes=VMEM_LIMIT)


def _sigmoid(x):
    return 1.0 / (1.0 + jnp.exp(-x))


def _silu(x):
    return x * _sigmoid(x)


def _mod_kernel(cc_ref, w_ref, b_ref, o_ref):
    s = _silu(cc_ref[...]).astype(BF16)
    w = w_ref[0].astype(BF16)
    o_ref[0] = jnp.dot(s, w, preferred_element_type=F32) + b_ref[0]


def _modulation(cc, w_mod, b_mod):
    depth, d, d3 = w_mod.shape
    return pl.pallas_call(
        _mod_kernel,
        grid=(depth,),
        in_specs=[pl.BlockSpec((8, d), lambda l: (0, 0)),
                  pl.BlockSpec((1, d, d3), lambda l: (l, 0, 0)),
                  pl.BlockSpec((1, 1, d3), lambda l: (l, 0, 0))],
        out_specs=pl.BlockSpec((1, 8, d3), lambda l: (l, 0, 0)),
        out_shape=jax.ShapeDtypeStruct((depth, 8, d3), F32),
        compiler_params=_cparams("arbitrary"),
        name="modulation",
    )(cc, w_mod, b_mod.reshape(depth, 1, d3))


def _log_sigmoid(x):
    return jnp.minimum(x, 0.0) - jnp.log1p(jnp.exp(-jnp.abs(x)))


def _gate_scans(gi, gf, cola_ref, colb_ref, rowa_ref):
    tm = gi.shape[0]
    row = jax.lax.broadcasted_iota(jnp.int32, (CHUNK, LANES), 0)
    fwd = jax.lax.broadcasted_iota(jnp.int32, (CHUNK, LANES), 1) < HEADS
    lf = _log_sigmoid(gf)

    def scan(x, op):
        pre, suf = x, x
        k = 1
        while k < CHUNK:
            sh = pltpu.roll(pre, k, axis=0)
            pre = jnp.where(row >= k, op(pre, sh), pre)
            sh = pltpu.roll(suf, CHUNK - k, axis=0)
            suf = jnp.where(row < CHUNK - k, op(suf, sh), suf)
            k *= 2
        return jnp.where(fwd, pre, suf)

    for c in range(tm // CHUNK):
        rows = slice(c * CHUNK, (c + 1) * CHUNK)
        b = scan(lf[rows], jnp.add)
        a = gi[rows] - b
        cola_ref[0, rows, :] = scan(a, jnp.maximum)
        colb_ref[0, rows, :] = b
        rowa_ref[0, :, rows] = a.T[0:UNITS, :]


def _inproj_kernel(x_ref, sc_ref, sh_ref, g_ref, wn_ref, wkt_ref, cs_ref, bi_ref, bf_ref, *out_refs, pieces):
    x = x_ref[0]
    y = x * jax.lax.rsqrt(jnp.mean(x * x, axis=-1, keepdims=True) + EPS)
    h = (y * g_ref[...]) * (1.0 + sc_ref[0]) + sh_ref[0]
    hb = h.astype(BF16)
    oi = 0
    for name, off, width in pieces:
        if name == "kT":
            r = jax.lax.dot_general(wkt_ref[...], hb, (((1,), (1,)), ((), ())),
                                    preferred_element_type=F32)
            out_refs[oi][0] = r.astype(BF16)
            oi += 1
            continue
        r = jnp.dot(hb, wn_ref[:, off:off + width], preferred_element_type=F32)
        if name == "gates":
            _gate_scans(r[:, :LANES] + bi_ref[...], r[:, LANES:] + bf_ref[...], *out_refs[oi:oi + 3])
            oi += 3
        elif name == "f":
            fb = r.astype(BF16)
            xr_ref, xi_ref = out_refs[oi], out_refs[oi + 1]
            for g in range(GROUPS):
                z = jnp.dot(fb[:, g * GC:(g + 1) * GC], cs_ref[...], preferred_element_type=F32)
                xr_ref[0, :, g * GC:(g + 1) * GC] = z[:, :GC].astype(BF16)
                xi_ref[0, :, g * GC:(g + 1) * GC] = z[:, GC:].astype(BF16)
            oi += 2
        else:
            out_refs[oi][0] = r.astype(BF16)
            oi += 1


_FULL_PIECES = (("q", QK_W), ("v", M_WIDTH), ("gates", 2 * LANES), ("o", M_WIDTH), ("za", M_WIDTH),
                ("u", S_WIDTH), ("vs", S_WIDTH), ("zb", S_WIDTH), ("f", F_WIDTH), ("zc", F_WIDTH))
_STATE_PIECES = _FULL_PIECES[:3]


def _inproj(x, sc, sh, g_pre, w_nat, w_kt, cs, b_i, b_f, *, full, tm):
    bsz, t, d = x.shape
    names = _FULL_PIECES if full else _STATE_PIECES
    row_spec = lambda width: pl.BlockSpec((1, tm, width), lambda b, i: (b, i, 0))
    pieces, off = [], 0
    out_names, out_shapes, out_specs = [], [], []
    for name, width in names:
        pieces.append((name, off, width))
        off += width
        if name == "gates":
            out_names += ["cola", "colb", "rowa"]
            out_shapes += [jax.ShapeDtypeStruct((bsz, t, LANES), F32)] * 2 + [jax.ShapeDtypeStruct((bsz, UNITS, t), F32)]
            out_specs += [row_spec(LANES), row_spec(LANES), pl.BlockSpec((1, UNITS, tm), lambda b, i: (b, 0, i))]
        elif name == "f":
            out_names += ["xr", "xi"]
            out_shapes += [jax.ShapeDtypeStruct((bsz, t, width), BF16)] * 2
            out_specs += [row_spec(width)] * 2
        else:
            out_names.append(name)
            out_shapes.append(jax.ShapeDtypeStruct((bsz, t, width), BF16))
            out_specs.append(row_spec(width))
    n_nat = off
    pieces.append(("kT", 0, QK_W))
    out_names.append("kT")
    out_shapes.append(jax.ShapeDtypeStruct((bsz, QK_W, t), BF16))
    out_specs.append(pl.BlockSpec((1, QK_W, tm), lambda b, i: (b, 0, i)))
    const = lambda shape, **kw: pl.BlockSpec(shape, lambda b, i: (0,) * len(shape), **kw)
    outs = pl.pallas_call(
        functools.partial(_inproj_kernel, pieces=tuple(pieces)),
        grid=(bsz, t // tm),
        in_specs=[pl.BlockSpec((1, tm, d), lambda b, i: (b, i, 0)),
                  pl.BlockSpec((1, 1, d), lambda b, i: (b, 0, 0)),
                  pl.BlockSpec((1, 1, d), lambda b, i: (b, 0, 0)),
                  const((1, d)),
                  const((d, n_nat), pipeline_mode=pl.Buffered(1)),
                  const((QK_W, d), pipeline_mode=pl.Buffered(1)),
                  const((GC, 2 * GC)), const((1, LANES)), const((1, LANES))],
        out_specs=out_specs,
        out_shape=out_shapes,
        compiler_params=_cparams("parallel", "parallel"),
        name="inproj_full" if full else "inproj_state",
    )(x, sc, sh, g_pre, w_nat, w_kt, cs, b_i, b_f)
    return dict(zip(out_names, outs))


def _mlstm_unit(h, d, q_ref, kt_ref, v_ref, cola_ref, colb_ref, rowa_ref, hout_ref, cn_ref, m_ref, mask):
    i = d * HEADS + h
    pair = h // 2
    end = CHUNK - 1 if d == 0 else 0
    q_pair = q_ref[0, :, pair * LANES:(pair + 1) * LANES]
    kt_h = kt_ref[0, h * DQK:(h + 1) * DQK, :]
    zk = jnp.zeros((DQK, CHUNK), BF16)
    kt_ext = jnp.concatenate([kt_h, zk] if h % 2 == 0 else [zk, kt_h], axis=0)
    v_h = v_ref[0, :, h * DV:(h + 1) * DV]
    vaug = jnp.concatenate([v_h, jnp.ones((CHUNK, DV), BF16)], axis=1)
    a_col = jnp.broadcast_to(cola_ref[0, :, i:i + 1], (CHUNK, CHUNK))
    b_col = jnp.broadcast_to(colb_ref[0, :, i:i + 1], (CHUNK, CHUNK))
    a_last = jnp.broadcast_to(cola_ref[0, end:end + 1, i:i + 1], (1, LANES))
    b_last = jnp.broadcast_to(colb_ref[0, end:end + 1, i:i + 1], (1, LANES))
    a_row = rowa_ref[0, h:h + 1, :]
    m_row = m_ref[0, i:i + 1, :]
    cn = cn_ref[0, i]

    g = jnp.maximum(m_row, a_col)
    dmat = jnp.where(mask, jnp.exp(a_row - g), 0.0)
    s = jnp.dot(q_pair, kt_ext, preferred_element_type=F32)
    p = (s * dmat).astype(BF16)
    qs = (q_pair.astype(F32) * jnp.exp(m_row - g)).astype(BF16)
    lhs = jnp.concatenate([p, qs], axis=1)
    zc = jnp.zeros((DQK, 2 * DV), BF16)
    cnb = cn.astype(BF16)
    cn_ext = jnp.concatenate([cnb, zc] if h % 2 == 0 else [zc, cnb], axis=0)
    rhs = jnp.concatenate([vaug, cn_ext], axis=0)
    out = jnp.dot(lhs, rhs, preferred_element_type=F32)
    num, den = out[:, :DV], out[:, DV:]
    hval = num / jnp.maximum(jnp.abs(den), jnp.exp(-(b_col + g)))
    hout_ref[0, :, h * DV:(h + 1) * DV] = hval.astype(hout_ref.dtype)

    g_end = jnp.maximum(m_row, a_last)
    kts = (kt_h.astype(F32) * jnp.exp(a_row - g_end)).astype(BF16)
    upd = jnp.dot(kts, vaug, preferred_element_type=F32)
    decay = jnp.exp(m_row - g_end)
    cn_ref[0, i] = jnp.concatenate([decay, decay], axis=1) * cn + upd
    m_ref[0, i:i + 1, :] = b_last + g_end


def _mlstm_kernel(qf_ref, ktf_ref, vf_ref, caf_ref, cbf_ref, raf_ref,
                  qb_ref, ktb_ref, vb_ref, cab_ref, cbb_ref, rab_ref,
                  cn0_ref, m0_ref, hf_ref, hb_ref, cn_ref, m_ref):
    @pl.when(pl.program_id(1) == 0)
    def _():
        cn_ref[...] = cn0_ref[...]
        m_ref[...] = m0_ref[...]

    t_idx = jax.lax.broadcasted_iota(jnp.int32, (CHUNK, CHUNK), 0)
    s_idx = jax.lax.broadcasted_iota(jnp.int32, (CHUNK, CHUNK), 1)
    for h in range(HEADS):
        _mlstm_unit(h, 0, qf_ref, ktf_ref, vf_ref, caf_ref, cbf_ref, raf_ref, hf_ref, cn_ref, m_ref, s_idx <= t_idx)
        _mlstm_unit(h, 1, qb_ref, ktb_ref, vb_ref, cab_ref, cbb_ref, rab_ref, hb_ref, cn_ref, m_ref, s_idx >= t_idx)


def _mlstm(q, kt, v, cola, colb, rowa, cn0, m0):
    bsz, t, _ = q.shape
    nc = t // CHUNK

    def specs(rev):
        cj = (lambda j: nc - 1 - j) if rev else (lambda j: j)
        d = 1 if rev else 0
        return [pl.BlockSpec((1, CHUNK, QK_W), lambda b, j: (b, cj(j), 0)),
                pl.BlockSpec((1, QK_W, CHUNK), lambda b, j: (b, 0, cj(j))),
                pl.BlockSpec((1, CHUNK, M_WIDTH), lambda b, j: (b, cj(j), 0)),
                pl.BlockSpec((1, CHUNK, LANES), lambda b, j: (b, cj(j), 0)),
                pl.BlockSpec((1, CHUNK, LANES), lambda b, j: (b, cj(j), 0)),
                pl.BlockSpec((1, HEADS, CHUNK), lambda b, j: (b, d, cj(j)))]

    cn_spec = pl.BlockSpec((1, UNITS, DQK, 2 * DV), lambda b, j: (b, 0, 0, 0))
    m_spec = pl.BlockSpec((1, UNITS, LANES), lambda b, j: (b, 0, 0))
    args = (q, kt, v, cola, colb, rowa)
    return pl.pallas_call(
        _mlstm_kernel,
        grid=(bsz, nc),
        in_specs=specs(False) + specs(True) + [cn_spec, m_spec],
        out_specs=[pl.BlockSpec((1, CHUNK, M_WIDTH), lambda b, j: (b, j, 0)),
                   pl.BlockSpec((1, CHUNK, M_WIDTH), lambda b, j: (b, nc - 1 - j, 0)),
                   cn_spec, m_spec],
        out_shape=[jax.ShapeDtypeStruct((bsz, t, M_WIDTH), BF16),
                   jax.ShapeDtypeStruct((bsz, t, M_WIDTH), BF16),
                   jax.ShapeDtypeStruct((bsz, UNITS, DQK, 2 * DV), F32),
                   jax.ShapeDtypeStruct((bsz, UNITS, LANES), F32)],
        compiler_params=_cparams("parallel", "arbitrary"),
        name="mlstm",
    )(*args, *args, cn0, m0)


def _group_linear(yr, wf_ref, bf_ref):
    outs = []
    for g in range(GROUPS):
        z = jnp.dot(yr[:, g * GC:(g + 1) * GC].astype(BF16), wf_ref[g], preferred_element_type=F32)
        outs.append(z + bf_ref[g:g + 1, :])
    return jnp.concatenate(outs, axis=1)


def _fourier_dense_kernel(xr_ref, xi_ref, cst_ref, wf_ref, bf_ref, o_ref):
    xx = jnp.concatenate([xr_ref[0], xi_ref[0]], axis=0)
    yr = jnp.dot(cst_ref[...], xx, preferred_element_type=F32)
    o_ref[0] = _group_linear(yr, wf_ref, bf_ref).astype(BF16)


def _fourier_dense(xr, xi, cst, wf, bfno):
    bsz, t, w = xr.shape
    blk = pl.BlockSpec((1, t, w), lambda b: (b, 0, 0))
    return pl.pallas_call(
        _fourier_dense_kernel,
        grid=(bsz,),
        in_specs=[blk, blk,
                  pl.BlockSpec((t, 2 * t), lambda b: (0, 0)),
                  pl.BlockSpec((GROUPS, GC, GC), lambda b: (0, 0, 0)),
                  pl.BlockSpec((GROUPS, GC), lambda b: (0, 0))],
        out_specs=blk,
        out_shape=jax.ShapeDtypeStruct((bsz, t, w), BF16),
        compiler_params=_cparams("parallel"),
        name="fourier_dense",
    )(xr, xi, cst, wf, bfno)


def _fourier_s1_kernel(xr_ref, xi_ref, f1_ref, gr_ref, gi_ref):
    n1 = xr_ref.shape[1]
    xx = jnp.concatenate([xr_ref[0], xi_ref[0]], axis=0)
    g = jnp.dot(f1_ref[...], xx, preferred_element_type=F32)
    gr_ref[0] = g[:n1].astype(BF16)
    gi_ref[0] = g[n1:].astype(BF16)


def _fourier_s3_kernel(gr_ref, gi_ref, m_ref, wf_ref, bf_ref, o_ref):
    kb = gr_ref.shape[1]
    for k in range(kb):
        gg = jnp.concatenate([gr_ref[0, k], gi_ref[0, k]], axis=0)
        yr = jnp.dot(m_ref[k], gg, preferred_element_type=F32)
        o_ref[0, k] = _group_linear(yr, wf_ref, bf_ref).astype(BF16)


def _fourier_two_stage(xr, xi, f1, mtab, wf, bfno, *, n1, cb, kb):
    bsz, t, w = xr.shape
    n2 = t // n1
    cb = min(cb, n2 * w)
    xr2, xi2 = xr.reshape(bsz, n1, n2 * w), xi.reshape(bsz, n1, n2 * w)
    blk = pl.BlockSpec((1, n1, cb), lambda b, c: (b, 0, c))
    gr, gi = pl.pallas_call(
        _fourier_s1_kernel,
        grid=(bsz, n2 * w // cb),
        in_specs=[blk, blk, pl.BlockSpec((2 * n1, 2 * n1), lambda b, c: (0, 0))],
        out_specs=[blk, blk],
        out_shape=[jax.ShapeDtypeStruct((bsz, n1, n2 * w), BF16)] * 2,
        compiler_params=_cparams("parallel", "parallel"),
        name="fourier_s1",
    )(xr2, xi2, f1)
    gr4, gi4 = gr.reshape(bsz, n1, n2, w), gi.reshape(bsz, n1, n2, w)
    gblk = pl.BlockSpec((1, kb, n2, w), lambda b, k: (b, k, 0, 0))
    o = pl.pallas_call(
        _fourier_s3_kernel,
        grid=(bsz, n1 // kb),
        in_specs=[gblk, gblk,
                  pl.BlockSpec((kb, n2, 2 * n2), lambda b, k: (k, 0, 0)),
                  pl.BlockSpec((GROUPS, GC, GC), lambda b, k: (0, 0, 0)),
                  pl.BlockSpec((GROUPS, GC), lambda b, k: (0, 0))],
        out_specs=gblk,
        out_shape=jax.ShapeDtypeStruct((bsz, n1, n2, w), BF16),
        compiler_params=_cparams("parallel", "parallel"),
        name="fourier_s3",
    )(gr4, gi4, mtab, wf, bfno)
    return jnp.swapaxes(o, 1, 2).reshape(bsz, t, w)


def _dft_consts(t):
    ang = 2.0 * np.pi / GC * np.outer(np.arange(GC), np.arange(GC))
    cs = np.concatenate([np.cos(ang), -np.sin(ang)], axis=1) / np.sqrt(GC)
    out = {"cs": cs.astype(np.float32)}
    if t <= 2 * CHUNK:
        ang = 2.0 * np.pi / t * np.mod(np.outer(np.arange(t), np.arange(t)), t)
        out["dense"] = (np.concatenate([np.cos(ang), np.sin(ang)], axis=1) / np.sqrt(t)).astype(np.float32)
    else:
        n1 = CHUNK
        n2 = t // n1
        ang = 2.0 * np.pi / n1 * np.mod(np.outer(np.arange(n1), np.arange(n1)), n1)
        c, s = np.cos(ang), np.sin(ang)
        out["f1"] = (np.block([[c, s], [-s, c]]) / np.sqrt(n1)).astype(np.float32)
        k = np.arange(n1)[:, None, None] + n1 * np.arange(n2)[None, :, None]
        ang = 2.0 * np.pi / t * np.mod(k * np.arange(n2)[None, None, :], t)
        out["mtab"] = (np.concatenate([np.cos(ang), np.sin(ang)], axis=2) / np.sqrt(n2)).astype(np.float32)
    return out


def _outproj_kernel(hf_ref, hb_ref, o_ref, za_ref, u_ref, vs_ref, zb_ref, ym_ref, zc_ref, x_ref, gt_ref,
                    wout_ref, ghn_ref, gsgu_ref, gpost_ref, wsp_ref, bsp_ref, xo_ref, y_scr):
    tm = x_ref.shape[1]
    for h in range(HEADS):
        sl = slice(h * DV, (h + 1) * DV)
        hh = hf_ref[0, :, sl].astype(F32) + hb_ref[0, :, sl].astype(F32)
        hn = hh * jax.lax.rsqrt(jnp.mean(hh * hh, axis=-1, keepdims=True) + EPS) * ghn_ref[:, sl]
        ya = hn * _sigmoid(o_ref[0, :, sl].astype(F32)) * _silu(za_ref[0, :, sl].astype(F32))
        y_scr[:, sl] = ya.astype(BF16)
    vs = vs_ref[0].astype(F32)
    vn = (vs * jax.lax.rsqrt(jnp.mean(vs * vs, axis=-1, keepdims=True) + EPS) * gsgu_ref[...]).astype(BF16)
    for c in range(tm // CHUNK):
        rows = slice(c * CHUNK, (c + 1) * CHUNK)
        for g in range(GROUPS):
            cols = slice(g * GC, (g + 1) * GC)
            mixed = jnp.dot(wsp_ref[g], vn[rows, cols], preferred_element_type=F32) + bsp_ref[:, cols]
            yb = u_ref[0, rows, cols].astype(F32) * mixed * _silu(zb_ref[0, rows, cols].astype(F32))
            y_scr[rows, M_WIDTH + g * GC:M_WIDTH + (g + 1) * GC] = yb.astype(BF16)
    yc = ym_ref[0].astype(F32) * _silu(zc_ref[0].astype(F32))
    y_scr[:, M_WIDTH + S_WIDTH:] = yc.astype(BF16)
    out = jnp.dot(y_scr[...], wout_ref[...], preferred_element_type=F32)
    on = out * jax.lax.rsqrt(jnp.mean(out * out, axis=-1, keepdims=True) + EPS) * gpost_ref[...]
    xo_ref[0] = x_ref[0] + gt_ref[0] * on


def _outproj(hf, hb, o, za, u, vs, zb, ym, zc, x, gt, w_out, ghn, gsgu, gpost, wsp, bsp, *, tm):
    bsz, t, d = x.shape
    wide = pl.BlockSpec((1, tm, M_WIDTH), lambda b, i: (b, i, 0))
    half = pl.BlockSpec((1, tm, S_WIDTH), lambda b, i: (b, i, 0))
    const2 = lambda shape: pl.BlockSpec(shape, lambda b, i: (0,) * len(shape))
    return pl.pallas_call(
        _outproj_kernel,
        grid=(bsz, t // tm),
        in_specs=[wide, wide, wide, wide, half, half, half, half, half, wide,
                  pl.BlockSpec((1, 1, d), lambda b, i: (b, 0, 0)),
                  const2((2 * D_MODEL, d)), const2((1, M_WIDTH)), const2((1, S_WIDTH)), const2((1, d)),
                  const2((GROUPS, CHUNK, CHUNK)), const2((CHUNK, S_WIDTH))],
        out_specs=wide,
        out_shape=jax.ShapeDtypeStruct((bsz, t, d), F32),
        scratch_shapes=[pltpu.VMEM((tm, 2 * D_MODEL), BF16)],
        compiler_params=_cparams("parallel", "parallel"),
        name="outproj",
    )(hf, hb, o, za, u, vs, zb, ym, zc, x, gt, w_out, ghn, gsgu, gpost, wsp, bsp)


def _layer_weights(w_in_l, b_gate_l):
    c = np.cumsum([0, QK_W, QK_W, M_WIDTH, 4 * HEADS, M_WIDTH, M_WIDTH, S_WIDTH, S_WIDTH, S_WIDTH, F_WIDTH, F_WIDTH])
    q, k, v, gates, o, za, u, vs, zb, f, zc = (w_in_l[:, c[i]:c[i + 1]] for i in range(11))
    pad = jnp.zeros((w_in_l.shape[0], LANES - UNITS), w_in_l.dtype)
    g_i = jnp.concatenate([gates[:, 0:8], gates[:, 16:24], pad], axis=1)
    g_f = jnp.concatenate([gates[:, 8:16], gates[:, 24:32], pad], axis=1)
    w_nat = jnp.concatenate([q * (DQK ** -0.5), v, g_i, g_f, o, za, u, vs, zb, f, zc], axis=1).astype(BF16)
    w_kt = k.T.astype(BF16)
    bpad = jnp.zeros((LANES - UNITS,), b_gate_l.dtype)
    b_i = jnp.concatenate([b_gate_l[0:8], b_gate_l[16:24], bpad]).reshape(1, LANES)
    b_f = jnp.concatenate([b_gate_l[8:16], b_gate_l[24:32], bpad]).reshape(1, LANES)
    return w_nat, w_kt, b_i, b_f


def kernel(x, c, ctx, c_ctx, w_mod, b_mod, g_pre, g_post, w_in, b_gate, g_hnorm, g_sgu, w_sp, b_sp, w_fno, b_fno, w_out):
    bsz, t_lat, d = x.shape
    t_ctx = ctx.shape[1]
    depth = w_mod.shape[0]
    assert d == D_MODEL and t_lat % (2 * CHUNK) == 0 and t_ctx % (2 * CHUNK) == 0 and bsz + 1 <= 8

    cc = jnp.concatenate([c, c_ctx[None, :], jnp.zeros((8 - bsz - 1, d), c.dtype)], axis=0)
    mod = _modulation(cc, w_mod, b_mod)

    consts_lat, consts_ctx = _dft_consts(t_lat), _dft_consts(t_ctx)
    cs = jnp.asarray(consts_lat["cs"]).astype(BF16)

    def fourier(xr, xi, consts, wf, bfno):
        if "dense" in consts:
            return _fourier_dense(xr, xi, jnp.asarray(consts["dense"]).astype(BF16), wf, bfno)
        return _fourier_two_stage(xr, xi, jnp.asarray(consts["f1"]).astype(BF16),
                                  jnp.asarray(consts["mtab"]).astype(BF16), wf, bfno,
                                  n1=CHUNK, cb=8 * F_WIDTH, kb=8)

    cn_zero = jnp.zeros((bsz, UNITS, DQK, 2 * DV), F32)
    m_zero = jnp.zeros((bsz, UNITS, LANES), F32)
    xc = ctx
    for l in range(depth):
        sh_l, sc_l, gt_l = (mod[l, :bsz, i * d:(i + 1) * d].reshape(bsz, 1, d) for i in range(3))
        sh_c, sc_c, gt_c = (jnp.broadcast_to(mod[l, bsz, i * d:(i + 1) * d].reshape(1, 1, d), (bsz, 1, d))
                            for i in range(3))
        w_nat, w_kt, b_i, b_f = _layer_weights(w_in[l], b_gate[l])
        gpre = g_pre[l].reshape(1, d)
        wf = w_fno[l].astype(BF16)
        wsp = w_sp[l].astype(BF16)
        bsp = jnp.broadcast_to(b_sp[l].T[:, :, None], (CHUNK, GROUPS, GC)).reshape(CHUNK, S_WIDTH)
        wo = w_out[l].astype(BF16)
        tail = (wo, g_hnorm[l].reshape(1, M_WIDTH), g_sgu[l].reshape(1, S_WIDTH), g_post[l].reshape(1, d), wsp, bsp)
        front = functools.partial(_inproj, g_pre=gpre, w_nat=w_nat, w_kt=w_kt, cs=cs, b_i=b_i, b_f=b_f)

        last = l == depth - 1
        pc = front(xc, sc_c, sh_c, full=not last, tm=256)
        hf_c, hb_c, cn_c, m_c = _mlstm(pc["q"], pc["kT"], pc["v"], pc["cola"], pc["colb"], pc["rowa"], cn_zero, m_zero)
        p = front(x, sc_l, sh_l, full=True, tm=512)
        hf, hb, _, _ = _mlstm(p["q"], p["kT"], p["v"], p["cola"], p["colb"], p["rowa"], cn_c, m_c)
        ym = fourier(p["xr"], p["xi"], consts_lat, wf, b_fno[l])
        x = _outproj(hf, hb, p["o"], p["za"], p["u"], p["vs"], p["zb"], ym, p["zc"], x, gt_l, *tail, tm=256)
        if not last:
            ymc = fourier(pc["xr"], pc["xi"], consts_ctx, wf, b_fno[l])
            xc = _outproj(hf_c, hb_c, pc["o"], pc["za"], pc["u"], pc["vs"], pc["zb"], ymc, pc["zc"], xc, gt_c,
                          *tail, tm=256)
    return x
```

```python
import functools

import numpy as np
import jax
import jax.numpy as jnp
from jax.experimental import pallas as pl
from jax.experimental.pallas import tpu as pltpu

EPS = 1e-6
D_MODEL = 1024
HEADS = 8
DV = 128
DQK = 64
QK_W = HEADS * DQK
M_WIDTH = HEADS * DV
CHUNK = 128
S_WIDTH = 512
F_WIDTH = 512
GROUPS = 4
GC = 128
UNITS = 2 * HEADS
LANES = 128

VMEM_LIMIT = 56 * 1024 * 1024

F32 = jnp.float32
BF16 = jnp.bfloat16


def _cparams(*sem):
    return pltpu.CompilerParams(dimension_semantics=sem, vmem_limit_bytes=VMEM_LIMIT)


def _sigmoid(x):
    return 1.0 / (1.0 + jnp.exp(-x))


def _silu(x):
    return x * _sigmoid(x)


def _mod_kernel(cc_ref, w_ref, b_ref, o_ref):
    s = _silu(cc_ref[...]).astype(BF16)
    w = w_ref[0].astype(BF16)
    o_ref[0] = jnp.dot(s, w, preferred_element_type=F32) + b_ref[0]


def _modulation(cc, w_mod, b_mod):
    depth, d, d3 = w_mod.shape
    return pl.pallas_call(
        _mod_kernel,
        grid=(depth,),
        in_specs=[pl.BlockSpec((8, d), lambda l: (0, 0)),
                  pl.BlockSpec((1, d, d3), lambda l: (l, 0, 0)),
                  pl.BlockSpec((1, 1, d3), lambda l: (l, 0, 0))],
        out_specs=pl.BlockSpec((1, 8, d3), lambda l: (l, 0, 0)),
        out_shape=jax.ShapeDtypeStruct((depth, 8, d3), F32),
        compiler_params=_cparams("arbitrary"),
        name="modulation",
    )(cc, w_mod, b_mod.reshape(depth, 1, d3))


def _log_sigmoid(x):
    return jnp.minimum(x, 0.0) - jnp.log1p(jnp.exp(-jnp.abs(x)))


def _gate_scans(gi, gf, cola_ref, colb_ref, rowa_ref):
    tm = gi.shape[0]
    row = jax.lax.broadcasted_iota(jnp.int32, (CHUNK, LANES), 0)
    fwd = jax.lax.broadcasted_iota(jnp.int32, (CHUNK, LANES), 1) < HEADS
    lf = _log_sigmoid(gf)

    def scan(x, op):
        pre, suf = x, x
        k = 1
        while k < CHUNK:
            sh = pltpu.roll(pre, k, axis=0)
            pre = jnp.where(row >= k, op(pre, sh), pre)
            sh = pltpu.roll(suf, CHUNK - k, axis=0)
            suf = jnp.where(row < CHUNK - k, op(suf, sh), suf)
            k *= 2
        return jnp.where(fwd, pre, suf)

    for c in range(tm // CHUNK):
        rows = slice(c * CHUNK, (c + 1) * CHUNK)
        b = scan(lf[rows], jnp.add)
        a = gi[rows] - b
        cola_ref[0, rows, :] = scan(a, jnp.maximum)
        colb_ref[0, rows, :] = b
        rowa_ref[0, :, rows] = a.T[0:UNITS, :]


def _pitch(n):
    p = n
    while (p // 8) % 2 == 0:
        p += 8
    return p


def _store_slabs(ref, g, z, n2):
    pin = _pitch(n2)
    for r in range(z.shape[0] // n2):
        ref[0, g, r * pin:r * pin + n2, :] = z[r * n2:(r + 1) * n2]
        ref[0, g, r * pin + n2:(r + 1) * pin, :] = jnp.zeros((pin - n2, GC), F32)


def _inproj_kernel(x_ref, sc_ref, sh_ref, g_ref, wn_ref, wkt_ref, cst_ref, wf_ref, bi_ref, bf_ref, *out_refs,
                   pieces, slab_n2):
    x = x_ref[0]
    y = x * jax.lax.rsqrt(jnp.mean(x * x, axis=-1, keepdims=True) + EPS)
    h = (y * g_ref[...]) * (1.0 + sc_ref[0]) + sh_ref[0]
    hb = h.astype(BF16)
    oi = 0
    for name, off, width in pieces:
        if name == "kT":
            r = jax.lax.dot_general(wkt_ref[...], hb, (((1,), (1,)), ((), ())),
                                    preferred_element_type=F32)
            out_refs[oi][0] = r.astype(BF16)
            oi += 1
            continue
        r = jnp.dot(hb, wn_ref[:, off:off + width], preferred_element_type=F32)
        if name == "gates":
            _gate_scans(r[:, :LANES] + bi_ref[...], r[:, LANES:] + bf_ref[...], *out_refs[oi:oi + 3])
            oi += 3
        elif name == "f":
            fb = r.astype(BF16)
            xr_ref, xi_ref = out_refs[oi], out_refs[oi + 1]
            for g in range(GROUPS):
                wc = jnp.dot(cst_ref[...], wf_ref[g], preferred_element_type=F32)
                wcat = jnp.concatenate([wc[:GC], wc[GC:]], axis=1).astype(BF16)
                z = jnp.dot(fb[:, g * GC:(g + 1) * GC], wcat, preferred_element_type=F32)
                if slab_n2 is None:
                    xr_ref[0, :, g * GC:(g + 1) * GC] = z[:, :GC].astype(BF16)
                    xi_ref[0, :, g * GC:(g + 1) * GC] = z[:, GC:].astype(BF16)
                else:
                    _store_slabs(xr_ref, g, z[:, :GC], slab_n2)
                    _store_slabs(xi_ref, g, z[:, GC:], slab_n2)
            oi += 2
        else:
            out_refs[oi][0] = r.astype(BF16)
            oi += 1


_FULL_PIECES = (("q", QK_W), ("v", M_WIDTH), ("gates", 2 * LANES), ("o", M_WIDTH), ("za", M_WIDTH),
                ("u", S_WIDTH), ("vs", S_WIDTH), ("zb", S_WIDTH), ("f", F_WIDTH), ("zc", F_WIDTH))
_STATE_PIECES = _FULL_PIECES[:3]


def _inproj(x, sc, sh, g_pre, w_nat, w_kt, cst, wf, b_i, b_f, *, full, tm):
    bsz, t, d = x.shape
    names = _FULL_PIECES if full else _STATE_PIECES
    slab_n2 = t // CHUNK if t > 2 * CHUNK else None
    row_spec = lambda width: pl.BlockSpec((1, tm, width), lambda b, i: (b, i, 0))
    pieces, off = [], 0
    out_names, out_shapes, out_specs = [], [], []
    for name, width in names:
        pieces.append((name, off, width))
        off += width
        if name == "gates":
            out_names += ["cola", "colb", "rowa"]
            out_shapes += [jax.ShapeDtypeStruct((bsz, t, LANES), F32)] * 2 + [jax.ShapeDtypeStruct((bsz, UNITS, t), F32)]
            out_specs += [row_spec(LANES), row_spec(LANES), pl.BlockSpec((1, UNITS, tm), lambda b, i: (b, 0, i))]
        elif name == "f":
            out_names += ["xr", "xi"]
            if slab_n2 is None:
                out_shapes += [jax.ShapeDtypeStruct((bsz, t, width), BF16)] * 2
                out_specs += [row_spec(width)] * 2
            else:
                assert tm % slab_n2 == 0
                pin = _pitch(slab_n2)
                out_shapes += [jax.ShapeDtypeStruct((bsz, GROUPS, CHUNK * pin, GC), F32)] * 2
                out_specs += [pl.BlockSpec((1, GROUPS, tm // slab_n2 * pin, GC), lambda b, i: (b, 0, i, 0))] * 2
        else:
            out_names.append(name)
            out_shapes.append(jax.ShapeDtypeStruct((bsz, t, width), BF16))
            out_specs.append(row_spec(width))
    n_nat = off
    pieces.append(("kT", 0, QK_W))
    out_names.append("kT")
    out_shapes.append(jax.ShapeDtypeStruct((bsz, QK_W, t), BF16))
    out_specs.append(pl.BlockSpec((1, QK_W, tm), lambda b, i: (b, 0, i)))
    const = lambda shape, **kw: pl.BlockSpec(shape, lambda b, i: (0,) * len(shape), **kw)
    outs = pl.pallas_call(
        functools.partial(_inproj_kernel, pieces=tuple(pieces), slab_n2=slab_n2),
        grid=(bsz, t // tm),
        in_specs=[pl.BlockSpec((1, tm, d), lambda b, i: (b, i, 0)),
                  pl.BlockSpec((1, 1, d), lambda b, i: (b, 0, 0)),
                  pl.BlockSpec((1, 1, d), lambda b, i: (b, 0, 0)),
                  const((1, d)),
                  const((d, n_nat), pipeline_mode=pl.Buffered(1)),
                  const((QK_W, d), pipeline_mode=pl.Buffered(1)),
                  const((2 * GC, GC)), const((GROUPS, GC, GC)), const((1, LANES)), const((1, LANES))],
        out_specs=out_specs,
        out_shape=out_shapes,
        compiler_params=_cparams("parallel", "parallel"),
        name="inproj_full" if full else "inproj_state",
    )(x, sc, sh, g_pre, w_nat, w_kt, cst, wf, b_i, b_f)
    return dict(zip(out_names, outs))


def _mlstm_unit(h, d, q_ref, kt_ref, v_ref, cola_ref, colb_ref, rowa_ref, hout_ref, cn_ref, m_ref, mask):
    i = d * HEADS + h
    pair = h // 2
    end = CHUNK - 1 if d == 0 else 0
    q_pair = q_ref[0, :, pair * LANES:(pair + 1) * LANES]
    kt_h = kt_ref[0, h * DQK:(h + 1) * DQK, :]
    zk = jnp.zeros((DQK, CHUNK), BF16)
    kt_ext = jnp.concatenate([kt_h, zk] if h % 2 == 0 else [zk, kt_h], axis=0)
    v_h = v_ref[0, :, h * DV:(h + 1) * DV]
    vaug = jnp.concatenate([v_h, jnp.ones((CHUNK, DV), BF16)], axis=1)
    a_col = jnp.broadcast_to(cola_ref[0, :, i:i + 1], (CHUNK, CHUNK))
    b_col = jnp.broadcast_to(colb_ref[0, :, i:i + 1], (CHUNK, CHUNK))
    a_last = jnp.broadcast_to(cola_ref[0, end:end + 1, i:i + 1], (1, LANES))
    b_last = jnp.broadcast_to(colb_ref[0, end:end + 1, i:i + 1], (1, LANES))
    a_row = rowa_ref[0, h:h + 1, :]
    m_row = m_ref[0, i:i + 1, :]
    cn = cn_ref[0, i]

    g = jnp.maximum(m_row, a_col)
    dmat = jnp.where(mask, jnp.exp(a_row - g), 0.0)
    s = jnp.dot(q_pair, kt_ext, preferred_element_type=F32)
    p = (s * dmat).astype(BF16)
    qs = (q_pair.astype(F32) * jnp.exp(m_row - g)).astype(BF16)
    lhs = jnp.concatenate([p, qs], axis=1)
    zc = jnp.zeros((DQK, 2 * DV), BF16)
    cnb = cn.astype(BF16)
    cn_ext = jnp.concatenate([cnb, zc] if h % 2 == 0 else [zc, cnb], axis=0)
    rhs = jnp.concatenate([vaug, cn_ext], axis=0)
    out = jnp.dot(lhs, rhs, preferred_element_type=F32)
    num, den = out[:, :DV], out[:, DV:]
    hval = num / jnp.maximum(jnp.abs(den), jnp.exp(-(b_col + g)))
    hout_ref[0, :, h * DV:(h + 1) * DV] = hval.astype(hout_ref.dtype)

    g_end = jnp.maximum(m_row, a_last)
    kts = (kt_h.astype(F32) * jnp.exp(a_row - g_end)).astype(BF16)
    upd = jnp.dot(kts, vaug, preferred_element_type=F32)
    decay = jnp.exp(m_row - g_end)
    cn_ref[0, i] = jnp.concatenate([decay, decay], axis=1) * cn + upd
    m_ref[0, i:i + 1, :] = b_last + g_end


def _mlstm_kernel(qf_ref, ktf_ref, vf_ref, caf_ref, cbf_ref, raf_ref,
                  qb_ref, ktb_ref, vb_ref, cab_ref, cbb_ref, rab_ref,
                  cn0_ref, m0_ref, hf_ref, hb_ref, cn_ref, m_ref):
    @pl.when(pl.program_id(1) == 0)
    def _():
        cn_ref[...] = cn0_ref[...]
        m_ref[...] = m0_ref[...]

    t_idx = jax.lax.broadcasted_iota(jnp.int32, (CHUNK, CHUNK), 0)
    s_idx = jax.lax.broadcasted_iota(jnp.int32, (CHUNK, CHUNK), 1)
    for h in range(HEADS):
        _mlstm_unit(h, 0, qf_ref, ktf_ref, vf_ref, caf_ref, cbf_ref, raf_ref, hf_ref, cn_ref, m_ref, s_idx <= t_idx)
        _mlstm_unit(h, 1, qb_ref, ktb_ref, vb_ref, cab_ref, cbb_ref, rab_ref, hb_ref, cn_ref, m_ref, s_idx >= t_idx)


def _mlstm(q, kt, v, cola, colb, rowa, cn0, m0):
    bsz, t, _ = q.shape
    nc = t // CHUNK

    def specs(rev):
        cj = (lambda j: nc - 1 - j) if rev else (lambda j: j)
        d = 1 if rev else 0
        return [pl.BlockSpec((1, CHUNK, QK_W), lambda b, j: (b, cj(j), 0)),
                pl.BlockSpec((1, QK_W, CHUNK), lambda b, j: (b, 0, cj(j))),
                pl.BlockSpec((1, CHUNK, M_WIDTH), lambda b, j: (b, cj(j), 0)),
                pl.BlockSpec((1, CHUNK, LANES), lambda b, j: (b, cj(j), 0)),
                pl.BlockSpec((1, CHUNK, LANES), lambda b, j: (b, cj(j), 0)),
                pl.BlockSpec((1, HEADS, CHUNK), lambda b, j: (b, d, cj(j)))]

    cn_spec = pl.BlockSpec((1, UNITS, DQK, 2 * DV), lambda b, j: (b, 0, 0, 0))
    m_spec = pl.BlockSpec((1, UNITS, LANES), lambda b, j: (b, 0, 0))
    args = (q, kt, v, cola, colb, rowa)
    return pl.pallas_call(
        _mlstm_kernel,
        grid=(bsz, nc),
        in_specs=specs(False) + specs(True) + [cn_spec, m_spec],
        out_specs=[pl.BlockSpec((1, CHUNK, M_WIDTH), lambda b, j: (b, j, 0)),
                   pl.BlockSpec((1, CHUNK, M_WIDTH), lambda b, j: (b, nc - 1 - j, 0)),
                   cn_spec, m_spec],
        out_shape=[jax.ShapeDtypeStruct((bsz, t, M_WIDTH), BF16),
                   jax.ShapeDtypeStruct((bsz, t, M_WIDTH), BF16),
                   jax.ShapeDtypeStruct((bsz, UNITS, DQK, 2 * DV), F32),
                   jax.ShapeDtypeStruct((bsz, UNITS, LANES), F32)],
        compiler_params=_cparams("parallel", "arbitrary"),
        name="mlstm",
    )(*args, *args, cn0, m0)


OUT_PITCH = _pitch(CHUNK)


def _fourier_dense_kernel(xr_ref, xi_ref, cst_ref, bf_ref, o_ref):
    t = xr_ref.shape[1]
    xx = jnp.concatenate([xr_ref[0], xi_ref[0]], axis=0)
    yr = jnp.dot(cst_ref[...], xx, preferred_element_type=F32)
    for g in range(GROUPS):
        for c in range(t // CHUNK):
            o_ref[0, g, c * OUT_PITCH:c * OUT_PITCH + CHUNK, :] = (
                yr[c * CHUNK:(c + 1) * CHUNK, g * GC:(g + 1) * GC] + bf_ref[g])
            o_ref[0, g, c * OUT_PITCH + CHUNK:(c + 1) * OUT_PITCH, :] = jnp.zeros((OUT_PITCH - CHUNK, GC), F32)


def _fourier_dense(xr, xi, cst, bfno):
    bsz, t, w = xr.shape
    blk = pl.BlockSpec((1, t, w), lambda b: (b, 0, 0))
    rows = t // CHUNK * OUT_PITCH
    return pl.pallas_call(
        _fourier_dense_kernel,
        grid=(bsz,),
        in_specs=[blk, blk,
                  pl.BlockSpec((t, 2 * t), lambda b: (0, 0)),
                  pl.BlockSpec((GROUPS, 1, GC), lambda b: (0, 0, 0))],
        out_specs=pl.BlockSpec((1, GROUPS, rows, GC), lambda b: (b, 0, 0, 0)),
        out_shape=jax.ShapeDtypeStruct((bsz, GROUPS, rows, GC), F32),
        compiler_params=_cparams("parallel"),
        name="fourier_dense",
    )(xr, xi, cst, bfno.reshape(GROUPS, 1, GC))


def _fourier_fused_kernel(xr_ref, xi_ref, f1_ref, m_ref, bf_ref, o_ref, gr_scr, gi_scr, *, n1, n2):
    pin = _pitch(n2)

    def stage1(j, carry):
        t2 = 2 * j

        def ld(ref, s):
            return ref[0, 0, pl.ds(s, n1, stride=pin), :].astype(BF16)

        top = jnp.concatenate([ld(xr_ref, t2), ld(xr_ref, t2 + 1)], axis=1)
        bot = jnp.concatenate([ld(xi_ref, t2), ld(xi_ref, t2 + 1)], axis=1)
        g = jnp.dot(f1_ref[...], jnp.concatenate([top, bot], axis=0), preferred_element_type=F32)
        gr_scr[pl.ds(t2, n1, stride=pin), :] = g[:n1, :GC]
        gr_scr[pl.ds(t2 + 1, n1, stride=pin), :] = g[:n1, GC:]
        gi_scr[pl.ds(t2, n1, stride=pin), :] = g[n1:, :GC]
        gi_scr[pl.ds(t2 + 1, n1, stride=pin), :] = g[n1:, GC:]
        return carry

    jax.lax.fori_loop(0, n2 // 2, stage1, 0, unroll=min(4, n2 // 2))

    def stage2(k1, carry):
        base = pl.multiple_of(k1 * pin, 8)
        gg = jnp.concatenate([gr_scr[pl.ds(base, n2), :], gi_scr[pl.ds(base, n2), :]], axis=0).astype(BF16)
        yr = jnp.dot(m_ref[k1], gg, preferred_element_type=F32) + bf_ref[0]
        o_ref[0, 0, pl.ds(k1, n2, stride=OUT_PITCH), :] = yr
        return carry

    jax.lax.fori_loop(0, n1, stage2, 0, unroll=8)
    for k2 in range(n2):
        o_ref[0, 0, k2 * OUT_PITCH + n1:(k2 + 1) * OUT_PITCH, :] = jnp.zeros((OUT_PITCH - n1, GC), F32)


def _fourier_fused(xr, xi, f1, mtab, bfno, *, n1):
    bsz, _, rows_in, _ = xr.shape
    n2 = mtab.shape[1]
    pin = _pitch(n2)
    assert rows_in == n1 * pin and n1 == CHUNK
    in_blk = pl.BlockSpec((1, 1, rows_in, GC), lambda b, g: (b, g, 0, 0))
    return pl.pallas_call(
        functools.partial(_fourier_fused_kernel, n1=n1, n2=n2),
        grid=(bsz, GROUPS),
        in_specs=[in_blk, in_blk,
                  pl.BlockSpec((2 * n1, 2 * n1), lambda b, g: (0, 0)),
                  pl.BlockSpec((n1, n2, 2 * n2), lambda b, g: (0, 0, 0)),
                  pl.BlockSpec((1, 1, GC), lambda b, g: (g, 0, 0))],
        out_specs=pl.BlockSpec((1, 1, n2 * OUT_PITCH, GC), lambda b, g: (b, g, 0, 0)),
        out_shape=jax.ShapeDtypeStruct((bsz, GROUPS, n2 * OUT_PITCH, GC), F32),
        scratch_shapes=[pltpu.VMEM((n1 * pin, GC), F32), pltpu.VMEM((n1 * pin, GC), F32)],
        compiler_params=_cparams("parallel", "parallel"),
        name="fourier_fused",
    )(xr, xi, f1, mtab, bfno.reshape(GROUPS, 1, GC))


def _dft_consts(t):
    ang = 2.0 * np.pi / GC * np.outer(np.arange(GC), np.arange(GC))
    cs = np.concatenate([np.cos(ang), -np.sin(ang)], axis=0) / np.sqrt(GC)
    out = {"cs": cs.astype(np.float32)}
    if t <= 2 * CHUNK:
        ang = 2.0 * np.pi / t * np.mod(np.outer(np.arange(t), np.arange(t)), t)
        out["dense"] = (np.concatenate([np.cos(ang), np.sin(ang)], axis=1) / np.sqrt(t)).astype(np.float32)
    else:
        n1 = CHUNK
        n2 = t // n1
        ang = 2.0 * np.pi / n1 * np.mod(np.outer(np.arange(n1), np.arange(n1)), n1)
        c, s = np.cos(ang), np.sin(ang)
        out["f1"] = (np.block([[c, s], [-s, c]]) / np.sqrt(n1)).astype(np.float32)
        k = np.arange(n1)[:, None, None] + n1 * np.arange(n2)[None, :, None]
        ang = 2.0 * np.pi / t * np.mod(k * np.arange(n2)[None, None, :], t)
        out["mtab"] = (np.concatenate([np.cos(ang), np.sin(ang)], axis=2) / np.sqrt(n2)).astype(np.float32)
    return out


def _outproj_kernel(hf_ref, hb_ref, o_ref, za_ref, u_ref, vs_ref, zb_ref, ym_ref, zc_ref, x_ref, gt_ref,
                    wout_ref, ghn_ref, gsgu_ref, gpost_ref, wsp_ref, bsp_ref, xo_ref, y_scr):
    tm = x_ref.shape[1]
    for h in range(HEADS):
        sl = slice(h * DV, (h + 1) * DV)
        hh = hf_ref[0, :, sl].astype(F32) + hb_ref[0, :, sl].astype(F32)
        hn = hh * jax.lax.rsqrt(jnp.mean(hh * hh, axis=-1, keepdims=True) + EPS) * ghn_ref[:, sl]
        ya = hn * _sigmoid(o_ref[0, :, sl].astype(F32)) * _silu(za_ref[0, :, sl].astype(F32))
        y_scr[:, sl] = ya.astype(BF16)
    vs = vs_ref[0].astype(F32)
    vn = (vs * jax.lax.rsqrt(jnp.mean(vs * vs, axis=-1, keepdims=True) + EPS) * gsgu_ref[...]).astype(BF16)
    for c in range(tm // CHUNK):
        rows = slice(c * CHUNK, (c + 1) * CHUNK)
        for g in range(GROUPS):
            cols = slice(g * GC, (g + 1) * GC)
            mixed = jnp.dot(wsp_ref[g], vn[rows, cols], preferred_element_type=F32) + bsp_ref[:, cols]
            yb = u_ref[0, rows, cols].astype(F32) * mixed * _silu(zb_ref[0, rows, cols].astype(F32))
            y_scr[rows, M_WIDTH + g * GC:M_WIDTH + (g + 1) * GC] = yb.astype(BF16)
    for c in range(tm // CHUNK):
        rows = slice(c * CHUNK, (c + 1) * CHUNK)
        for g in range(GROUPS):
            cols = slice(g * GC, (g + 1) * GC)
            yc = ym_ref[0, g, c * OUT_PITCH:c * OUT_PITCH + CHUNK, :] * _silu(zc_ref[0, rows, cols].astype(F32))
            y_scr[rows, M_WIDTH + S_WIDTH + g * GC:M_WIDTH + S_WIDTH + (g + 1) * GC] = yc.astype(BF16)
    out = jnp.dot(y_scr[...], wout_ref[...], preferred_element_type=F32)
    on = out * jax.lax.rsqrt(jnp.mean(out * out, axis=-1, keepdims=True) + EPS) * gpost_ref[...]
    xo_ref[0] = x_ref[0] + gt_ref[0] * on


def _outproj(hf, hb, o, za, u, vs, zb, ym, zc, x, gt, w_out, ghn, gsgu, gpost, wsp, bsp, *, tm):
    bsz, t, d = x.shape
    wide = pl.BlockSpec((1, tm, M_WIDTH), lambda b, i: (b, i, 0))
    half = pl.BlockSpec((1, tm, S_WIDTH), lambda b, i: (b, i, 0))
    const2 = lambda shape: pl.BlockSpec(shape, lambda b, i: (0,) * len(shape))
    return pl.pallas_call(
        _outproj_kernel,
        grid=(bsz, t // tm),
        in_specs=[wide, wide, wide, wide, half, half, half,
                  pl.BlockSpec((1, GROUPS, tm // CHUNK * OUT_PITCH, GC), lambda b, i: (b, 0, i, 0)),
                  half, wide,
                  pl.BlockSpec((1, 1, d), lambda b, i: (b, 0, 0)),
                  const2((2 * D_MODEL, d)), const2((1, M_WIDTH)), const2((1, S_WIDTH)), const2((1, d)),
                  const2((GROUPS, CHUNK, CHUNK)), const2((CHUNK, S_WIDTH))],
        out_specs=wide,
        out_shape=jax.ShapeDtypeStruct((bsz, t, d), F32),
        scratch_shapes=[pltpu.VMEM((tm, 2 * D_MODEL), BF16)],
        compiler_params=_cparams("parallel", "parallel"),
        name="outproj",
    )(hf, hb, o, za, u, vs, zb, ym, zc, x, gt, w_out, ghn, gsgu, gpost, wsp, bsp)


def _layer_weights(w_in_l, b_gate_l):
    c = np.cumsum([0, QK_W, QK_W, M_WIDTH, 4 * HEADS, M_WIDTH, M_WIDTH, S_WIDTH, S_WIDTH, S_WIDTH, F_WIDTH, F_WIDTH])
    q, k, v, gates, o, za, u, vs, zb, f, zc = (w_in_l[:, c[i]:c[i + 1]] for i in range(11))
    pad = jnp.zeros((w_in_l.shape[0], LANES - UNITS), w_in_l.dtype)
    g_i = jnp.concatenate([gates[:, 0:8], gates[:, 16:24], pad], axis=1)
    g_f = jnp.concatenate([gates[:, 8:16], gates[:, 24:32], pad], axis=1)
    w_nat = jnp.concatenate([q * (DQK ** -0.5), v, g_i, g_f, o, za, u, vs, zb, f, zc], axis=1).astype(BF16)
    w_kt = k.T.astype(BF16)
    bpad = jnp.zeros((LANES - UNITS,), b_gate_l.dtype)
    b_i = jnp.concatenate([b_gate_l[0:8], b_gate_l[16:24], bpad]).reshape(1, LANES)
    b_f = jnp.concatenate([b_gate_l[8:16], b_gate_l[24:32], bpad]).reshape(1, LANES)
    return w_nat, w_kt, b_i, b_f


def kernel(x, c, ctx, c_ctx, w_mod, b_mod, g_pre, g_post, w_in, b_gate, g_hnorm, g_sgu, w_sp, b_sp, w_fno, b_fno, w_out):
    bsz, t_lat, d = x.shape
    t_ctx = ctx.shape[1]
    depth = w_mod.shape[0]
    assert d == D_MODEL and t_lat % (2 * CHUNK) == 0 and t_ctx % (2 * CHUNK) == 0 and bsz + 1 <= 8

    cc = jnp.concatenate([c, c_ctx[None, :], jnp.zeros((8 - bsz - 1, d), c.dtype)], axis=0)
    mod = _modulation(cc, w_mod, b_mod)

    consts_lat, consts_ctx = _dft_consts(t_lat), _dft_consts(t_ctx)
    cst = jnp.asarray(consts_lat["cs"]).astype(BF16)

    def fourier(xr, xi, consts, bfno):
        if "dense" in consts:
            return _fourier_dense(xr, xi, jnp.asarray(consts["dense"]).astype(BF16), bfno)
        return _fourier_fused(xr, xi, jnp.asarray(consts["f1"]).astype(BF16),
                              jnp.asarray(consts["mtab"]).astype(BF16), bfno, n1=CHUNK)

    cn_zero = jnp.zeros((bsz, UNITS, DQK, 2 * DV), F32)
    m_zero = jnp.zeros((bsz, UNITS, LANES), F32)
    xc = ctx
    for l in range(depth):
        sh_l, sc_l, gt_l = (mod[l, :bsz, i * d:(i + 1) * d].reshape(bsz, 1, d) for i in range(3))
        sh_c, sc_c, gt_c = (jnp.broadcast_to(mod[l, bsz, i * d:(i + 1) * d].reshape(1, 1, d), (bsz, 1, d))
                            for i in range(3))
        w_nat, w_kt, b_i, b_f = _layer_weights(w_in[l], b_gate[l])
        gpre = g_pre[l].reshape(1, d)
        wf = w_fno[l].astype(BF16)
        wsp = w_sp[l].astype(BF16)
        bsp = jnp.broadcast_to(b_sp[l].T[:, :, None], (CHUNK, GROUPS, GC)).reshape(CHUNK, S_WIDTH)
        wo = w_out[l].astype(BF16)
        tail = (wo, g_hnorm[l].reshape(1, M_WIDTH), g_sgu[l].reshape(1, S_WIDTH), g_post[l].reshape(1, d), wsp, bsp)
        front = functools.partial(_inproj, g_pre=gpre, w_nat=w_nat, w_kt=w_kt, cst=cst, wf=wf, b_i=b_i, b_f=b_f)

        last = l == depth - 1
        pc = front(xc, sc_c, sh_c, full=not last, tm=256)
        hf_c, hb_c, cn_c, m_c = _mlstm(pc["q"], pc["kT"], pc["v"], pc["cola"], pc["colb"], pc["rowa"], cn_zero, m_zero)
        p = front(x, sc_l, sh_l, full=True, tm=512)
        hf, hb, _, _ = _mlstm(p["q"], p["kT"], p["v"], p["cola"], p["colb"], p["rowa"], cn_c, m_c)
        ym = fourier(p["xr"], p["xi"], consts_lat, b_fno[l])
        x = _outproj(hf, hb, p["o"], p["za"], p["u"], p["vs"], p["zb"], ym, p["zc"], x, gt_l, *tail, tm=256)
        if not last:
            ymc = fourier(pc["xr"], pc["xi"], consts_ctx, b_fno[l])
            xc = _outproj(hf_c, hb_c, pc["o"], pc["za"], pc["u"], pc["vs"], pc["zb"], ymc, pc["zc"], xc, gt_c,
                          *tail, tm=256)
    return x
```

```python
import functools

import numpy as np
import jax
import jax.numpy as jnp
from jax.experimental import pallas as pl
from jax.experimental.pallas import tpu as pltpu

EPS = 1e-6
LOG2E = 1.4426950408889634
D_MODEL = 1024
HEADS = 8
DV = 128
DQK = 64
QK_W = HEADS * DQK
M_WIDTH = HEADS * DV
CHUNK = 128
S_WIDTH = 512
F_WIDTH = 512
GROUPS = 4
GC = 128
UNITS = 2 * HEADS
LANES = 128

VMEM_LIMIT = 56 * 1024 * 1024

F32 = jnp.float32
BF16 = jnp.bfloat16


def _cparams(*sem):
    return pltpu.CompilerParams(dimension_semantics=sem, vmem_limit_bytes=VMEM_LIMIT)


def _sigmoid(x):
    return 1.0 / (1.0 + jnp.exp(-x))


def _silu(x):
    return x * _sigmoid(x)


def _mod_kernel(cc_ref, w_ref, b_ref, o_ref):
    s = _silu(cc_ref[...]).astype(BF16)
    w = w_ref[0].astype(BF16)
    o_ref[0] = jnp.dot(s, w, preferred_element_type=F32) + b_ref[0]


def _modulation(cc, w_mod, b_mod):
    depth, d, d3 = w_mod.shape
    return pl.pallas_call(
        _mod_kernel,
        grid=(depth,),
        in_specs=[pl.BlockSpec((8, d), lambda l: (0, 0)),
                  pl.BlockSpec((1, d, d3), lambda l: (l, 0, 0)),
                  pl.BlockSpec((1, 1, d3), lambda l: (l, 0, 0))],
        out_specs=pl.BlockSpec((1, 8, d3), lambda l: (l, 0, 0)),
        out_shape=jax.ShapeDtypeStruct((depth, 8, d3), F32),
        compiler_params=_cparams("arbitrary"),
        name="modulation",
    )(cc, w_mod, b_mod.reshape(depth, 1, d3))


def _log_sigmoid(x):
    return jnp.minimum(x, 0.0) - jnp.log1p(jnp.exp(-jnp.abs(x)))


def _gate_scans(gi, gf, cola_ref, colb_ref, rowa_ref):
    tm = gi.shape[0]
    row = jax.lax.broadcasted_iota(jnp.int32, (CHUNK, LANES), 0)
    fwd = jax.lax.broadcasted_iota(jnp.int32, (CHUNK, LANES), 1) < HEADS
    lf = _log_sigmoid(gf)

    def scan(x, op):
        pre, suf = x, x
        k = 1
        while k < CHUNK:
            sh = pltpu.roll(pre, k, axis=0)
            pre = jnp.where(row >= k, op(pre, sh), pre)
            sh = pltpu.roll(suf, CHUNK - k, axis=0)
            suf = jnp.where(row < CHUNK - k, op(suf, sh), suf)
            k *= 2
        return jnp.where(fwd, pre, suf)

    for c in range(tm // CHUNK):
        rows = slice(c * CHUNK, (c + 1) * CHUNK)
        b = scan(lf[rows], jnp.add)
        a = (gi[rows] - b) * LOG2E
        cola_ref[0, rows, :] = scan(a, jnp.maximum)
        colb_ref[0, rows, :] = b * LOG2E
        rowa_ref[0, :, rows] = a.T[0:UNITS, :]


def _pitch(n):
    p = n
    while (p // 8) % 2 == 0:
        p += 8
    return p


def _store_slabs(ref, g, z, n2):
    pin = _pitch(n2)
    for r in range(z.shape[0] // n2):
        ref[0, g, r * pin:r * pin + n2, :] = z[r * n2:(r + 1) * n2]
        ref[0, g, r * pin + n2:(r + 1) * pin, :] = jnp.zeros((pin - n2, GC), F32)


def _inproj_kernel(x_ref, sc_ref, sh_ref, g_ref, wn_ref, wkt_ref, cst_ref, wf_ref, bi_ref, bf_ref, *out_refs,
                   pieces, slab_n2):
    x = x_ref[0]
    y = x * jax.lax.rsqrt(jnp.mean(x * x, axis=-1, keepdims=True) + EPS)
    h = (y * g_ref[...]) * (1.0 + sc_ref[0]) + sh_ref[0]
    hb = h.astype(BF16)
    oi = 0
    for name, off, width in pieces:
        if name == "kT":
            r = jax.lax.dot_general(wkt_ref[...], hb, (((1,), (1,)), ((), ())),
                                    preferred_element_type=F32)
            out_refs[oi][0] = r.astype(BF16)
            oi += 1
            continue
        r = jnp.dot(hb, wn_ref[:, off:off + width], preferred_element_type=F32)
        if name == "gates":
            _gate_scans(r[:, :LANES] + bi_ref[...], r[:, LANES:] + bf_ref[...], *out_refs[oi:oi + 3])
            oi += 3
        elif name == "f":
            fb = r.astype(BF16)
            xr_ref, xi_ref = out_refs[oi], out_refs[oi + 1]
            for g in range(GROUPS):
                wc = jnp.dot(cst_ref[...], wf_ref[g], preferred_element_type=F32)
                wcat = jnp.concatenate([wc[:GC], wc[GC:]], axis=1).astype(BF16)
                z = jnp.dot(fb[:, g * GC:(g + 1) * GC], wcat, preferred_element_type=F32)
                if slab_n2 is None:
                    xr_ref[0, :, g * GC:(g + 1) * GC] = z[:, :GC].astype(BF16)
                    xi_ref[0, :, g * GC:(g + 1) * GC] = z[:, GC:].astype(BF16)
                else:
                    _store_slabs(xr_ref, g, z[:, :GC], slab_n2)
                    _store_slabs(xi_ref, g, z[:, GC:], slab_n2)
            oi += 2
        else:
            out_refs[oi][0] = r.astype(BF16)
            oi += 1


_FULL_PIECES = (("q", QK_W), ("v", M_WIDTH), ("gates", 2 * LANES), ("o", M_WIDTH), ("za", M_WIDTH),
                ("u", S_WIDTH), ("vs", S_WIDTH), ("zb", S_WIDTH), ("f", F_WIDTH), ("zc", F_WIDTH))
_STATE_PIECES = _FULL_PIECES[:3]


def _inproj(x, sc, sh, g_pre, w_nat, w_kt, cst, wf, b_i, b_f, *, full, tm):
    bsz, t, d = x.shape
    names = _FULL_PIECES if full else _STATE_PIECES
    slab_n2 = t // CHUNK if t > 2 * CHUNK else None
    row_spec = lambda width: pl.BlockSpec((1, tm, width), lambda b, i: (b, i, 0))
    pieces, off = [], 0
    out_names, out_shapes, out_specs = [], [], []
    for name, width in names:
        pieces.append((name, off, width))
        off += width
        if name == "gates":
            out_names += ["cola", "colb", "rowa"]
            out_shapes += [jax.ShapeDtypeStruct((bsz, t, LANES), F32)] * 2 + [jax.ShapeDtypeStruct((bsz, UNITS, t), F32)]
            out_specs += [row_spec(LANES), row_spec(LANES), pl.BlockSpec((1, UNITS, tm), lambda b, i: (b, 0, i))]
        elif name == "f":
            out_names += ["xr", "xi"]
            if slab_n2 is None:
                out_shapes += [jax.ShapeDtypeStruct((bsz, t, width), BF16)] * 2
                out_specs += [row_spec(width)] * 2
            else:
                assert tm % slab_n2 == 0
                pin = _pitch(slab_n2)
                out_shapes += [jax.ShapeDtypeStruct((bsz, GROUPS, CHUNK * pin, GC), F32)] * 2
                out_specs += [pl.BlockSpec((1, GROUPS, tm // slab_n2 * pin, GC), lambda b, i: (b, 0, i, 0))] * 2
        else:
            out_names.append(name)
            out_shapes.append(jax.ShapeDtypeStruct((bsz, t, width), BF16))
            out_specs.append(row_spec(width))
    n_nat = off
    pieces.append(("kT", 0, QK_W))
    out_names.append("kT")
    out_shapes.append(jax.ShapeDtypeStruct((bsz, QK_W, t), BF16))
    out_specs.append(pl.BlockSpec((1, QK_W, tm), lambda b, i: (b, 0, i)))
    const = lambda shape, **kw: pl.BlockSpec(shape, lambda b, i: (0,) * len(shape), **kw)
    outs = pl.pallas_call(
        functools.partial(_inproj_kernel, pieces=tuple(pieces), slab_n2=slab_n2),
        grid=(bsz, t // tm),
        in_specs=[pl.BlockSpec((1, tm, d), lambda b, i: (b, i, 0)),
                  pl.BlockSpec((1, 1, d), lambda b, i: (b, 0, 0)),
                  pl.BlockSpec((1, 1, d), lambda b, i: (b, 0, 0)),
                  const((1, d)),
                  const((d, n_nat), pipeline_mode=pl.Buffered(1)),
                  const((QK_W, d), pipeline_mode=pl.Buffered(1)),
                  const((2 * GC, GC)), const((GROUPS, GC, GC)), const((1, LANES)), const((1, LANES))],
        out_specs=out_specs,
        out_shape=out_shapes,
        compiler_params=_cparams("parallel", "parallel"),
        name="inproj_full" if full else "inproj_state",
    )(x, sc, sh, g_pre, w_nat, w_kt, cst, wf, b_i, b_f)
    return dict(zip(out_names, outs))


def _mlstm_unit(h, d, sub, q_ref, kt_ref, v_ref, rowa_ref, m_diag, g_cols, c_cols, g_ends, hout_ref, cn_ref, mask):
    i = d * HEADS + h
    pair = h // 2
    rows = slice(sub * CHUNK, (sub + 1) * CHUNK)
    q_pair = q_ref[0, rows, pair * LANES:(pair + 1) * LANES]
    kt_h = kt_ref[0, h * DQK:(h + 1) * DQK, rows]
    zk = jnp.zeros((DQK, CHUNK), BF16)
    kt_ext = jnp.concatenate([kt_h, zk] if h % 2 == 0 else [zk, kt_h], axis=0)
    v_h = v_ref[0, rows, h * DV:(h + 1) * DV]
    vaug = jnp.concatenate([v_h, jnp.ones((CHUNK, DV), BF16)], axis=1)
    g = jnp.broadcast_to(g_cols[:, i:i + 1], (CHUNK, CHUNK))
    c = jnp.broadcast_to(c_cols[:, i:i + 1], (CHUNK, CHUNK))
    g_end = jnp.broadcast_to(g_ends[:, i:i + 1], (1, LANES))
    m_row = jnp.broadcast_to(m_diag[:, i:i + 1], (1, LANES))
    a_row = rowa_ref[0, h:h + 1, rows]
    cn = cn_ref[0, i]

    dmat = jnp.where(mask, jnp.exp2(a_row - g), 0.0)
    s = jnp.dot(q_pair, kt_ext, preferred_element_type=F32)
    p = (s * dmat).astype(BF16)
    qs = q_pair * jnp.exp2(m_row - g).astype(BF16)
    lhs = jnp.concatenate([p, qs], axis=1)
    zc = jnp.zeros((DQK, 2 * DV), BF16)
    cnb = cn.astype(BF16)
    cn_ext = jnp.concatenate([cnb, zc] if h % 2 == 0 else [zc, cnb], axis=0)
    rhs = jnp.concatenate([vaug, cn_ext], axis=0)
    out = jnp.dot(lhs, rhs, preferred_element_type=F32)
    num, den = out[:, :DV], out[:, DV:]
    hval = num / jnp.maximum(jnp.abs(den), jnp.exp2(c))
    hout_ref[0, rows, h * DV:(h + 1) * DV] = hval.astype(hout_ref.dtype)

    kts = kt_h * jnp.exp2(a_row - g_end).astype(BF16)
    upd = jnp.dot(kts, vaug, preferred_element_type=F32)
    decay = jnp.exp2(m_row - g_end)
    cn_ref[0, i] = jnp.concatenate([decay, decay], axis=1) * cn + upd


MLSTM_SUB = 4


def _mlstm_kernel(qf_ref, ktf_ref, vf_ref, caf_ref, cbf_ref, raf_ref,
                  qb_ref, ktb_ref, vb_ref, cab_ref, cbb_ref, rab_ref,
                  cn0_ref, m0_ref, hf_ref, hb_ref, cn_ref, m_ref, *, n_sub):
    @pl.when(pl.program_id(1) == 0)
    def _():
        cn_ref[...] = cn0_ref[...]
        m_ref[...] = m0_ref[...]

    t_idx = jax.lax.broadcasted_iota(jnp.int32, (CHUNK, CHUNK), 0)
    s_idx = jax.lax.broadcasted_iota(jnp.int32, (CHUNK, CHUNK), 1)
    unit_row = jax.lax.broadcasted_iota(jnp.int32, (UNITS, LANES), 0)
    unit_lane = jax.lax.broadcasted_iota(jnp.int32, (UNITS, LANES), 1)
    m_diag = jnp.sum(jnp.where(unit_row == unit_lane, m_ref[0], 0.0), axis=0, keepdims=True)
    fwd_lane = jax.lax.broadcasted_iota(jnp.int32, (1, LANES), 1) < HEADS

    def columns(ca_ref, cb_ref, sub, end):
        rows = slice(sub * CHUNK, (sub + 1) * CHUNK)
        g_cols = jnp.maximum(m_diag, ca_ref[0, rows, :])
        g_ends = g_cols[end:end + 1, :]
        m_new = cb_ref[0, sub * CHUNK + end:sub * CHUNK + end + 1, :] + g_ends
        return (g_cols, -(cb_ref[0, rows, :] + g_cols), g_ends), m_new

    for step in range(n_sub):
        sub_f, sub_b = step, n_sub - 1 - step
        cols_f, m_new_f = columns(caf_ref, cbf_ref, sub_f, CHUNK - 1)
        cols_b, m_new_b = columns(cab_ref, cbb_ref, sub_b, 0)
        for h in range(HEADS):
            _mlstm_unit(h, 0, sub_f, qf_ref, ktf_ref, vf_ref, raf_ref, m_diag, *cols_f, hf_ref, cn_ref, s_idx <= t_idx)
            _mlstm_unit(h, 1, sub_b, qb_ref, ktb_ref, vb_ref, rab_ref, m_diag, *cols_b, hb_ref, cn_ref, s_idx >= t_idx)
        m_diag = jnp.where(fwd_lane, m_new_f, m_new_b)
    m_ref[0] = jnp.where(unit_row == unit_lane, jnp.broadcast_to(m_diag, (UNITS, LANES)), 0.0)


def _mlstm(q, kt, v, cola, colb, rowa, cn0, m0):
    bsz, t, _ = q.shape
    n_sub = min(MLSTM_SUB, t // CHUNK)
    blk = n_sub * CHUNK
    nc = t // blk

    def specs(rev):
        cj = (lambda j: nc - 1 - j) if rev else (lambda j: j)
        d = 1 if rev else 0
        return [pl.BlockSpec((1, blk, QK_W), lambda b, j: (b, cj(j), 0)),
                pl.BlockSpec((1, QK_W, blk), lambda b, j: (b, 0, cj(j))),
                pl.BlockSpec((1, blk, M_WIDTH), lambda b, j: (b, cj(j), 0)),
                pl.BlockSpec((1, blk, LANES), lambda b, j: (b, cj(j), 0)),
                pl.BlockSpec((1, blk, LANES), lambda b, j: (b, cj(j), 0)),
                pl.BlockSpec((1, HEADS, blk), lambda b, j: (b, d, cj(j)))]

    cn_spec = pl.BlockSpec((1, UNITS, DQK, 2 * DV), lambda b, j: (b, 0, 0, 0))
    m_spec = pl.BlockSpec((1, UNITS, LANES), lambda b, j: (b, 0, 0))
    args = (q, kt, v, cola, colb, rowa)
    return pl.pallas_call(
        functools.partial(_mlstm_kernel, n_sub=n_sub),
        grid=(bsz, nc),
        in_specs=specs(False) + specs(True) + [cn_spec, m_spec],
        out_specs=[pl.BlockSpec((1, blk, M_WIDTH), lambda b, j: (b, j, 0)),
                   pl.BlockSpec((1, blk, M_WIDTH), lambda b, j: (b, nc - 1 - j, 0)),
                   cn_spec, m_spec],
        out_shape=[jax.ShapeDtypeStruct((bsz, t, M_WIDTH), BF16),
                   jax.ShapeDtypeStruct((bsz, t, M_WIDTH), BF16),
                   jax.ShapeDtypeStruct((bsz, UNITS, DQK, 2 * DV), F32),
                   jax.ShapeDtypeStruct((bsz, UNITS, LANES), F32)],
        compiler_params=_cparams("parallel", "arbitrary"),
        name="mlstm",
    )(*args, *args, cn0, m0)


OUT_PITCH = _pitch(CHUNK)


def _fourier_dense_kernel(xr_ref, xi_ref, cst_ref, bf_ref, o_ref):
    t = xr_ref.shape[1]
    xx = jnp.concatenate([xr_ref[0], xi_ref[0]], axis=0)
    yr = jnp.dot(cst_ref[...], xx, preferred_element_type=F32)
    for g in range(GROUPS):
        for c in range(t // CHUNK):
            o_ref[0, g, c * OUT_PITCH:c * OUT_PITCH + CHUNK, :] = (
                yr[c * CHUNK:(c + 1) * CHUNK, g * GC:(g + 1) * GC] + bf_ref[g])
            o_ref[0, g, c * OUT_PITCH + CHUNK:(c + 1) * OUT_PITCH, :] = jnp.zeros((OUT_PITCH - CHUNK, GC), F32)


def _fourier_dense(xr, xi, cst, bfno):
    bsz, t, w = xr.shape
    blk = pl.BlockSpec((1, t, w), lambda b: (b, 0, 0))
    rows = t // CHUNK * OUT_PITCH
    return pl.pallas_call(
        _fourier_dense_kernel,
        grid=(bsz,),
        in_specs=[blk, blk,
                  pl.BlockSpec((t, 2 * t), lambda b: (0, 0)),
                  pl.BlockSpec((GROUPS, 1, GC), lambda b: (0, 0, 0))],
        out_specs=pl.BlockSpec((1, GROUPS, rows, GC), lambda b: (b, 0, 0, 0)),
        out_shape=jax.ShapeDtypeStruct((bsz, GROUPS, rows, GC), F32),
        compiler_params=_cparams("parallel"),
        name="fourier_dense",
    )(xr, xi, cst, bfno.reshape(GROUPS, 1, GC))


def _fourier_fused_kernel(xr_ref, xi_ref, f1_ref, m_ref, bf_ref, o_ref, gr_scr, gi_scr, *, n1, n2):
    pin = _pitch(n2)

    def stage1(j, carry):
        t2 = 2 * j

        def ld(ref, s):
            return ref[0, 0, pl.ds(s, n1, stride=pin), :].astype(BF16)

        top = jnp.concatenate([ld(xr_ref, t2), ld(xr_ref, t2 + 1)], axis=1)
        bot = jnp.concatenate([ld(xi_ref, t2), ld(xi_ref, t2 + 1)], axis=1)
        g = jnp.dot(f1_ref[...], jnp.concatenate([top, bot], axis=0), preferred_element_type=F32)
        gr_scr[pl.ds(t2, n1, stride=pin), :] = g[:n1, :GC]
        gr_scr[pl.ds(t2 + 1, n1, stride=pin), :] = g[:n1, GC:]
        gi_scr[pl.ds(t2, n1, stride=pin), :] = g[n1:, :GC]
        gi_scr[pl.ds(t2 + 1, n1, stride=pin), :] = g[n1:, GC:]
        return carry

    jax.lax.fori_loop(0, n2 // 2, stage1, 0, unroll=min(4, n2 // 2))

    def stage2(k1, carry):
        base = pl.multiple_of(k1 * pin, 8)
        gg = jnp.concatenate([gr_scr[pl.ds(base, n2), :], gi_scr[pl.ds(base, n2), :]], axis=0).astype(BF16)
        yr = jnp.dot(m_ref[k1], gg, preferred_element_type=F32) + bf_ref[0]
        o_ref[0, 0, pl.ds(k1, n2, stride=OUT_PITCH), :] = yr
        return carry

    jax.lax.fori_loop(0, n1, stage2, 0, unroll=8)
    for k2 in range(n2):
        o_ref[0, 0, k2 * OUT_PITCH + n1:(k2 + 1) * OUT_PITCH, :] = jnp.zeros((OUT_PITCH - n1, GC), F32)


def _fourier_fused(xr, xi, f1, mtab, bfno, *, n1):
    bsz, _, rows_in, _ = xr.shape
    n2 = mtab.shape[1]
    pin = _pitch(n2)
    assert rows_in == n1 * pin and n1 == CHUNK
    in_blk = pl.BlockSpec((1, 1, rows_in, GC), lambda b, g: (b, g, 0, 0))
    return pl.pallas_call(
        functools.partial(_fourier_fused_kernel, n1=n1, n2=n2),
        grid=(bsz, GROUPS),
        in_specs=[in_blk, in_blk,
                  pl.BlockSpec((2 * n1, 2 * n1), lambda b, g: (0, 0)),
                  pl.BlockSpec((n1, n2, 2 * n2), lambda b, g: (0, 0, 0)),
                  pl.BlockSpec((1, 1, GC), lambda b, g: (g, 0, 0))],
        out_specs=pl.BlockSpec((1, 1, n2 * OUT_PITCH, GC), lambda b, g: (b, g, 0, 0)),
        out_shape=jax.ShapeDtypeStruct((bsz, GROUPS, n2 * OUT_PITCH, GC), F32),
        scratch_shapes=[pltpu.VMEM((n1 * pin, GC), F32), pltpu.VMEM((n1 * pin, GC), F32)],
        compiler_params=_cparams("parallel", "parallel"),
        name="fourier_fused",
    )(xr, xi, f1, mtab, bfno.reshape(GROUPS, 1, GC))


def _dft_consts(t):
    ang = 2.0 * np.pi / GC * np.outer(np.arange(GC), np.arange(GC))
    cs = np.concatenate([np.cos(ang), -np.sin(ang)], axis=0) / np.sqrt(GC)
    out = {"cs": cs.astype(np.float32)}
    if t <= 2 * CHUNK:
        ang = 2.0 * np.pi / t * np.mod(np.outer(np.arange(t), np.arange(t)), t)
        out["dense"] = (np.concatenate([np.cos(ang), np.sin(ang)], axis=1) / np.sqrt(t)).astype(np.float32)
    else:
        n1 = CHUNK
        n2 = t // n1
        ang = 2.0 * np.pi / n1 * np.mod(np.outer(np.arange(n1), np.arange(n1)), n1)
        c, s = np.cos(ang), np.sin(ang)
        out["f1"] = (np.block([[c, s], [-s, c]]) / np.sqrt(n1)).astype(np.float32)
        k = np.arange(n1)[:, None, None] + n1 * np.arange(n2)[None, :, None]
        ang = 2.0 * np.pi / t * np.mod(k * np.arange(n2)[None, None, :], t)
        out["mtab"] = (np.concatenate([np.cos(ang), np.sin(ang)], axis=2) / np.sqrt(n2)).astype(np.float32)
    return out


def _outproj_kernel(hf_ref, hb_ref, o_ref, za_ref, u_ref, vs_ref, zb_ref, ym_ref, zc_ref, x_ref, gt_ref,
                    wout_ref, ghn_ref, gsgu_ref, gpost_ref, wsp_ref, bsp_ref, xo_ref, y_scr):
    tm = x_ref.shape[1]
    kc = S_WIDTH

    def project(k):
        return jnp.dot(y_scr[:, k * kc:(k + 1) * kc], wout_ref[k * kc:(k + 1) * kc, :], preferred_element_type=F32)

    out = None
    for h in range(HEADS):
        sl = slice(h * DV, (h + 1) * DV)
        hh = (hf_ref[0, :, sl] + hb_ref[0, :, sl]).astype(F32)
        hn = hh * jax.lax.rsqrt(jnp.mean(hh * hh, axis=-1, keepdims=True) + EPS) * ghn_ref[:, sl]
        o, za = o_ref[0, :, sl], za_ref[0, :, sl]
        y_scr[:, sl] = hn.astype(BF16) * (za / ((1 + jnp.exp(-o)) * (1 + jnp.exp(-za))))
        if (h + 1) * DV % kc == 0:
            part = project((h + 1) * DV // kc - 1)
            out = part if out is None else out + part
    vs = vs_ref[0].astype(F32)
    vn = (vs * jax.lax.rsqrt(jnp.mean(vs * vs, axis=-1, keepdims=True) + EPS) * gsgu_ref[...]).astype(BF16)
    for c in range(tm // CHUNK):
        rows = slice(c * CHUNK, (c + 1) * CHUNK)
        for g in range(GROUPS):
            cols = slice(g * GC, (g + 1) * GC)
            mixed = jnp.dot(wsp_ref[g], vn[rows, cols], preferred_element_type=F32) + bsp_ref[:, cols]
            zb = zb_ref[0, rows, cols]
            y_scr[rows, M_WIDTH + g * GC:M_WIDTH + (g + 1) * GC] = (
                u_ref[0, rows, cols] * mixed.astype(BF16) * (zb / (1 + jnp.exp(-zb))))
    out = out + project(M_WIDTH // kc)
    for c in range(tm // CHUNK):
        rows = slice(c * CHUNK, (c + 1) * CHUNK)
        for g in range(GROUPS):
            cols = slice(g * GC, (g + 1) * GC)
            zc = zc_ref[0, rows, cols]
            y_scr[rows, M_WIDTH + S_WIDTH + g * GC:M_WIDTH + S_WIDTH + (g + 1) * GC] = (
                ym_ref[0, g, c * OUT_PITCH:c * OUT_PITCH + CHUNK, :].astype(BF16) * (zc / (1 + jnp.exp(-zc))))
    out = out + project((M_WIDTH + S_WIDTH) // kc)
    on = out * jax.lax.rsqrt(jnp.mean(out * out, axis=-1, keepdims=True) + EPS) * gpost_ref[...]
    xo_ref[0] = x_ref[0] + gt_ref[0] * on


def _outproj(hf, hb, o, za, u, vs, zb, ym, zc, x, gt, w_out, ghn, gsgu, gpost, wsp, bsp, *, tm):
    bsz, t, d = x.shape
    wide = pl.BlockSpec((1, tm, M_WIDTH), lambda b, i: (b, i, 0))
    half = pl.BlockSpec((1, tm, S_WIDTH), lambda b, i: (b, i, 0))
    const2 = lambda shape: pl.BlockSpec(shape, lambda b, i: (0,) * len(shape))
    return pl.pallas_call(
        _outproj_kernel,
        grid=(bsz, t // tm),
        in_specs=[wide, wide, wide, wide, half, half, half,
                  pl.BlockSpec((1, GROUPS, tm // CHUNK * OUT_PITCH, GC), lambda b, i: (b, 0, i, 0)),
                  half, wide,
                  pl.BlockSpec((1, 1, d), lambda b, i: (b, 0, 0)),
                  const2((2 * D_MODEL, d)), const2((1, M_WIDTH)), const2((1, S_WIDTH)), const2((1, d)),
                  const2((GROUPS, CHUNK, CHUNK)), const2((CHUNK, S_WIDTH))],
        out_specs=wide,
        out_shape=jax.ShapeDtypeStruct((bsz, t, d), F32),
        scratch_shapes=[pltpu.VMEM((tm, 2 * D_MODEL), BF16)],
        compiler_params=_cparams("parallel", "parallel"),
        name="outproj",
    )(hf, hb, o, za, u, vs, zb, ym, zc, x, gt, w_out, ghn, gsgu, gpost, wsp, bsp)


def _layer_weights(w_in_l, b_gate_l):
    c = np.cumsum([0, QK_W, QK_W, M_WIDTH, 4 * HEADS, M_WIDTH, M_WIDTH, S_WIDTH, S_WIDTH, S_WIDTH, F_WIDTH, F_WIDTH])
    q, k, v, gates, o, za, u, vs, zb, f, zc = (w_in_l[:, c[i]:c[i + 1]] for i in range(11))
    pad = jnp.zeros((w_in_l.shape[0], LANES - UNITS), w_in_l.dtype)
    g_i = jnp.concatenate([gates[:, 0:8], gates[:, 16:24], pad], axis=1)
    g_f = jnp.concatenate([gates[:, 8:16], gates[:, 24:32], pad], axis=1)
    w_nat = jnp.concatenate([q * (DQK ** -0.5), v, g_i, g_f, o, za, u, vs, zb, f, zc], axis=1).astype(BF16)
    w_kt = k.T.astype(BF16)
    bpad = jnp.zeros((LANES - UNITS,), b_gate_l.dtype)
    b_i = jnp.concatenate([b_gate_l[0:8], b_gate_l[16:24], bpad]).reshape(1, LANES)
    b_f = jnp.concatenate([b_gate_l[8:16], b_gate_l[24:32], bpad]).reshape(1, LANES)
    return w_nat, w_kt, b_i, b_f


def kernel(x, c, ctx, c_ctx, w_mod, b_mod, g_pre, g_post, w_in, b_gate, g_hnorm, g_sgu, w_sp, b_sp, w_fno, b_fno, w_out):
    bsz, t_lat, d = x.shape
    t_ctx = ctx.shape[1]
    depth = w_mod.shape[0]
    assert d == D_MODEL and t_lat % (2 * CHUNK) == 0 and t_ctx % (2 * CHUNK) == 0 and bsz + 1 <= 8

    cc = jnp.concatenate([c, c_ctx[None, :], jnp.zeros((8 - bsz - 1, d), c.dtype)], axis=0)
    mod = _modulation(cc, w_mod, b_mod)

    consts_lat, consts_ctx = _dft_consts(t_lat), _dft_consts(t_ctx)
    cst = jnp.asarray(consts_lat["cs"]).astype(BF16)

    def fourier(xr, xi, consts, bfno):
        if "dense" in consts:
            return _fourier_dense(xr, xi, jnp.asarray(consts["dense"]).astype(BF16), bfno)
        return _fourier_fused(xr, xi, jnp.asarray(consts["f1"]).astype(BF16),
                              jnp.asarray(consts["mtab"]).astype(BF16), bfno, n1=CHUNK)

    cn_zero = jnp.zeros((bsz, UNITS, DQK, 2 * DV), F32)
    m_zero = jnp.zeros((bsz, UNITS, LANES), F32)
    xc = ctx
    for l in range(depth):
        sh_l, sc_l, gt_l = (mod[l, :bsz, i * d:(i + 1) * d].reshape(bsz, 1, d) for i in range(3))
        sh_c, sc_c, gt_c = (jnp.broadcast_to(mod[l, bsz, i * d:(i + 1) * d].reshape(1, 1, d), (bsz, 1, d))
                            for i in range(3))
        w_nat, w_kt, b_i, b_f = _layer_weights(w_in[l], b_gate[l])
        gpre = g_pre[l].reshape(1, d)
        wf = w_fno[l].astype(BF16)
        wsp = w_sp[l].astype(BF16)
        bsp = jnp.broadcast_to(b_sp[l].T[:, :, None], (CHUNK, GROUPS, GC)).reshape(CHUNK, S_WIDTH)
        wo = w_out[l].astype(BF16)
        tail = (wo, g_hnorm[l].reshape(1, M_WIDTH), g_sgu[l].reshape(1, S_WIDTH), g_post[l].reshape(1, d), wsp, bsp)
        front = functools.partial(_inproj, g_pre=gpre, w_nat=w_nat, w_kt=w_kt, cst=cst, wf=wf, b_i=b_i, b_f=b_f)

        last = l == depth - 1
        pc = front(xc, sc_c, sh_c, full=not last, tm=256)
        hf_c, hb_c, cn_c, m_c = _mlstm(pc["q"], pc["kT"], pc["v"], pc["cola"], pc["colb"], pc["rowa"], cn_zero, m_zero)
        p = front(x, sc_l, sh_l, full=True, tm=512)
        hf, hb, _, _ = _mlstm(p["q"], p["kT"], p["v"], p["cola"], p["colb"], p["rowa"], cn_c, m_c)
        ym = fourier(p["xr"], p["xi"], consts_lat, b_fno[l])
        x = _outproj(hf, hb, p["o"], p["za"], p["u"], p["vs"], p["zb"], ym, p["zc"], x, gt_l, *tail, tm=512)
        if not last:
            ymc = fourier(pc["xr"], pc["xi"], consts_ctx, b_fno[l])
            xc = _outproj(hf_c, hb_c, pc["o"], pc["za"], pc["u"], pc["vs"], pc["zb"], ymc, pc["zc"], xc, gt_c,
                          *tail, tm=256)
    return x
```

```python
import functools

import numpy as np
import jax
import jax.numpy as jnp
from jax.experimental import pallas as pl
from jax.experimental.pallas import tpu as pltpu

EPS = 1e-6
LOG2E = 1.4426950408889634
D_MODEL = 1024
HEADS = 8
DV = 128
DQK = 64
QK_W = HEADS * DQK
M_WIDTH = HEADS * DV
CHUNK = 128
S_WIDTH = 512
F_WIDTH = 512
GROUPS = 4
GC = 128
UNITS = 2 * HEADS
LANES = 128

VMEM_LIMIT = 56 * 1024 * 1024

F32 = jnp.float32
BF16 = jnp.bfloat16


def _cparams(*sem):
    return pltpu.CompilerParams(dimension_semantics=sem, vmem_limit_bytes=VMEM_LIMIT)


def _sigmoid(x):
    return 1.0 / (1.0 + jnp.exp(-x))


def _silu(x):
    return x * _sigmoid(x)


def _mod_kernel(cc_ref, w_ref, b_ref, o_ref):
    s = _silu(cc_ref[...]).astype(BF16)
    w = w_ref[0].astype(BF16)
    o_ref[0] = jnp.dot(s, w, preferred_element_type=F32) + b_ref[0]


def _modulation(cc, w_mod, b_mod):
    depth, d, d3 = w_mod.shape
    return pl.pallas_call(
        _mod_kernel,
        grid=(depth,),
        in_specs=[pl.BlockSpec((8, d), lambda l: (0, 0)),
                  pl.BlockSpec((1, d, d3), lambda l: (l, 0, 0)),
                  pl.BlockSpec((1, 1, d3), lambda l: (l, 0, 0))],
        out_specs=pl.BlockSpec((1, 8, d3), lambda l: (l, 0, 0)),
        out_shape=jax.ShapeDtypeStruct((depth, 8, d3), F32),
        compiler_params=_cparams("arbitrary"),
        name="modulation",
    )(cc, w_mod, b_mod.reshape(depth, 1, d3))


def _log_sigmoid(x):
    return jnp.minimum(x, 0.0) - jnp.log1p(jnp.exp(-jnp.abs(x)))


def _gate_scans(gi, gf, cola_ref, colb_ref, rowa_ref):
    tm = gi.shape[0]
    row = jax.lax.broadcasted_iota(jnp.int32, (CHUNK, LANES), 0)
    fwd = jax.lax.broadcasted_iota(jnp.int32, (CHUNK, LANES), 1) < HEADS
    lf = _log_sigmoid(gf)

    def scan(x, op):
        pre, suf = x, x
        k = 1
        while k < CHUNK:
            sh = pltpu.roll(pre, k, axis=0)
            pre = jnp.where(row >= k, op(pre, sh), pre)
            sh = pltpu.roll(suf, CHUNK - k, axis=0)
            suf = jnp.where(row < CHUNK - k, op(suf, sh), suf)
            k *= 2
        return jnp.where(fwd, pre, suf)

    for c in range(tm // CHUNK):
        rows = slice(c * CHUNK, (c + 1) * CHUNK)
        b = scan(lf[rows], jnp.add)
        a = (gi[rows] - b) * LOG2E
        cola_ref[0, rows, :] = scan(a, jnp.maximum)
        colb_ref[0, rows, :] = b * LOG2E
        rowa_ref[0, :, rows] = a.T[0:UNITS, :]


def _pitch(n):
    p = n
    while (p // 8) % 2 == 0:
        p += 8
    return p


def _store_slabs(ref, g, z, n2):
    pin = _pitch(n2)
    for r in range(z.shape[0] // n2):
        ref[0, g, r * pin:r * pin + n2, :] = z[r * n2:(r + 1) * n2]
        ref[0, g, r * pin + n2:(r + 1) * pin, :] = jnp.zeros((pin - n2, GC), F32)


def _inproj_kernel(x_ref, sc_ref, sh_ref, g_ref, wn_ref, wkt_ref, cst_ref, wf_ref, bi_ref, bf_ref, *out_refs,
                   pieces, slab_n2):
    x = x_ref[0]
    y = x * jax.lax.rsqrt(jnp.mean(x * x, axis=-1, keepdims=True) + EPS)
    h = (y * g_ref[...]) * (1.0 + sc_ref[0]) + sh_ref[0]
    hb = h.astype(BF16)
    oi = 0
    for name, off, width in pieces:
        if name == "kT":
            r = jax.lax.dot_general(wkt_ref[...], hb, (((1,), (1,)), ((), ())),
                                    preferred_element_type=F32)
            out_refs[oi][0] = r.astype(BF16)
            oi += 1
            continue
        r = jnp.dot(hb, wn_ref[:, off:off + width], preferred_element_type=F32)
        if name == "gates":
            _gate_scans(r[:, :LANES] + bi_ref[...], r[:, LANES:] + bf_ref[...], *out_refs[oi:oi + 3])
            oi += 3
        elif name == "f":
            fb = r.astype(BF16)
            xr_ref, xi_ref = out_refs[oi], out_refs[oi + 1]
            for g in range(GROUPS):
                wc = jnp.dot(cst_ref[...], wf_ref[g], preferred_element_type=F32)
                wcat = jnp.concatenate([wc[:GC], wc[GC:]], axis=1).astype(BF16)
                z = jnp.dot(fb[:, g * GC:(g + 1) * GC], wcat, preferred_element_type=F32)
                if slab_n2 is None:
                    xr_ref[0, :, g * GC:(g + 1) * GC] = z[:, :GC].astype(BF16)
                    xi_ref[0, :, g * GC:(g + 1) * GC] = z[:, GC:].astype(BF16)
                else:
                    _store_slabs(xr_ref, g, z[:, :GC], slab_n2)
                    _store_slabs(xi_ref, g, z[:, GC:], slab_n2)
            oi += 2
        else:
            out_refs[oi][0] = r.astype(BF16)
            oi += 1


_FULL_PIECES = (("q", QK_W), ("v", M_WIDTH), ("gates", 2 * LANES), ("o", M_WIDTH), ("za", M_WIDTH),
                ("u", S_WIDTH), ("vs", S_WIDTH), ("zb", S_WIDTH), ("f", F_WIDTH), ("zc", F_WIDTH))
_STATE_PIECES = _FULL_PIECES[:3]


def _inproj(x, sc, sh, g_pre, w_nat, w_kt, cst, wf, b_i, b_f, *, layer, full, tm):
    bsz, t, d = x.shape
    names = _FULL_PIECES if full else _STATE_PIECES
    slab_n2 = t // CHUNK if t > 2 * CHUNK else None
    row_spec = lambda width: pl.BlockSpec((1, tm, width), lambda b, i: (b, i, 0))
    pieces, off = [], 0
    out_names, out_shapes, out_specs = [], [], []
    for name, width in names:
        pieces.append((name, off, width))
        off += width
        if name == "gates":
            out_names += ["cola", "colb", "rowa"]
            out_shapes += [jax.ShapeDtypeStruct((bsz, t, LANES), F32)] * 2 + [jax.ShapeDtypeStruct((bsz, UNITS, t), F32)]
            out_specs += [row_spec(LANES), row_spec(LANES), pl.BlockSpec((1, UNITS, tm), lambda b, i: (b, 0, i))]
        elif name == "f":
            out_names += ["xr", "xi"]
            if slab_n2 is None:
                out_shapes += [jax.ShapeDtypeStruct((bsz, t, width), BF16)] * 2
                out_specs += [row_spec(width)] * 2
            else:
                assert tm % slab_n2 == 0
                pin = _pitch(slab_n2)
                out_shapes += [jax.ShapeDtypeStruct((bsz, GROUPS, CHUNK * pin, GC), F32)] * 2
                out_specs += [pl.BlockSpec((1, GROUPS, tm // slab_n2 * pin, GC), lambda b, i: (b, 0, i, 0))] * 2
        else:
            out_names.append(name)
            out_shapes.append(jax.ShapeDtypeStruct((bsz, t, width), BF16))
            out_specs.append(row_spec(width))
    n_nat = off
    pieces.append(("kT", 0, QK_W))
    out_names.append("kT")
    out_shapes.append(jax.ShapeDtypeStruct((bsz, QK_W, t), BF16))
    out_specs.append(pl.BlockSpec((1, QK_W, tm), lambda b, i: (b, 0, i)))
    const = lambda shape, **kw: pl.BlockSpec(shape, lambda b, i: (0,) * len(shape), **kw)
    outs = pl.pallas_call(
        functools.partial(_inproj_kernel, pieces=tuple(pieces), slab_n2=slab_n2),
        grid=(bsz, t // tm),
        in_specs=[pl.BlockSpec((1, tm, d), lambda b, i: (b, i, 0)),
                  pl.BlockSpec((1, 1, d), lambda b, i: (b, 0, 0)),
                  pl.BlockSpec((1, 1, d), lambda b, i: (b, 0, 0)),
                  const((1, d)),
                  pl.BlockSpec((None, d, n_nat), lambda b, i: (layer, 0, 0), pipeline_mode=pl.Buffered(1)),
                  pl.BlockSpec((None, QK_W, d), lambda b, i: (layer, 0, 0), pipeline_mode=pl.Buffered(1)),
                  const((2 * GC, GC)), const((GROUPS, GC, GC)), const((1, LANES)), const((1, LANES))],
        out_specs=out_specs,
        out_shape=out_shapes,
        compiler_params=_cparams("parallel", "parallel"),
        name="inproj_full" if full else "inproj_state",
    )(x, sc, sh, g_pre, w_nat, w_kt, cst, wf, b_i, b_f)
    return dict(zip(out_names, outs))


def _mlstm_unit(h, d, sub, q_ref, kt_ref, v_ref, rowa_ref, m_diag, g_cols, c_cols, g_ends, hout_ref, cn_ref, mask):
    i = d * HEADS + h
    pair = h // 2
    rows = slice(sub * CHUNK, (sub + 1) * CHUNK)
    q_pair = q_ref[0, rows, pair * LANES:(pair + 1) * LANES]
    kt_h = kt_ref[0, h * DQK:(h + 1) * DQK, rows]
    zk = jnp.zeros((DQK, CHUNK), BF16)
    kt_ext = jnp.concatenate([kt_h, zk] if h % 2 == 0 else [zk, kt_h], axis=0)
    v_h = v_ref[0, rows, h * DV:(h + 1) * DV]
    vaug = jnp.concatenate([v_h, jnp.ones((CHUNK, DV), BF16)], axis=1)
    g = jnp.broadcast_to(g_cols[:, i:i + 1], (CHUNK, CHUNK))
    c = jnp.broadcast_to(c_cols[:, i:i + 1], (CHUNK, CHUNK))
    g_end = jnp.broadcast_to(g_ends[:, i:i + 1], (1, LANES))
    m_row = jnp.broadcast_to(m_diag[:, i:i + 1], (1, LANES))
    a_row = rowa_ref[0, h:h + 1, rows]
    cn = cn_ref[0, i]

    dmat = jnp.where(mask, jnp.exp2(a_row - g), 0.0)
    s = jnp.dot(q_pair, kt_ext, preferred_element_type=F32)
    p = (s * dmat).astype(BF16)
    qs = q_pair * jnp.exp2(m_row - g).astype(BF16)
    lhs = jnp.concatenate([p, qs], axis=1)
    zc = jnp.zeros((DQK, 2 * DV), BF16)
    cnb = cn.astype(BF16)
    cn_ext = jnp.concatenate([cnb, zc] if h % 2 == 0 else [zc, cnb], axis=0)
    rhs = jnp.concatenate([vaug, cn_ext], axis=0)
    out = jnp.dot(lhs, rhs, preferred_element_type=F32)
    num, den = out[:, :DV], out[:, DV:]
    hval = num / jnp.maximum(jnp.abs(den), jnp.exp2(c))
    hout_ref[0, rows, h * DV:(h + 1) * DV] = hval.astype(hout_ref.dtype)

    kts = kt_h * jnp.exp2(a_row - g_end).astype(BF16)
    upd = jnp.dot(kts, vaug, preferred_element_type=F32)
    decay = jnp.exp2(m_row - g_end)
    cn_ref[0, i] = jnp.concatenate([decay, decay], axis=1) * cn + upd


MLSTM_SUB = 4


def _mlstm_kernel(qf_ref, ktf_ref, vf_ref, caf_ref, cbf_ref, raf_ref,
                  qb_ref, ktb_ref, vb_ref, cab_ref, cbb_ref, rab_ref,
                  cn0_ref, m0_ref, hf_ref, hb_ref, cn_ref, m_ref, *, n_sub):
    @pl.when(pl.program_id(1) == 0)
    def _():
        cn_ref[...] = cn0_ref[...]
        m_ref[...] = m0_ref[...]

    t_idx = jax.lax.broadcasted_iota(jnp.int32, (CHUNK, CHUNK), 0)
    s_idx = jax.lax.broadcasted_iota(jnp.int32, (CHUNK, CHUNK), 1)
    unit_row = jax.lax.broadcasted_iota(jnp.int32, (UNITS, LANES), 0)
    unit_lane = jax.lax.broadcasted_iota(jnp.int32, (UNITS, LANES), 1)
    m_diag = jnp.sum(jnp.where(unit_row == unit_lane, m_ref[0], 0.0), axis=0, keepdims=True)
    fwd_lane = jax.lax.broadcasted_iota(jnp.int32, (1, LANES), 1) < HEADS

    def columns(ca_ref, cb_ref, sub, end):
        rows = slice(sub * CHUNK, (sub + 1) * CHUNK)
        g_cols = jnp.maximum(m_diag, ca_ref[0, rows, :])
        g_ends = g_cols[end:end + 1, :]
        m_new = cb_ref[0, sub * CHUNK + end:sub * CHUNK + end + 1, :] + g_ends
        return (g_cols, -(cb_ref[0, rows, :] + g_cols), g_ends), m_new

    for step in range(n_sub):
        sub_f, sub_b = step, n_sub - 1 - step
        cols_f, m_new_f = columns(caf_ref, cbf_ref, sub_f, CHUNK - 1)
        cols_b, m_new_b = columns(cab_ref, cbb_ref, sub_b, 0)
        for h in range(HEADS):
            _mlstm_unit(h, 0, sub_f, qf_ref, ktf_ref, vf_ref, raf_ref, m_diag, *cols_f, hf_ref, cn_ref, s_idx <= t_idx)
            _mlstm_unit(h, 1, sub_b, qb_ref, ktb_ref, vb_ref, rab_ref, m_diag, *cols_b, hb_ref, cn_ref, s_idx >= t_idx)
        m_diag = jnp.where(fwd_lane, m_new_f, m_new_b)
    m_ref[0] = jnp.where(unit_row == unit_lane, jnp.broadcast_to(m_diag, (UNITS, LANES)), 0.0)


def _mlstm(q, kt, v, cola, colb, rowa, cn0, m0):
    bsz, t, _ = q.shape
    n_sub = min(MLSTM_SUB, t // CHUNK)
    blk = n_sub * CHUNK
    nc = t // blk

    def specs(rev):
        cj = (lambda j: nc - 1 - j) if rev else (lambda j: j)
        d = 1 if rev else 0
        return [pl.BlockSpec((1, blk, QK_W), lambda b, j: (b, cj(j), 0)),
                pl.BlockSpec((1, QK_W, blk), lambda b, j: (b, 0, cj(j))),
                pl.BlockSpec((1, blk, M_WIDTH), lambda b, j: (b, cj(j), 0)),
                pl.BlockSpec((1, blk, LANES), lambda b, j: (b, cj(j), 0)),
                pl.BlockSpec((1, blk, LANES), lambda b, j: (b, cj(j), 0)),
                pl.BlockSpec((1, HEADS, blk), lambda b, j: (b, d, cj(j)))]

    cn_spec = pl.BlockSpec((1, UNITS, DQK, 2 * DV), lambda b, j: (b, 0, 0, 0))
    m_spec = pl.BlockSpec((1, UNITS, LANES), lambda b, j: (b, 0, 0))
    args = (q, kt, v, cola, colb, rowa)
    return pl.pallas_call(
        functools.partial(_mlstm_kernel, n_sub=n_sub),
        grid=(bsz, nc),
        in_specs=specs(False) + specs(True) + [cn_spec, m_spec],
        out_specs=[pl.BlockSpec((1, blk, M_WIDTH), lambda b, j: (b, j, 0)),
                   pl.BlockSpec((1, blk, M_WIDTH), lambda b, j: (b, nc - 1 - j, 0)),
                   cn_spec, m_spec],
        out_shape=[jax.ShapeDtypeStruct((bsz, t, M_WIDTH), BF16),
                   jax.ShapeDtypeStruct((bsz, t, M_WIDTH), BF16),
                   jax.ShapeDtypeStruct((bsz, UNITS, DQK, 2 * DV), F32),
                   jax.ShapeDtypeStruct((bsz, UNITS, LANES), F32)],
        compiler_params=_cparams("parallel", "arbitrary"),
        name="mlstm",
    )(*args, *args, cn0, m0)


OUT_PITCH = _pitch(CHUNK)


def _fourier_dense_kernel(xr_ref, xi_ref, cst_ref, bf_ref, o_ref):
    t = xr_ref.shape[1]
    xx = jnp.concatenate([xr_ref[0], xi_ref[0]], axis=0)
    yr = jnp.dot(cst_ref[...], xx, preferred_element_type=F32)
    for g in range(GROUPS):
        for c in range(t // CHUNK):
            o_ref[0, g, c * OUT_PITCH:c * OUT_PITCH + CHUNK, :] = (
                yr[c * CHUNK:(c + 1) * CHUNK, g * GC:(g + 1) * GC] + bf_ref[g])
            o_ref[0, g, c * OUT_PITCH + CHUNK:(c + 1) * OUT_PITCH, :] = jnp.zeros((OUT_PITCH - CHUNK, GC), F32)


def _fourier_dense(xr, xi, cst, bfno):
    bsz, t, w = xr.shape
    blk = pl.BlockSpec((1, t, w), lambda b: (b, 0, 0))
    rows = t // CHUNK * OUT_PITCH
    return pl.pallas_call(
        _fourier_dense_kernel,
        grid=(bsz,),
        in_specs=[blk, blk,
                  pl.BlockSpec((t, 2 * t), lambda b: (0, 0)),
                  pl.BlockSpec((GROUPS, 1, GC), lambda b: (0, 0, 0))],
        out_specs=pl.BlockSpec((1, GROUPS, rows, GC), lambda b: (b, 0, 0, 0)),
        out_shape=jax.ShapeDtypeStruct((bsz, GROUPS, rows, GC), F32),
        compiler_params=_cparams("parallel"),
        name="fourier_dense",
    )(xr, xi, cst, bfno.reshape(GROUPS, 1, GC))


def _fourier_fused_kernel(xr_ref, xi_ref, f1_ref, m_ref, bf_ref, o_ref, gr_scr, gi_scr, *, n1, n2):
    pin = _pitch(n2)

    def stage1(j, carry):
        t2 = 2 * j

        def ld(ref, s):
            return ref[0, 0, pl.ds(s, n1, stride=pin), :].astype(BF16)

        top = jnp.concatenate([ld(xr_ref, t2), ld(xr_ref, t2 + 1)], axis=1)
        bot = jnp.concatenate([ld(xi_ref, t2), ld(xi_ref, t2 + 1)], axis=1)
        g = jnp.dot(f1_ref[...], jnp.concatenate([top, bot], axis=0), preferred_element_type=F32)
        gr_scr[pl.ds(t2, n1, stride=pin), :] = g[:n1, :GC]
        gr_scr[pl.ds(t2 + 1, n1, stride=pin), :] = g[:n1, GC:]
        gi_scr[pl.ds(t2, n1, stride=pin), :] = g[n1:, :GC]
        gi_scr[pl.ds(t2 + 1, n1, stride=pin), :] = g[n1:, GC:]
        return carry

    jax.lax.fori_loop(0, n2 // 2, stage1, 0, unroll=min(4, n2 // 2))

    def stage2(k1, carry):
        base = pl.multiple_of(k1 * pin, 8)
        gg = jnp.concatenate([gr_scr[pl.ds(base, n2), :], gi_scr[pl.ds(base, n2), :]], axis=0).astype(BF16)
        yr = jnp.dot(m_ref[k1], gg, preferred_element_type=F32) + bf_ref[0]
        o_ref[0, 0, pl.ds(k1, n2, stride=OUT_PITCH), :] = yr
        return carry

    jax.lax.fori_loop(0, n1, stage2, 0, unroll=8)
    for k2 in range(n2):
        o_ref[0, 0, k2 * OUT_PITCH + n1:(k2 + 1) * OUT_PITCH, :] = jnp.zeros((OUT_PITCH - n1, GC), F32)


def _fourier_fused(xr, xi, f1, mtab, bfno, *, n1):
    bsz, _, rows_in, _ = xr.shape
    n2 = mtab.shape[1]
    pin = _pitch(n2)
    assert rows_in == n1 * pin and n1 == CHUNK
    in_blk = pl.BlockSpec((1, 1, rows_in, GC), lambda b, g: (b, g, 0, 0))
    return pl.pallas_call(
        functools.partial(_fourier_fused_kernel, n1=n1, n2=n2),
        grid=(bsz, GROUPS),
        in_specs=[in_blk, in_blk,
                  pl.BlockSpec((2 * n1, 2 * n1), lambda b, g: (0, 0)),
                  pl.BlockSpec((n1, n2, 2 * n2), lambda b, g: (0, 0, 0)),
                  pl.BlockSpec((1, 1, GC), lambda b, g: (g, 0, 0))],
        out_specs=pl.BlockSpec((1, 1, n2 * OUT_PITCH, GC), lambda b, g: (b, g, 0, 0)),
        out_shape=jax.ShapeDtypeStruct((bsz, GROUPS, n2 * OUT_PITCH, GC), F32),
        scratch_shapes=[pltpu.VMEM((n1 * pin, GC), F32), pltpu.VMEM((n1 * pin, GC), F32)],
        compiler_params=_cparams("parallel", "parallel"),
        name="fourier_fused",
    )(xr, xi, f1, mtab, bfno.reshape(GROUPS, 1, GC))


def _dft_consts(t):
    ang = 2.0 * np.pi / GC * np.outer(np.arange(GC), np.arange(GC))
    cs = np.concatenate([np.cos(ang), -np.sin(ang)], axis=0) / np.sqrt(GC)
    out = {"cs": cs.astype(np.float32)}
    if t <= 2 * CHUNK:
        ang = 2.0 * np.pi / t * np.mod(np.outer(np.arange(t), np.arange(t)), t)
        out["dense"] = (np.concatenate([np.cos(ang), np.sin(ang)], axis=1) / np.sqrt(t)).astype(np.float32)
    else:
        n1 = CHUNK
        n2 = t // n1
        ang = 2.0 * np.pi / n1 * np.mod(np.outer(np.arange(n1), np.arange(n1)), n1)
        c, s = np.cos(ang), np.sin(ang)
        out["f1"] = (np.block([[c, s], [-s, c]]) / np.sqrt(n1)).astype(np.float32)
        k = np.arange(n1)[:, None, None] + n1 * np.arange(n2)[None, :, None]
        ang = 2.0 * np.pi / t * np.mod(k * np.arange(n2)[None, None, :], t)
        out["mtab"] = (np.concatenate([np.cos(ang), np.sin(ang)], axis=2) / np.sqrt(n2)).astype(np.float32)
    return out


def _outproj_kernel(hf_ref, hb_ref, o_ref, za_ref, u_ref, vs_ref, zb_ref, ym_ref, zc_ref, x_ref, gt_ref,
                    wout_ref, ghn_ref, gsgu_ref, gpost_ref, wsp_ref, bsp_ref, xo_ref, y_scr):
    tm = x_ref.shape[1]
    kc = S_WIDTH

    def project(k):
        return jnp.dot(y_scr[:, k * kc:(k + 1) * kc], wout_ref[k * kc:(k + 1) * kc, :], preferred_element_type=F32)

    def silu(z):
        t = z * 0.5
        return t + t * jnp.tanh(t)

    out = None
    for h in range(HEADS):
        sl = slice(h * DV, (h + 1) * DV)
        hh = (hf_ref[0, :, sl] + hb_ref[0, :, sl]).astype(F32)
        hn = hh * jax.lax.rsqrt(jnp.mean(hh * hh, axis=-1, keepdims=True) + EPS) * ghn_ref[:, sl]
        o, za = o_ref[0, :, sl], za_ref[0, :, sl]
        y_scr[:, sl] = hn.astype(BF16) * ((0.5 + 0.5 * jnp.tanh(o * 0.5)) * silu(za))
        if (h + 1) * DV % kc == 0:
            part = project((h + 1) * DV // kc - 1)
            out = part if out is None else out + part
    vs = vs_ref[0].astype(F32)
    vn = (vs * jax.lax.rsqrt(jnp.mean(vs * vs, axis=-1, keepdims=True) + EPS) * gsgu_ref[...]).astype(BF16)
    for c in range(tm // CHUNK):
        rows = slice(c * CHUNK, (c + 1) * CHUNK)
        for g in range(GROUPS):
            cols = slice(g * GC, (g + 1) * GC)
            mixed = jnp.dot(wsp_ref[g], vn[rows, cols], preferred_element_type=F32) + bsp_ref[:, cols]
            zb = zb_ref[0, rows, cols]
            y_scr[rows, M_WIDTH + g * GC:M_WIDTH + (g + 1) * GC] = (
                u_ref[0, rows, cols] * mixed.astype(BF16) * silu(zb))
    out = out + project(M_WIDTH // kc)
    for c in range(tm // CHUNK):
        rows = slice(c * CHUNK, (c + 1) * CHUNK)
        for g in range(GROUPS):
            cols = slice(g * GC, (g + 1) * GC)
            zc = zc_ref[0, rows, cols]
            y_scr[rows, M_WIDTH + S_WIDTH + g * GC:M_WIDTH + S_WIDTH + (g + 1) * GC] = (
                ym_ref[0, g, c * OUT_PITCH:c * OUT_PITCH + CHUNK, :].astype(BF16) * silu(zc))
    out = out + project((M_WIDTH + S_WIDTH) // kc)
    on = out * jax.lax.rsqrt(jnp.mean(out * out, axis=-1, keepdims=True) + EPS) * gpost_ref[...]
    xo_ref[0] = x_ref[0] + gt_ref[0] * on


def _outproj(hf, hb, o, za, u, vs, zb, ym, zc, x, gt, w_out, ghn, gsgu, gpost, wsp, bsp, *, tm):
    bsz, t, d = x.shape
    wide = pl.BlockSpec((1, tm, M_WIDTH), lambda b, i: (b, i, 0))
    half = pl.BlockSpec((1, tm, S_WIDTH), lambda b, i: (b, i, 0))
    const2 = lambda shape: pl.BlockSpec(shape, lambda b, i: (0,) * len(shape))
    return pl.pallas_call(
        _outproj_kernel,
        grid=(bsz, t // tm),
        in_specs=[wide, wide, wide, wide, half, half, half,
                  pl.BlockSpec((1, GROUPS, tm // CHUNK * OUT_PITCH, GC), lambda b, i: (b, 0, i, 0)),
                  half, wide,
                  pl.BlockSpec((1, 1, d), lambda b, i: (b, 0, 0)),
                  const2((2 * D_MODEL, d)), const2((1, M_WIDTH)), const2((1, S_WIDTH)), const2((1, d)),
                  const2((GROUPS, CHUNK, CHUNK)), const2((CHUNK, S_WIDTH))],
        out_specs=wide,
        out_shape=jax.ShapeDtypeStruct((bsz, t, d), F32),
        scratch_shapes=[pltpu.VMEM((tm, 2 * D_MODEL), BF16)],
        compiler_params=_cparams("parallel", "parallel"),
        name="outproj",
    )(hf, hb, o, za, u, vs, zb, ym, zc, x, gt, w_out, ghn, gsgu, gpost, wsp, bsp)


_W_IN_SPLITS = np.cumsum([0, QK_W, QK_W, M_WIDTH, 4 * HEADS, M_WIDTH, M_WIDTH, S_WIDTH, S_WIDTH, S_WIDTH, F_WIDTH,
                          F_WIDTH])
_N_NAT = sum(w for _, w in _FULL_PIECES)


def _wprep_kernel(w_ref, wn_ref, wkt_ref):
    c = _W_IN_SPLITS
    rows = w_ref.shape[1]
    piece = lambda i: w_ref[0, :, c[i]:c[i + 1]]
    gates = piece(3)
    pad = jnp.zeros((rows, LANES - UNITS), F32)
    parts = [piece(0) * (DQK ** -0.5), piece(2),
             gates[:, 0:8], gates[:, 16:24], pad,
             gates[:, 8:16], gates[:, 24:32], pad]
    parts += [piece(i) for i in range(4, 11)]
    off = 0
    for p in parts:
        wn_ref[0, :, off:off + p.shape[1]] = p.astype(BF16)
        off += p.shape[1]
    wkt_ref[0] = piece(1).T.astype(BF16)


def _wprep(w_in):
    depth, d, p_in = w_in.shape
    rb = 128
    return pl.pallas_call(
        _wprep_kernel,
        grid=(depth, d // rb),
        in_specs=[pl.BlockSpec((1, rb, p_in), lambda l, i: (l, i, 0))],
        out_specs=[pl.BlockSpec((1, rb, _N_NAT), lambda l, i: (l, i, 0)),
                   pl.BlockSpec((1, QK_W, rb), lambda l, i: (l, 0, i))],
        out_shape=[jax.ShapeDtypeStruct((depth, d, _N_NAT), BF16),
                   jax.ShapeDtypeStruct((depth, QK_W, d), BF16)],
        compiler_params=_cparams("parallel", "parallel"),
        name="wprep",
    )(w_in)


def _gate_bias(b_gate_l):
    bpad = jnp.zeros((LANES - UNITS,), b_gate_l.dtype)
    b_i = jnp.concatenate([b_gate_l[0:8], b_gate_l[16:24], bpad]).reshape(1, LANES)
    b_f = jnp.concatenate([b_gate_l[8:16], b_gate_l[24:32], bpad]).reshape(1, LANES)
    return b_i, b_f


def kernel(x, c, ctx, c_ctx, w_mod, b_mod, g_pre, g_post, w_in, b_gate, g_hnorm, g_sgu, w_sp, b_sp, w_fno, b_fno, w_out):
    bsz, t_lat, d = x.shape
    t_ctx = ctx.shape[1]
    depth = w_mod.shape[0]
    assert d == D_MODEL and t_lat % (2 * CHUNK) == 0 and t_ctx % (2 * CHUNK) == 0 and bsz + 1 <= 8

    cc = jnp.concatenate([c, c_ctx[None, :], jnp.zeros((8 - bsz - 1, d), c.dtype)], axis=0)
    mod = _modulation(cc, w_mod, b_mod)
    w_nat_all, w_kt_all = _wprep(w_in)

    consts_lat, consts_ctx = _dft_consts(t_lat), _dft_consts(t_ctx)
    cst = jnp.asarray(consts_lat["cs"]).astype(BF16)

    def fourier(xr, xi, consts, bfno):
        if "dense" in consts:
            return _fourier_dense(xr, xi, jnp.asarray(consts["dense"]).astype(BF16), bfno)
        return _fourier_fused(xr, xi, jnp.asarray(consts["f1"]).astype(BF16),
                              jnp.asarray(consts["mtab"]).astype(BF16), bfno, n1=CHUNK)

    cn_zero = jnp.zeros((bsz, UNITS, DQK, 2 * DV), F32)
    m_zero = jnp.zeros((bsz, UNITS, LANES), F32)
    xc = ctx
    for l in range(depth):
        sh_l, sc_l, gt_l = (mod[l, :bsz, i * d:(i + 1) * d].reshape(bsz, 1, d) for i in range(3))
        sh_c, sc_c, gt_c = (jnp.broadcast_to(mod[l, bsz, i * d:(i + 1) * d].reshape(1, 1, d), (bsz, 1, d))
                            for i in range(3))
        b_i, b_f = _gate_bias(b_gate[l])
        gpre = g_pre[l].reshape(1, d)
        wf = w_fno[l].astype(BF16)
        wsp = w_sp[l].astype(BF16)
        bsp = jnp.broadcast_to(b_sp[l].T[:, :, None], (CHUNK, GROUPS, GC)).reshape(CHUNK, S_WIDTH)
        wo = w_out[l].astype(BF16)
        tail = (wo, g_hnorm[l].reshape(1, M_WIDTH), g_sgu[l].reshape(1, S_WIDTH), g_post[l].reshape(1, d), wsp, bsp)
        front = functools.partial(_inproj, g_pre=gpre, w_nat=w_nat_all, w_kt=w_kt_all, cst=cst, wf=wf, b_i=b_i,
                                  b_f=b_f, layer=l)

        last = l == depth - 1
        pc = front(xc, sc_c, sh_c, full=not last, tm=256)
        hf_c, hb_c, cn_c, m_c = _mlstm(pc["q"], pc["kT"], pc["v"], pc["cola"], pc["colb"], pc["rowa"], cn_zero, m_zero)
        p = front(x, sc_l, sh_l, full=True, tm=512)
        hf, hb, _, _ = _mlstm(p["q"], p["kT"], p["v"], p["cola"], p["colb"], p["rowa"], cn_c, m_c)
        ym = fourier(p["xr"], p["xi"], consts_lat, b_fno[l])
        x = _outproj(hf, hb, p["o"], p["za"], p["u"], p["vs"], p["zb"], ym, p["zc"], x, gt_l, *tail, tm=512)
        if not last:
            ymc = fourier(pc["xr"], pc["xi"], consts_ctx, b_fno[l])
            xc = _outproj(hf_c, hb_c, pc["o"], pc["za"], pc["u"], pc["vs"], pc["zb"], ymc, pc["zc"], xc, gt_c,
                          *tail, tm=256)
    return x
```

```python
import functools

import numpy as np
import jax
import jax.numpy as jnp
from jax.experimental import pallas as pl
from jax.experimental.pallas import tpu as pltpu

EPS = 1e-6
LOG2E = 1.4426950408889634
D_MODEL = 1024
HEADS = 8
DV = 128
DQK = 64
QK_W = HEADS * DQK
M_WIDTH = HEADS * DV
CHUNK = 128
S_WIDTH = 512
F_WIDTH = 512
GROUPS = 4
GC = 128
UNITS = 2 * HEADS
LANES = 128

VMEM_LIMIT = 56 * 1024 * 1024

F32 = jnp.float32
BF16 = jnp.bfloat16


def _cparams(*sem):
    return pltpu.CompilerParams(dimension_semantics=sem, vmem_limit_bytes=VMEM_LIMIT)


def _sigmoid(x):
    return 1.0 / (1.0 + jnp.exp(-x))


def _silu(x):
    return x * _sigmoid(x)


def _mod_kernel(cc_ref, w_ref, b_ref, o_ref):
    s = _silu(cc_ref[...]).astype(BF16)
    w = w_ref[0].astype(BF16)
    o_ref[0] = jnp.dot(s, w, preferred_element_type=F32) + b_ref[0]


def _modulation(cc, w_mod, b_mod):
    depth, d, d3 = w_mod.shape
    return pl.pallas_call(
        _mod_kernel,
        grid=(depth,),
        in_specs=[pl.BlockSpec((8, d), lambda l: (0, 0)),
                  pl.BlockSpec((1, d, d3), lambda l: (l, 0, 0)),
                  pl.BlockSpec((1, 1, d3), lambda l: (l, 0, 0))],
        out_specs=pl.BlockSpec((1, 8, d3), lambda l: (l, 0, 0)),
        out_shape=jax.ShapeDtypeStruct((depth, 8, d3), F32),
        compiler_params=_cparams("arbitrary"),
        name="modulation",
    )(cc, w_mod, b_mod.reshape(depth, 1, d3))


def _log_sigmoid(x):
    return jnp.minimum(x, 0.0) - jnp.log1p(jnp.exp(-jnp.abs(x)))


def _gate_scans(gi, gf, cola_ref, colb_ref, rowa_ref):
    tm = gi.shape[0]
    row = jax.lax.broadcasted_iota(jnp.int32, (CHUNK, LANES), 0)
    fwd = jax.lax.broadcasted_iota(jnp.int32, (CHUNK, LANES), 1) < HEADS
    lf = _log_sigmoid(gf)

    def scan(x, op):
        pre, suf = x, x
        k = 1
        while k < CHUNK:
            sh = pltpu.roll(pre, k, axis=0)
            pre = jnp.where(row >= k, op(pre, sh), pre)
            sh = pltpu.roll(suf, CHUNK - k, axis=0)
            suf = jnp.where(row < CHUNK - k, op(suf, sh), suf)
            k *= 2
        return jnp.where(fwd, pre, suf)

    for c in range(tm // CHUNK):
        rows = slice(c * CHUNK, (c + 1) * CHUNK)
        b = scan(lf[rows], jnp.add)
        a = (gi[rows] - b) * LOG2E
        cola_ref[0, rows, :] = scan(a, jnp.maximum)
        colb_ref[0, rows, :] = b * LOG2E
        rowa_ref[0, :, rows] = a.T[0:UNITS, :]


def _pitch(n):
    p = n
    while (p // 8) % 2 == 0:
        p += 8
    return p


def _store_slabs(ref, g, z, n2):
    pin = _pitch(n2)
    for r in range(z.shape[0] // n2):
        ref[0, g, r * pin:r * pin + n2, :] = z[r * n2:(r + 1) * n2]
        ref[0, g, r * pin + n2:(r + 1) * pin, :] = jnp.zeros((pin - n2, GC), F32)


def _inproj_kernel(x_ref, sc_ref, sh_ref, g_ref, wt_ref, cst_ref, wf_ref, bi_ref, bf_ref, *out_refs,
                   pieces, slab_n2):
    nt = (((1,), (1,)), ((), ()))
    x = x_ref[0]
    y = x * jax.lax.rsqrt(jnp.mean(x * x, axis=-1, keepdims=True) + EPS)
    h = (y * g_ref[...]) * (1.0 + sc_ref[0]) + sh_ref[0]
    hb = h.astype(BF16)
    oi = 0
    for name, off, width in pieces:
        if name == "kT":
            r = jax.lax.dot_general(wt_ref[0:QK_W, :], hb, nt, preferred_element_type=F32)
            out_refs[oi][0] = r.astype(BF16)
            oi += 1
            continue
        r = jax.lax.dot_general(hb, wt_ref[QK_W + off:QK_W + off + width, :], nt, preferred_element_type=F32)
        if name == "gates":
            _gate_scans(r[:, :LANES] + bi_ref[...], r[:, LANES:] + bf_ref[...], *out_refs[oi:oi + 3])
            oi += 3
        elif name == "f":
            fb = r.astype(BF16)
            xr_ref, xi_ref = out_refs[oi], out_refs[oi + 1]
            for g in range(GROUPS):
                wc = jnp.dot(cst_ref[...], wf_ref[g], preferred_element_type=F32)
                wcat = jnp.concatenate([wc[:GC], wc[GC:]], axis=1).astype(BF16)
                z = jnp.dot(fb[:, g * GC:(g + 1) * GC], wcat, preferred_element_type=F32)
                if slab_n2 is None:
                    xr_ref[0, :, g * GC:(g + 1) * GC] = z[:, :GC].astype(BF16)
                    xi_ref[0, :, g * GC:(g + 1) * GC] = z[:, GC:].astype(BF16)
                else:
                    _store_slabs(xr_ref, g, z[:, :GC], slab_n2)
                    _store_slabs(xi_ref, g, z[:, GC:], slab_n2)
            oi += 2
        else:
            out_refs[oi][0] = r.astype(BF16)
            oi += 1


_FULL_PIECES = (("q", QK_W), ("v", M_WIDTH), ("gates", 2 * LANES), ("o", M_WIDTH), ("za", M_WIDTH),
                ("u", S_WIDTH), ("vs", S_WIDTH), ("zb", S_WIDTH), ("f", F_WIDTH), ("zc", F_WIDTH))
_STATE_PIECES = _FULL_PIECES[:3]


def _inproj(x, sc, sh, g_pre, w_t, cst, wf, b_i, b_f, *, layer, full, tm):
    bsz, t, d = x.shape
    names = _FULL_PIECES if full else _STATE_PIECES
    slab_n2 = t // CHUNK if t > 2 * CHUNK else None
    row_spec = lambda width: pl.BlockSpec((1, tm, width), lambda b, i: (b, i, 0))
    pieces, off = [], 0
    out_names, out_shapes, out_specs = [], [], []
    for name, width in names:
        pieces.append((name, off, width))
        off += width
        if name == "gates":
            out_names += ["cola", "colb", "rowa"]
            out_shapes += [jax.ShapeDtypeStruct((bsz, t, LANES), F32)] * 2 + [jax.ShapeDtypeStruct((bsz, UNITS, t), F32)]
            out_specs += [row_spec(LANES), row_spec(LANES), pl.BlockSpec((1, UNITS, tm), lambda b, i: (b, 0, i))]
        elif name == "f":
            out_names += ["xr", "xi"]
            if slab_n2 is None:
                out_shapes += [jax.ShapeDtypeStruct((bsz, t, width), BF16)] * 2
                out_specs += [row_spec(width)] * 2
            else:
                assert tm % slab_n2 == 0
                pin = _pitch(slab_n2)
                out_shapes += [jax.ShapeDtypeStruct((bsz, GROUPS, CHUNK * pin, GC), F32)] * 2
                out_specs += [pl.BlockSpec((1, GROUPS, tm // slab_n2 * pin, GC), lambda b, i: (b, 0, i, 0))] * 2
        else:
            out_names.append(name)
            out_shapes.append(jax.ShapeDtypeStruct((bsz, t, width), BF16))
            out_specs.append(row_spec(width))
    n_nat = off
    pieces.append(("kT", 0, QK_W))
    out_names.append("kT")
    out_shapes.append(jax.ShapeDtypeStruct((bsz, QK_W, t), BF16))
    out_specs.append(pl.BlockSpec((1, QK_W, tm), lambda b, i: (b, 0, i)))
    const = lambda shape, **kw: pl.BlockSpec(shape, lambda b, i: (0,) * len(shape), **kw)
    outs = pl.pallas_call(
        functools.partial(_inproj_kernel, pieces=tuple(pieces), slab_n2=slab_n2),
        grid=(bsz, t // tm),
        in_specs=[pl.BlockSpec((1, tm, d), lambda b, i: (b, i, 0)),
                  pl.BlockSpec((1, 1, d), lambda b, i: (b, 0, 0)),
                  pl.BlockSpec((1, 1, d), lambda b, i: (b, 0, 0)),
                  const((1, d)),
                  pl.BlockSpec((None, QK_W + n_nat, d), lambda b, i: (layer, 0, 0), pipeline_mode=pl.Buffered(1)),
                  const((2 * GC, GC)), const((GROUPS, GC, GC)), const((1, LANES)), const((1, LANES))],
        out_specs=out_specs,
        out_shape=out_shapes,
        compiler_params=_cparams("parallel", "parallel"),
        name="inproj_full" if full else "inproj_state",
    )(x, sc, sh, g_pre, w_t, cst, wf, b_i, b_f)
    return dict(zip(out_names, outs))


def _mlstm_unit(h, d, sub, q_ref, kt_ref, v_ref, rowa_ref, m_diag, g_cols, c_cols, g_ends, hout_ref, cn_ref, mask):
    i = d * HEADS + h
    pair = h // 2
    rows = slice(sub * CHUNK, (sub + 1) * CHUNK)
    q_pair = q_ref[0, rows, pair * LANES:(pair + 1) * LANES]
    kt_h = kt_ref[0, h * DQK:(h + 1) * DQK, rows]
    zk = jnp.zeros((DQK, CHUNK), BF16)
    kt_ext = jnp.concatenate([kt_h, zk] if h % 2 == 0 else [zk, kt_h], axis=0)
    v_h = v_ref[0, rows, h * DV:(h + 1) * DV]
    vaug = jnp.concatenate([v_h, jnp.ones((CHUNK, DV), BF16)], axis=1)
    g = jnp.broadcast_to(g_cols[:, i:i + 1], (CHUNK, CHUNK))
    c = jnp.broadcast_to(c_cols[:, i:i + 1], (CHUNK, CHUNK))
    g_end = jnp.broadcast_to(g_ends[:, i:i + 1], (1, LANES))
    m_row = jnp.broadcast_to(m_diag[:, i:i + 1], (1, LANES))
    a_row = rowa_ref[0, h:h + 1, rows]
    cn = cn_ref[0, i]

    dmat = jnp.where(mask, jnp.exp2(a_row - g), 0.0)
    s = jnp.dot(q_pair, kt_ext, preferred_element_type=F32)
    p = (s * dmat).astype(BF16)
    qs = q_pair * jnp.exp2(m_row - g).astype(BF16)
    lhs = jnp.concatenate([p, qs], axis=1)
    zc = jnp.zeros((DQK, 2 * DV), BF16)
    cnb = cn.astype(BF16)
    cn_ext = jnp.concatenate([cnb, zc] if h % 2 == 0 else [zc, cnb], axis=0)
    rhs = jnp.concatenate([vaug, cn_ext], axis=0)
    out = jnp.dot(lhs, rhs, preferred_element_type=F32)
    num, den = out[:, :DV], out[:, DV:]
    hval = num / jnp.maximum(jnp.abs(den), jnp.exp2(c))
    hout_ref[0, rows, h * DV:(h + 1) * DV] = hval.astype(hout_ref.dtype)

    kts = kt_h * jnp.exp2(a_row - g_end).astype(BF16)
    upd = jnp.dot(kts, vaug, preferred_element_type=F32)
    decay = jnp.exp2(m_row - g_end)
    cn_ref[0, i] = jnp.concatenate([decay, decay], axis=1) * cn + upd


MLSTM_SUB = 4


def _mlstm_kernel(qf_ref, ktf_ref, vf_ref, caf_ref, cbf_ref, raf_ref,
                  qb_ref, ktb_ref, vb_ref, cab_ref, cbb_ref, rab_ref,
                  cn0_ref, m0_ref, hf_ref, hb_ref, cn_ref, m_ref, *, n_sub):
    @pl.when(pl.program_id(1) == 0)
    def _():
        cn_ref[...] = cn0_ref[...]
        m_ref[...] = m0_ref[...]

    t_idx = jax.lax.broadcasted_iota(jnp.int32, (CHUNK, CHUNK), 0)
    s_idx = jax.lax.broadcasted_iota(jnp.int32, (CHUNK, CHUNK), 1)
    unit_row = jax.lax.broadcasted_iota(jnp.int32, (UNITS, LANES), 0)
    unit_lane = jax.lax.broadcasted_iota(jnp.int32, (UNITS, LANES), 1)
    m_diag = jnp.sum(jnp.where(unit_row == unit_lane, m_ref[0], 0.0), axis=0, keepdims=True)
    fwd_lane = jax.lax.broadcasted_iota(jnp.int32, (1, LANES), 1) < HEADS

    def columns(ca_ref, cb_ref, sub, end):
        rows = slice(sub * CHUNK, (sub + 1) * CHUNK)
        g_cols = jnp.maximum(m_diag, ca_ref[0, rows, :])
        g_ends = g_cols[end:end + 1, :]
        m_new = cb_ref[0, sub * CHUNK + end:sub * CHUNK + end + 1, :] + g_ends
        return (g_cols, -(cb_ref[0, rows, :] + g_cols), g_ends), m_new

    for step in range(n_sub):
        sub_f, sub_b = step, n_sub - 1 - step
        cols_f, m_new_f = columns(caf_ref, cbf_ref, sub_f, CHUNK - 1)
        cols_b, m_new_b = columns(cab_ref, cbb_ref, sub_b, 0)
        for h in range(HEADS):
            _mlstm_unit(h, 0, sub_f, qf_ref, ktf_ref, vf_ref, raf_ref, m_diag, *cols_f, hf_ref, cn_ref, s_idx <= t_idx)
            _mlstm_unit(h, 1, sub_b, qb_ref, ktb_ref, vb_ref, rab_ref, m_diag, *cols_b, hb_ref, cn_ref, s_idx >= t_idx)
        m_diag = jnp.where(fwd_lane, m_new_f, m_new_b)
    m_ref[0] = jnp.where(unit_row == unit_lane, jnp.broadcast_to(m_diag, (UNITS, LANES)), 0.0)


def _mlstm(q, kt, v, cola, colb, rowa, cn0, m0):
    bsz, t, _ = q.shape
    n_sub = min(MLSTM_SUB, t // CHUNK)
    blk = n_sub * CHUNK
    nc = t // blk

    def specs(rev):
        cj = (lambda j: nc - 1 - j) if rev else (lambda j: j)
        d = 1 if rev else 0
        return [pl.BlockSpec((1, blk, QK_W), lambda b, j: (b, cj(j), 0)),
                pl.BlockSpec((1, QK_W, blk), lambda b, j: (b, 0, cj(j))),
                pl.BlockSpec((1, blk, M_WIDTH), lambda b, j: (b, cj(j), 0)),
                pl.BlockSpec((1, blk, LANES), lambda b, j: (b, cj(j), 0)),
                pl.BlockSpec((1, blk, LANES), lambda b, j: (b, cj(j), 0)),
                pl.BlockSpec((1, HEADS, blk), lambda b, j: (b, d, cj(j)))]

    cn_spec = pl.BlockSpec((1, UNITS, DQK, 2 * DV), lambda b, j: (b, 0, 0, 0))
    m_spec = pl.BlockSpec((1, UNITS, LANES), lambda b, j: (b, 0, 0))
    args = (q, kt, v, cola, colb, rowa)
    return pl.pallas_call(
        functools.partial(_mlstm_kernel, n_sub=n_sub),
        grid=(bsz, nc),
        in_specs=specs(False) + specs(True) + [cn_spec, m_spec],
        out_specs=[pl.BlockSpec((1, blk, M_WIDTH), lambda b, j: (b, j, 0)),
                   pl.BlockSpec((1, blk, M_WIDTH), lambda b, j: (b, nc - 1 - j, 0)),
                   cn_spec, m_spec],
        out_shape=[jax.ShapeDtypeStruct((bsz, t, M_WIDTH), BF16),
                   jax.ShapeDtypeStruct((bsz, t, M_WIDTH), BF16),
                   jax.ShapeDtypeStruct((bsz, UNITS, DQK, 2 * DV), F32),
                   jax.ShapeDtypeStruct((bsz, UNITS, LANES), F32)],
        compiler_params=_cparams("parallel", "arbitrary"),
        name="mlstm",
    )(*args, *args, cn0, m0)


OUT_PITCH = _pitch(CHUNK)


def _fourier_dense_kernel(xr_ref, xi_ref, cst_ref, bf_ref, o_ref):
    t = xr_ref.shape[1]
    xx = jnp.concatenate([xr_ref[0], xi_ref[0]], axis=0)
    yr = jnp.dot(cst_ref[...], xx, preferred_element_type=F32)
    for g in range(GROUPS):
        for c in range(t // CHUNK):
            o_ref[0, g, c * OUT_PITCH:c * OUT_PITCH + CHUNK, :] = (
                yr[c * CHUNK:(c + 1) * CHUNK, g * GC:(g + 1) * GC] + bf_ref[g])
            o_ref[0, g, c * OUT_PITCH + CHUNK:(c + 1) * OUT_PITCH, :] = jnp.zeros((OUT_PITCH - CHUNK, GC), F32)


def _fourier_dense(xr, xi, cst, bfno):
    bsz, t, w = xr.shape
    blk = pl.BlockSpec((1, t, w), lambda b: (b, 0, 0))
    rows = t // CHUNK * OUT_PITCH
    return pl.pallas_call(
        _fourier_dense_kernel,
        grid=(bsz,),
        in_specs=[blk, blk,
                  pl.BlockSpec((t, 2 * t), lambda b: (0, 0)),
                  pl.BlockSpec((GROUPS, 1, GC), lambda b: (0, 0, 0))],
        out_specs=pl.BlockSpec((1, GROUPS, rows, GC), lambda b: (b, 0, 0, 0)),
        out_shape=jax.ShapeDtypeStruct((bsz, GROUPS, rows, GC), F32),
        compiler_params=_cparams("parallel"),
        name="fourier_dense",
    )(xr, xi, cst, bfno.reshape(GROUPS, 1, GC))


def _fourier_fused_kernel(xr_ref, xi_ref, f1_ref, m_ref, bf_ref, o_ref, gr_scr, gi_scr, *, n1, n2):
    pin = _pitch(n2)

    def stage1(j, carry):
        t2 = 2 * j

        def ld(ref, s):
            return ref[0, 0, pl.ds(s, n1, stride=pin), :].astype(BF16)

        top = jnp.concatenate([ld(xr_ref, t2), ld(xr_ref, t2 + 1)], axis=1)
        bot = jnp.concatenate([ld(xi_ref, t2), ld(xi_ref, t2 + 1)], axis=1)
        g = jnp.dot(f1_ref[...], jnp.concatenate([top, bot], axis=0), preferred_element_type=F32)
        gr_scr[pl.ds(t2, n1, stride=pin), :] = g[:n1, :GC]
        gr_scr[pl.ds(t2 + 1, n1, stride=pin), :] = g[:n1, GC:]
        gi_scr[pl.ds(t2, n1, stride=pin), :] = g[n1:, :GC]
        gi_scr[pl.ds(t2 + 1, n1, stride=pin), :] = g[n1:, GC:]
        return carry

    jax.lax.fori_loop(0, n2 // 2, stage1, 0, unroll=min(4, n2 // 2))

    def stage2(k1, carry):
        base = pl.multiple_of(k1 * pin, 8)
        gg = jnp.concatenate([gr_scr[pl.ds(base, n2), :], gi_scr[pl.ds(base, n2), :]], axis=0).astype(BF16)
        yr = jnp.dot(m_ref[k1], gg, preferred_element_type=F32) + bf_ref[0]
        o_ref[0, 0, pl.ds(k1, n2, stride=OUT_PITCH), :] = yr
        return carry

    jax.lax.fori_loop(0, n1, stage2, 0, unroll=8)
    for k2 in range(n2):
        o_ref[0, 0, k2 * OUT_PITCH + n1:(k2 + 1) * OUT_PITCH, :] = jnp.zeros((OUT_PITCH - n1, GC), F32)


def _fourier_fused(xr, xi, f1, mtab, bfno, *, n1):
    bsz, _, rows_in, _ = xr.shape
    n2 = mtab.shape[1]
    pin = _pitch(n2)
    assert rows_in == n1 * pin and n1 == CHUNK
    in_blk = pl.BlockSpec((1, 1, rows_in, GC), lambda b, g: (b, g, 0, 0))
    return pl.pallas_call(
        functools.partial(_fourier_fused_kernel, n1=n1, n2=n2),
        grid=(bsz, GROUPS),
        in_specs=[in_blk, in_blk,
                  pl.BlockSpec((2 * n1, 2 * n1), lambda b, g: (0, 0)),
                  pl.BlockSpec((n1, n2, 2 * n2), lambda b, g: (0, 0, 0)),
                  pl.BlockSpec((1, 1, GC), lambda b, g: (g, 0, 0))],
        out_specs=pl.BlockSpec((1, 1, n2 * OUT_PITCH, GC), lambda b, g: (b, g, 0, 0)),
        out_shape=jax.ShapeDtypeStruct((bsz, GROUPS, n2 * OUT_PITCH, GC), F32),
        scratch_shapes=[pltpu.VMEM((n1 * pin, GC), F32), pltpu.VMEM((n1 * pin, GC), F32)],
        compiler_params=_cparams("parallel", "parallel"),
        name="fourier_fused",
    )(xr, xi, f1, mtab, bfno.reshape(GROUPS, 1, GC))


def _dft_consts(t):
    ang = 2.0 * np.pi / GC * np.outer(np.arange(GC), np.arange(GC))
    cs = np.concatenate([np.cos(ang), -np.sin(ang)], axis=0) / np.sqrt(GC)
    out = {"cs": cs.astype(np.float32)}
    if t <= 2 * CHUNK:
        ang = 2.0 * np.pi / t * np.mod(np.outer(np.arange(t), np.arange(t)), t)
        out["dense"] = (np.concatenate([np.cos(ang), np.sin(ang)], axis=1) / np.sqrt(t)).astype(np.float32)
    else:
        n1 = CHUNK
        n2 = t // n1
        ang = 2.0 * np.pi / n1 * np.mod(np.outer(np.arange(n1), np.arange(n1)), n1)
        c, s = np.cos(ang), np.sin(ang)
        out["f1"] = (np.block([[c, s], [-s, c]]) / np.sqrt(n1)).astype(np.float32)
        k = np.arange(n1)[:, None, None] + n1 * np.arange(n2)[None, :, None]
        ang = 2.0 * np.pi / t * np.mod(k * np.arange(n2)[None, None, :], t)
        out["mtab"] = (np.concatenate([np.cos(ang), np.sin(ang)], axis=2) / np.sqrt(n2)).astype(np.float32)
    return out


def _outproj_kernel(hf_ref, hb_ref, o_ref, za_ref, u_ref, vs_ref, zb_ref, ym_ref, zc_ref, x_ref, gt_ref,
                    wout_ref, ghn_ref, gsgu_ref, gpost_ref, wsp_ref, bsp_ref, xo_ref, y_scr):
    tm = x_ref.shape[1]
    kc = S_WIDTH

    def project(k):
        return jnp.dot(y_scr[:, k * kc:(k + 1) * kc], wout_ref[k * kc:(k + 1) * kc, :], preferred_element_type=F32)

    def silu(z):
        t = z * 0.5
        return t + t * jnp.tanh(t)

    out = None
    for h in range(HEADS):
        sl = slice(h * DV, (h + 1) * DV)
        hh = (hf_ref[0, :, sl] + hb_ref[0, :, sl]).astype(F32)
        hn = hh * jax.lax.rsqrt(jnp.mean(hh * hh, axis=-1, keepdims=True) + EPS) * ghn_ref[:, sl]
        o, za = o_ref[0, :, sl], za_ref[0, :, sl]
        y_scr[:, sl] = hn.astype(BF16) * ((0.5 + 0.5 * jnp.tanh(o * 0.5)) * silu(za))
        if (h + 1) * DV % kc == 0:
            part = project((h + 1) * DV // kc - 1)
            out = part if out is None else out + part
    vs = vs_ref[0].astype(F32)
    vn = (vs * jax.lax.rsqrt(jnp.mean(vs * vs, axis=-1, keepdims=True) + EPS) * gsgu_ref[...]).astype(BF16)
    for c in range(tm // CHUNK):
        rows = slice(c * CHUNK, (c + 1) * CHUNK)
        for g in range(GROUPS):
            cols = slice(g * GC, (g + 1) * GC)
            mixed = jnp.dot(wsp_ref[g], vn[rows, cols], preferred_element_type=F32) + bsp_ref[:, cols]
            zb = zb_ref[0, rows, cols]
            y_scr[rows, M_WIDTH + g * GC:M_WIDTH + (g + 1) * GC] = (
                u_ref[0, rows, cols] * mixed.astype(BF16) * silu(zb))
    out = out + project(M_WIDTH // kc)
    for c in range(tm // CHUNK):
        rows = slice(c * CHUNK, (c + 1) * CHUNK)
        for g in range(GROUPS):
            cols = slice(g * GC, (g + 1) * GC)
            zc = zc_ref[0, rows, cols]
            y_scr[rows, M_WIDTH + S_WIDTH + g * GC:M_WIDTH + S_WIDTH + (g + 1) * GC] = (
                ym_ref[0, g, c * OUT_PITCH:c * OUT_PITCH + CHUNK, :].astype(BF16) * silu(zc))
    out = out + project((M_WIDTH + S_WIDTH) // kc)
    on = out * jax.lax.rsqrt(jnp.mean(out * out, axis=-1, keepdims=True) + EPS) * gpost_ref[...]
    xo_ref[0] = x_ref[0] + gt_ref[0] * on


def _outproj(hf, hb, o, za, u, vs, zb, ym, zc, x, gt, w_out, ghn, gsgu, gpost, wsp, bsp, *, tm):
    bsz, t, d = x.shape
    wide = pl.BlockSpec((1, tm, M_WIDTH), lambda b, i: (b, i, 0))
    half = pl.BlockSpec((1, tm, S_WIDTH), lambda b, i: (b, i, 0))
    const2 = lambda shape: pl.BlockSpec(shape, lambda b, i: (0,) * len(shape))
    return pl.pallas_call(
        _outproj_kernel,
        grid=(bsz, t // tm),
        in_specs=[wide, wide, wide, wide, half, half, half,
                  pl.BlockSpec((1, GROUPS, tm // CHUNK * OUT_PITCH, GC), lambda b, i: (b, 0, i, 0)),
                  half, wide,
                  pl.BlockSpec((1, 1, d), lambda b, i: (b, 0, 0)),
                  const2((2 * D_MODEL, d)), const2((1, M_WIDTH)), const2((1, S_WIDTH)), const2((1, d)),
                  const2((GROUPS, CHUNK, CHUNK)), const2((CHUNK, S_WIDTH))],
        out_specs=wide,
        out_shape=jax.ShapeDtypeStruct((bsz, t, d), F32),
        scratch_shapes=[pltpu.VMEM((tm, 2 * D_MODEL), BF16)],
        compiler_params=_cparams("parallel", "parallel"),
        name="outproj",
    )(hf, hb, o, za, u, vs, zb, ym, zc, x, gt, w_out, ghn, gsgu, gpost, wsp, bsp)


_W_IN_SPLITS = np.cumsum([0, QK_W, QK_W, M_WIDTH, 4 * HEADS, M_WIDTH, M_WIDTH, S_WIDTH, S_WIDTH, S_WIDTH, F_WIDTH,
                          F_WIDTH])
_N_NAT = sum(w for _, w in _FULL_PIECES)


_WT_BLK = 256
_GATE_ROWS = 4 * HEADS


def _wprep_plan():
    c = _W_IN_SPLITS
    starts, q_blocks, gate_block = [], None, None
    for i in (1, 0, 2, 3, 4, 5, 6, 7, 8, 9, 10):
        if i == 3:
            gate_block = len(starts)
            starts.append(0)
            continue
        first = len(starts)
        starts += list(range(int(c[i]), int(c[i + 1]), _WT_BLK))
        if i == 0:
            q_blocks = (first, len(starts))
    return starts, q_blocks, gate_block


def _wprep_kernel(starts_ref, w_ref, g_ref, o_ref, *, q_blocks, gate_block):
    del starts_ref
    s = pl.program_id(1)

    @pl.when(s != gate_block)
    def _():
        scale = jnp.where((s >= q_blocks[0]) & (s < q_blocks[1]), DQK ** -0.5, 1.0)
        o_ref[0] = (w_ref[0] * scale).astype(BF16)

    @pl.when(s == gate_block)
    def _():
        g = g_ref[0]
        zero = jnp.zeros((LANES - UNITS, g.shape[1]), F32)
        o_ref[0] = jnp.concatenate([g[0:8], g[16:24], zero,
                                    g[8:16], g[24:32], zero],
                                   axis=0).astype(BF16)


def _wprep(w_in):
    depth, d, p_in = w_in.shape
    w_t = jnp.swapaxes(w_in, 1, 2)
    starts, q_blocks, gate_block = _wprep_plan()
    assert _W_IN_SPLITS[3] % _GATE_ROWS == 0 and len(starts) * _WT_BLK == QK_W + _N_NAT
    grid_spec = pltpu.PrefetchScalarGridSpec(
        num_scalar_prefetch=1,
        grid=(depth, len(starts)),
        in_specs=[pl.BlockSpec((pl.Element(1), pl.Element(_WT_BLK), pl.Element(d)),
                               lambda l, s, st: (l, st[s] * 8, 0)),
                  pl.BlockSpec((1, _GATE_ROWS, d), lambda l, s, st: (l, int(_W_IN_SPLITS[3]) // _GATE_ROWS, 0))],
        out_specs=pl.BlockSpec((1, _WT_BLK, d), lambda l, s, st: (l, s, 0)),
    )
    return pl.pallas_call(
        functools.partial(_wprep_kernel, q_blocks=q_blocks, gate_block=gate_block),
        grid_spec=grid_spec,
        out_shape=jax.ShapeDtypeStruct((depth, QK_W + _N_NAT, d), BF16),
        compiler_params=_cparams("parallel", "arbitrary"),
        name="wprep",
    )(jnp.asarray(starts, jnp.int32) // 8, w_t, w_t)


def _gate_bias(b_gate_l):
    bpad = jnp.zeros((LANES - UNITS,), b_gate_l.dtype)
    b_i = jnp.concatenate([b_gate_l[0:8], b_gate_l[16:24], bpad]).reshape(1, LANES)
    b_f = jnp.concatenate([b_gate_l[8:16], b_gate_l[24:32], bpad]).reshape(1, LANES)
    return b_i, b_f


def kernel(x, c, ctx, c_ctx, w_mod, b_mod, g_pre, g_post, w_in, b_gate, g_hnorm, g_sgu, w_sp, b_sp, w_fno, b_fno, w_out):
    bsz, t_lat, d = x.shape
    t_ctx = ctx.shape[1]
    depth = w_mod.shape[0]
    assert d == D_MODEL and t_lat % (2 * CHUNK) == 0 and t_ctx % (2 * CHUNK) == 0 and bsz + 1 <= 8

    cc = jnp.concatenate([c, c_ctx[None, :], jnp.zeros((8 - bsz - 1, d), c.dtype)], axis=0)
    mod = _modulation(cc, w_mod, b_mod)
    w_t_all = _wprep(w_in)

    consts_lat, consts_ctx = _dft_consts(t_lat), _dft_consts(t_ctx)
    cst = jnp.asarray(consts_lat["cs"]).astype(BF16)

    def fourier(xr, xi, consts, bfno):
        if "dense" in consts:
            return _fourier_dense(xr, xi, jnp.asarray(consts["dense"]).astype(BF16), bfno)
        return _fourier_fused(xr, xi, jnp.asarray(consts["f1"]).astype(BF16),
                              jnp.asarray(consts["mtab"]).astype(BF16), bfno, n1=CHUNK)

    cn_zero = jnp.zeros((bsz, UNITS, DQK, 2 * DV), F32)
    m_zero = jnp.zeros((bsz, UNITS, LANES), F32)
    xc = ctx
    for l in range(depth):
        sh_l, sc_l, gt_l = (mod[l, :bsz, i * d:(i + 1) * d].reshape(bsz, 1, d) for i in range(3))
        sh_c, sc_c, gt_c = (jnp.broadcast_to(mod[l, bsz, i * d:(i + 1) * d].reshape(1, 1, d), (bsz, 1, d))
                            for i in range(3))
        b_i, b_f = _gate_bias(b_gate[l])
        gpre = g_pre[l].reshape(1, d)
        wf = w_fno[l].astype(BF16)
        wsp = w_sp[l].astype(BF16)
        bsp = jnp.broadcast_to(b_sp[l].T[:, :, None], (CHUNK, GROUPS, GC)).reshape(CHUNK, S_WIDTH)
        wo = w_out[l].astype(BF16)
        tail = (wo, g_hnorm[l].reshape(1, M_WIDTH), g_sgu[l].reshape(1, S_WIDTH), g_post[l].reshape(1, d), wsp, bsp)
        front = functools.partial(_inproj, g_pre=gpre, w_t=w_t_all, cst=cst, wf=wf, b_i=b_i, b_f=b_f, layer=l)

        last = l == depth - 1
        pc = front(xc, sc_c, sh_c, full=not last, tm=256)
        hf_c, hb_c, cn_c, m_c = _mlstm(pc["q"], pc["kT"], pc["v"], pc["cola"], pc["colb"], pc["rowa"], cn_zero, m_zero)
        p = front(x, sc_l, sh_l, full=True, tm=512)
        hf, hb, _, _ = _mlstm(p["q"], p["kT"], p["v"], p["cola"], p["colb"], p["rowa"], cn_c, m_c)
        ym = fourier(p["xr"], p["xi"], consts_lat, b_fno[l])
        x = _outproj(hf, hb, p["o"], p["za"], p["u"], p["vs"], p["zb"], ym, p["zc"], x, gt_l, *tail, tm=512)
        if not last:
            ymc = fourier(pc["xr"], pc["xi"], consts_ctx, b_fno[l])
            xc = _outproj(hf_c, hb_c, pc["o"], pc["za"], pc["u"], pc["vs"], pc["zb"], ymc, pc["zc"], xc, gt_c,
                          *tail, tm=256)
    return x
```

```python
import functools

import numpy as np
import jax
import jax.numpy as jnp
from jax.experimental import pallas as pl
from jax.experimental.pallas import tpu as pltpu

EPS = 1e-6
LOG2E = 1.4426950408889634
D_MODEL = 1024
HEADS = 8
DV = 128
DQK = 64
QK_W = HEADS * DQK
M_WIDTH = HEADS * DV
CHUNK = 128
S_WIDTH = 512
F_WIDTH = 512
GROUPS = 4
GC = 128
UNITS = 2 * HEADS
LANES = 128

VMEM_LIMIT = 56 * 1024 * 1024

F32 = jnp.float32
BF16 = jnp.bfloat16


def _cparams(*sem):
    return pltpu.CompilerParams(dimension_semantics=sem, vmem_limit_bytes=VMEM_LIMIT)


def _sigmoid(x):
    return 1.0 / (1.0 + jnp.exp(-x))


def _silu(x):
    return x * _sigmoid(x)


def _mod_kernel(cc_ref, w_ref, b_ref, o_ref):
    s = _silu(cc_ref[...]).astype(BF16)
    w = w_ref[0].astype(BF16)
    o_ref[0] = jnp.dot(s, w, preferred_element_type=F32) + b_ref[0]


def _modulation(cc, w_mod, b_mod):
    depth, d, d3 = w_mod.shape
    return pl.pallas_call(
        _mod_kernel,
        grid=(depth,),
        in_specs=[pl.BlockSpec((8, d), lambda l: (0, 0)),
                  pl.BlockSpec((1, d, d3), lambda l: (l, 0, 0)),
                  pl.BlockSpec((1, 1, d3), lambda l: (l, 0, 0))],
        out_specs=pl.BlockSpec((1, 8, d3), lambda l: (l, 0, 0)),
        out_shape=jax.ShapeDtypeStruct((depth, 8, d3), F32),
        compiler_params=_cparams("arbitrary"),
        name="modulation",
    )(cc, w_mod, b_mod.reshape(depth, 1, d3))


def _log_sigmoid(x):
    return jnp.minimum(x, 0.0) - jnp.log1p(jnp.exp(-jnp.abs(x)))


def _gate_scans(gi, gf, cola_ref, colb_ref, rowa_ref):
    tm = gi.shape[0]
    row = jax.lax.broadcasted_iota(jnp.int32, (CHUNK, LANES), 0)
    fwd = jax.lax.broadcasted_iota(jnp.int32, (CHUNK, LANES), 1) < HEADS
    lf = _log_sigmoid(gf)

    def scan(x, op):
        pre, suf = x, x
        k = 1
        while k < CHUNK:
            sh = pltpu.roll(pre, k, axis=0)
            pre = jnp.where(row >= k, op(pre, sh), pre)
            sh = pltpu.roll(suf, CHUNK - k, axis=0)
            suf = jnp.where(row < CHUNK - k, op(suf, sh), suf)
            k *= 2
        return jnp.where(fwd, pre, suf)

    for c in range(tm // CHUNK):
        rows = slice(c * CHUNK, (c + 1) * CHUNK)
        b = scan(lf[rows], jnp.add)
        a = (gi[rows] - b) * LOG2E
        cola_ref[0, rows, :] = scan(a, jnp.maximum)
        colb_ref[0, rows, :] = b * LOG2E
        rowa_ref[0, :, rows] = a.T[0:UNITS, :]


def _pitch(n):
    p = n
    while (p // 8) % 2 == 0:
        p += 8
    return p


def _store_slabs(ref, g, z, n2):
    pin = _pitch(n2)
    for r in range(z.shape[0] // n2):
        ref[0, g, r * pin:r * pin + n2, :] = z[r * n2:(r + 1) * n2]
        ref[0, g, r * pin + n2:(r + 1) * pin, :] = jnp.zeros((pin - n2, GC), F32)


def _inproj_kernel(x_ref, sc_ref, sh_ref, g_ref, wt_ref, bi_ref, bf_ref, *out_refs, pieces, slab_n2):
    nt = (((1,), (1,)), ((), ()))
    x = x_ref[0]
    y = x * jax.lax.rsqrt(jnp.mean(x * x, axis=-1, keepdims=True) + EPS)
    h = (y * g_ref[...]) * (1.0 + sc_ref[0]) + sh_ref[0]
    hb = h.astype(BF16)
    oi = 0
    for name, off, width in pieces:
        if name == "kT":
            r = jax.lax.dot_general(wt_ref[0:QK_W, :], hb, nt, preferred_element_type=F32)
            out_refs[oi][0] = r.astype(BF16)
            oi += 1
            continue
        r = jax.lax.dot_general(hb, wt_ref[QK_W + off:QK_W + off + width, :], nt, preferred_element_type=F32)
        if name == "gates":
            _gate_scans(r[:, :LANES] + bi_ref[...], r[:, LANES:] + bf_ref[...], *out_refs[oi:oi + 3])
            oi += 3
        elif name == "f":
            xr_ref, xi_ref = out_refs[oi], out_refs[oi + 1]
            for g in range(GROUPS):
                zr, zi = r[:, 2 * g * GC:(2 * g + 1) * GC], r[:, (2 * g + 1) * GC:(2 * g + 2) * GC]
                if slab_n2 is None:
                    xr_ref[0, :, g * GC:(g + 1) * GC] = zr.astype(BF16)
                    xi_ref[0, :, g * GC:(g + 1) * GC] = zi.astype(BF16)
                else:
                    _store_slabs(xr_ref, g, zr, slab_n2)
                    _store_slabs(xi_ref, g, zi, slab_n2)
            oi += 2
        else:
            out_refs[oi][0] = r.astype(BF16)
            oi += 1


_FULL_PIECES = (("q", QK_W), ("v", M_WIDTH), ("gates", 2 * LANES), ("o", M_WIDTH), ("za", M_WIDTH),
                ("u", S_WIDTH), ("vs", S_WIDTH), ("zb", S_WIDTH), ("f", 2 * F_WIDTH), ("zc", F_WIDTH))
_STATE_PIECES = _FULL_PIECES[:3]
_WT_BLK = 512


def _wrows(width):
    return -(-width // _WT_BLK) * _WT_BLK


def _inproj(x, sc, sh, g_pre, w_t, b_i, b_f, *, layer, full, tm):
    bsz, t, d = x.shape
    names = _FULL_PIECES if full else _STATE_PIECES
    slab_n2 = t // CHUNK if t > 2 * CHUNK else None
    row_spec = lambda width: pl.BlockSpec((1, tm, width), lambda b, i: (b, i, 0))
    pieces, off = [], 0
    out_names, out_shapes, out_specs = [], [], []
    for name, width in names:
        pieces.append((name, off, width))
        off += _wrows(width)
        if name == "gates":
            out_names += ["cola", "colb", "rowa"]
            out_shapes += [jax.ShapeDtypeStruct((bsz, t, LANES), F32)] * 2 + [jax.ShapeDtypeStruct((bsz, UNITS, t), F32)]
            out_specs += [row_spec(LANES), row_spec(LANES), pl.BlockSpec((1, UNITS, tm), lambda b, i: (b, 0, i))]
        elif name == "f":
            out_names += ["xr", "xi"]
            if slab_n2 is None:
                out_shapes += [jax.ShapeDtypeStruct((bsz, t, F_WIDTH), BF16)] * 2
                out_specs += [row_spec(F_WIDTH)] * 2
            else:
                assert tm % slab_n2 == 0
                pin = _pitch(slab_n2)
                out_shapes += [jax.ShapeDtypeStruct((bsz, GROUPS, CHUNK * pin, GC), F32)] * 2
                out_specs += [pl.BlockSpec((1, GROUPS, tm // slab_n2 * pin, GC), lambda b, i: (b, 0, i, 0))] * 2
        else:
            out_names.append(name)
            out_shapes.append(jax.ShapeDtypeStruct((bsz, t, width), BF16))
            out_specs.append(row_spec(width))
    n_nat = off
    pieces.append(("kT", 0, QK_W))
    out_names.append("kT")
    out_shapes.append(jax.ShapeDtypeStruct((bsz, QK_W, t), BF16))
    out_specs.append(pl.BlockSpec((1, QK_W, tm), lambda b, i: (b, 0, i)))
    const = lambda shape, **kw: pl.BlockSpec(shape, lambda b, i: (0,) * len(shape), **kw)
    outs = pl.pallas_call(
        functools.partial(_inproj_kernel, pieces=tuple(pieces), slab_n2=slab_n2),
        grid=(bsz, t // tm),
        in_specs=[pl.BlockSpec((1, tm, d), lambda b, i: (b, i, 0)),
                  pl.BlockSpec((1, 1, d), lambda b, i: (b, 0, 0)),
                  pl.BlockSpec((1, 1, d), lambda b, i: (b, 0, 0)),
                  const((1, d)),
                  pl.BlockSpec((None, QK_W + n_nat, d), lambda b, i: (layer, 0, 0), pipeline_mode=pl.Buffered(1)),
                  const((1, LANES)), const((1, LANES))],
        out_specs=out_specs,
        out_shape=out_shapes,
        compiler_params=_cparams("parallel", "parallel"),
        name="inproj_full" if full else "inproj_state",
    )(x, sc, sh, g_pre, w_t, b_i, b_f)
    return dict(zip(out_names, outs))


def _mlstm_unit(h, d, sub, q_ref, kt_ref, v_ref, rowa_ref, m_diag, g_cols, c_cols, g_ends, hout_ref, cn_ref, mask):
    i = d * HEADS + h
    pair = h // 2
    rows = slice(sub * CHUNK, (sub + 1) * CHUNK)
    q_pair = q_ref[0, rows, pair * LANES:(pair + 1) * LANES]
    kt_h = kt_ref[0, h * DQK:(h + 1) * DQK, rows]
    zk = jnp.zeros((DQK, CHUNK), BF16)
    kt_ext = jnp.concatenate([kt_h, zk] if h % 2 == 0 else [zk, kt_h], axis=0)
    v_h = v_ref[0, rows, h * DV:(h + 1) * DV]
    vaug = jnp.concatenate([v_h, jnp.ones((CHUNK, DV), BF16)], axis=1)
    g = jnp.broadcast_to(g_cols[:, i:i + 1], (CHUNK, CHUNK))
    c = jnp.broadcast_to(c_cols[:, i:i + 1], (CHUNK, CHUNK))
    g_end = jnp.broadcast_to(g_ends[:, i:i + 1], (1, LANES))
    m_row = jnp.broadcast_to(m_diag[:, i:i + 1], (1, LANES))
    a_row = rowa_ref[0, h:h + 1, rows]
    cn = cn_ref[0, i]

    dmat = jnp.where(mask, jnp.exp2(a_row - g), 0.0)
    s = jnp.dot(q_pair, kt_ext, preferred_element_type=F32)
    p = (s * dmat).astype(BF16)
    qs = q_pair * jnp.exp2(m_row - g).astype(BF16)
    lhs = jnp.concatenate([p, qs], axis=1)
    zc = jnp.zeros((DQK, 2 * DV), BF16)
    cnb = cn.astype(BF16)
    cn_ext = jnp.concatenate([cnb, zc] if h % 2 == 0 else [zc, cnb], axis=0)
    rhs = jnp.concatenate([vaug, cn_ext], axis=0)
    out = jnp.dot(lhs, rhs, preferred_element_type=F32)
    num, den = out[:, :DV], out[:, DV:]
    hval = num / jnp.maximum(jnp.abs(den), jnp.exp2(c))
    hout_ref[0, rows, h * DV:(h + 1) * DV] = hval.astype(hout_ref.dtype)

    kts = kt_h * jnp.exp2(a_row - g_end).astype(BF16)
    upd = jnp.dot(kts, vaug, preferred_element_type=F32)
    decay = jnp.exp2(m_row - g_end)
    cn_ref[0, i] = jnp.concatenate([decay, decay], axis=1) * cn + upd


MLSTM_SUB = 8


def _mlstm_kernel(qf_ref, ktf_ref, vf_ref, caf_ref, cbf_ref, raf_ref,
                  qb_ref, ktb_ref, vb_ref, cab_ref, cbb_ref, rab_ref,
                  cn0_ref, m0_ref, hf_ref, hb_ref, cn_ref, m_ref, *, n_sub):
    @pl.when(pl.program_id(1) == 0)
    def _():
        cn_ref[...] = cn0_ref[...]
        m_ref[...] = m0_ref[...]

    t_idx = jax.lax.broadcasted_iota(jnp.int32, (CHUNK, CHUNK), 0)
    s_idx = jax.lax.broadcasted_iota(jnp.int32, (CHUNK, CHUNK), 1)
    unit_row = jax.lax.broadcasted_iota(jnp.int32, (UNITS, LANES), 0)
    unit_lane = jax.lax.broadcasted_iota(jnp.int32, (UNITS, LANES), 1)
    m_diag = jnp.sum(jnp.where(unit_row == unit_lane, m_ref[0], 0.0), axis=0, keepdims=True)
    fwd_lane = jax.lax.broadcasted_iota(jnp.int32, (1, LANES), 1) < HEADS

    def columns(ca_ref, cb_ref, sub, end):
        rows = slice(sub * CHUNK, (sub + 1) * CHUNK)
        g_cols = jnp.maximum(m_diag, ca_ref[0, rows, :])
        g_ends = g_cols[end:end + 1, :]
        m_new = cb_ref[0, sub * CHUNK + end:sub * CHUNK + end + 1, :] + g_ends
        return (g_cols, -(cb_ref[0, rows, :] + g_cols), g_ends), m_new

    for step in range(n_sub):
        sub_f, sub_b = step, n_sub - 1 - step
        cols_f, m_new_f = columns(caf_ref, cbf_ref, sub_f, CHUNK - 1)
        cols_b, m_new_b = columns(cab_ref, cbb_ref, sub_b, 0)
        for h in range(HEADS):
            _mlstm_unit(h, 0, sub_f, qf_ref, ktf_ref, vf_ref, raf_ref, m_diag, *cols_f, hf_ref, cn_ref, s_idx <= t_idx)
            _mlstm_unit(h, 1, sub_b, qb_ref, ktb_ref, vb_ref, rab_ref, m_diag, *cols_b, hb_ref, cn_ref, s_idx >= t_idx)
        m_diag = jnp.where(fwd_lane, m_new_f, m_new_b)
    m_ref[0] = jnp.where(unit_row == unit_lane, jnp.broadcast_to(m_diag, (UNITS, LANES)), 0.0)


def _mlstm(q, kt, v, cola, colb, rowa, cn0, m0):
    bsz, t, _ = q.shape
    n_sub = min(MLSTM_SUB, t // CHUNK)
    blk = n_sub * CHUNK
    nc = t // blk

    def specs(rev):
        cj = (lambda j: nc - 1 - j) if rev else (lambda j: j)
        d = 1 if rev else 0
        return [pl.BlockSpec((1, blk, QK_W), lambda b, j: (b, cj(j), 0)),
                pl.BlockSpec((1, QK_W, blk), lambda b, j: (b, 0, cj(j))),
                pl.BlockSpec((1, blk, M_WIDTH), lambda b, j: (b, cj(j), 0)),
                pl.BlockSpec((1, blk, LANES), lambda b, j: (b, cj(j), 0)),
                pl.BlockSpec((1, blk, LANES), lambda b, j: (b, cj(j), 0)),
                pl.BlockSpec((1, HEADS, blk), lambda b, j: (b, d, cj(j)))]

    cn_spec = pl.BlockSpec((1, UNITS, DQK, 2 * DV), lambda b, j: (b, 0, 0, 0))
    m_spec = pl.BlockSpec((1, UNITS, LANES), lambda b, j: (b, 0, 0))
    args = (q, kt, v, cola, colb, rowa)
    return pl.pallas_call(
        functools.partial(_mlstm_kernel, n_sub=n_sub),
        grid=(bsz, nc),
        in_specs=specs(False) + specs(True) + [cn_spec, m_spec],
        out_specs=[pl.BlockSpec((1, blk, M_WIDTH), lambda b, j: (b, j, 0)),
                   pl.BlockSpec((1, blk, M_WIDTH), lambda b, j: (b, nc - 1 - j, 0)),
                   cn_spec, m_spec],
        out_shape=[jax.ShapeDtypeStruct((bsz, t, M_WIDTH), BF16),
                   jax.ShapeDtypeStruct((bsz, t, M_WIDTH), BF16),
                   jax.ShapeDtypeStruct((bsz, UNITS, DQK, 2 * DV), F32),
                   jax.ShapeDtypeStruct((bsz, UNITS, LANES), F32)],
        compiler_params=_cparams("parallel", "arbitrary"),
        name="mlstm",
    )(*args, *args, cn0, m0)


OUT_PITCH = _pitch(CHUNK)


def _fourier_dense_kernel(xr_ref, xi_ref, cst_ref, bf_ref, o_ref):
    t = xr_ref.shape[1]
    xx = jnp.concatenate([xr_ref[0], xi_ref[0]], axis=0)
    yr = jnp.dot(cst_ref[...], xx, preferred_element_type=F32)
    for g in range(GROUPS):
        for c in range(t // CHUNK):
            o_ref[0, g, c * OUT_PITCH:c * OUT_PITCH + CHUNK, :] = (
                yr[c * CHUNK:(c + 1) * CHUNK, g * GC:(g + 1) * GC] + bf_ref[g])
            o_ref[0, g, c * OUT_PITCH + CHUNK:(c + 1) * OUT_PITCH, :] = jnp.zeros((OUT_PITCH - CHUNK, GC), F32)


def _fourier_dense(xr, xi, cst, bfno):
    bsz, t, w = xr.shape
    blk = pl.BlockSpec((1, t, w), lambda b: (b, 0, 0))
    rows = t // CHUNK * OUT_PITCH
    return pl.pallas_call(
        _fourier_dense_kernel,
        grid=(bsz,),
        in_specs=[blk, blk,
                  pl.BlockSpec((t, 2 * t), lambda b: (0, 0)),
                  pl.BlockSpec((GROUPS, 1, GC), lambda b: (0, 0, 0))],
        out_specs=pl.BlockSpec((1, GROUPS, rows, GC), lambda b: (b, 0, 0, 0)),
        out_shape=jax.ShapeDtypeStruct((bsz, GROUPS, rows, GC), F32),
        compiler_params=_cparams("parallel"),
        name="fourier_dense",
    )(xr, xi, cst, bfno.reshape(GROUPS, 1, GC))


def _fourier_fused_kernel(xr_ref, xi_ref, f1_ref, m_ref, bf_ref, o_ref, gr_scr, gi_scr, *, n1, n2):
    pin = _pitch(n2)

    def stage1(j, carry):
        t2 = 2 * j

        def ld(ref, s):
            return ref[0, 0, pl.ds(s, n1, stride=pin), :].astype(BF16)

        top = jnp.concatenate([ld(xr_ref, t2), ld(xr_ref, t2 + 1)], axis=1)
        bot = jnp.concatenate([ld(xi_ref, t2), ld(xi_ref, t2 + 1)], axis=1)
        g = jnp.dot(f1_ref[...], jnp.concatenate([top, bot], axis=0), preferred_element_type=F32)
        gr_scr[pl.ds(t2, n1, stride=pin), :] = g[:n1, :GC]
        gr_scr[pl.ds(t2 + 1, n1, stride=pin), :] = g[:n1, GC:]
        gi_scr[pl.ds(t2, n1, stride=pin), :] = g[n1:, :GC]
        gi_scr[pl.ds(t2 + 1, n1, stride=pin), :] = g[n1:, GC:]
        return carry

    jax.lax.fori_loop(0, n2 // 2, stage1, 0, unroll=min(4, n2 // 2))

    def stage2(k1, carry):
        base = pl.multiple_of(k1 * pin, 8)
        gg = jnp.concatenate([gr_scr[pl.ds(base, n2), :], gi_scr[pl.ds(base, n2), :]], axis=0).astype(BF16)
        yr = jnp.dot(m_ref[k1], gg, preferred_element_type=F32) + bf_ref[0]
        o_ref[0, 0, pl.ds(k1, n2, stride=OUT_PITCH), :] = yr
        return carry

    jax.lax.fori_loop(0, n1, stage2, 0, unroll=8)
    for k2 in range(n2):
        o_ref[0, 0, k2 * OUT_PITCH + n1:(k2 + 1) * OUT_PITCH, :] = jnp.zeros((OUT_PITCH - n1, GC), F32)


def _fourier_fused(xr, xi, f1, mtab, bfno, *, n1):
    bsz, _, rows_in, _ = xr.shape
    n2 = mtab.shape[1]
    pin = _pitch(n2)
    assert rows_in == n1 * pin and n1 == CHUNK
    in_blk = pl.BlockSpec((1, 1, rows_in, GC), lambda b, g: (b, g, 0, 0))
    return pl.pallas_call(
        functools.partial(_fourier_fused_kernel, n1=n1, n2=n2),
        grid=(bsz, GROUPS),
        in_specs=[in_blk, in_blk,
                  pl.BlockSpec((2 * n1, 2 * n1), lambda b, g: (0, 0)),
                  pl.BlockSpec((n1, n2, 2 * n2), lambda b, g: (0, 0, 0)),
                  pl.BlockSpec((1, 1, GC), lambda b, g: (g, 0, 0))],
        out_specs=pl.BlockSpec((1, 1, n2 * OUT_PITCH, GC), lambda b, g: (b, g, 0, 0)),
        out_shape=jax.ShapeDtypeStruct((bsz, GROUPS, n2 * OUT_PITCH, GC), F32),
        scratch_shapes=[pltpu.VMEM((n1 * pin, GC), F32), pltpu.VMEM((n1 * pin, GC), F32)],
        compiler_params=_cparams("parallel", "parallel"),
        name="fourier_fused",
    )(xr, xi, f1, mtab, bfno.reshape(GROUPS, 1, GC))


def _dft_consts(t):
    ang = 2.0 * np.pi / GC * np.outer(np.arange(GC), np.arange(GC))
    cs = np.concatenate([np.cos(ang), -np.sin(ang)], axis=0) / np.sqrt(GC)
    out = {"cs": cs.astype(np.float32)}
    if t <= 2 * CHUNK:
        ang = 2.0 * np.pi / t * np.mod(np.outer(np.arange(t), np.arange(t)), t)
        out["dense"] = (np.concatenate([np.cos(ang), np.sin(ang)], axis=1) / np.sqrt(t)).astype(np.float32)
    else:
        n1 = CHUNK
        n2 = t // n1
        ang = 2.0 * np.pi / n1 * np.mod(np.outer(np.arange(n1), np.arange(n1)), n1)
        c, s = np.cos(ang), np.sin(ang)
        out["f1"] = (np.block([[c, s], [-s, c]]) / np.sqrt(n1)).astype(np.float32)
        k = np.arange(n1)[:, None, None] + n1 * np.arange(n2)[None, :, None]
        ang = 2.0 * np.pi / t * np.mod(k * np.arange(n2)[None, None, :], t)
        out["mtab"] = (np.concatenate([np.cos(ang), np.sin(ang)], axis=2) / np.sqrt(n2)).astype(np.float32)
    return out


def _outproj_kernel(hf_ref, hb_ref, o_ref, za_ref, u_ref, vs_ref, zb_ref, ym_ref, zc_ref, x_ref, gt_ref,
                    wout_ref, ghn_ref, gsgu_ref, gpost_ref, wsp_ref, bsp_ref, xo_ref, y_scr):
    tm = x_ref.shape[1]
    kc = S_WIDTH

    def project(k):
        return jnp.dot(y_scr[:, k * kc:(k + 1) * kc], wout_ref[k * kc:(k + 1) * kc, :], preferred_element_type=F32)

    def silu(z):
        t = z * 0.5
        return t + t * jnp.tanh(t)

    out = None
    for h in range(HEADS):
        sl = slice(h * DV, (h + 1) * DV)
        hh = (hf_ref[0, :, sl] + hb_ref[0, :, sl]).astype(F32)
        hn = hh * jax.lax.rsqrt(jnp.mean(hh * hh, axis=-1, keepdims=True) + EPS) * ghn_ref[:, sl]
        o, za = o_ref[0, :, sl], za_ref[0, :, sl]
        y_scr[:, sl] = hn.astype(BF16) * ((0.5 + 0.5 * jnp.tanh(o * 0.5)) * silu(za))
        if (h + 1) * DV % kc == 0:
            part = project((h + 1) * DV // kc - 1)
            out = part if out is None else out + part
    vs = vs_ref[0].astype(F32)
    vn = (vs * jax.lax.rsqrt(jnp.mean(vs * vs, axis=-1, keepdims=True) + EPS) * gsgu_ref[...]).astype(BF16)
    for c in range(tm // CHUNK):
        rows = slice(c * CHUNK, (c + 1) * CHUNK)
        for g in range(GROUPS):
            cols = slice(g * GC, (g + 1) * GC)
            mixed = jnp.dot(wsp_ref[g], vn[rows, cols], preferred_element_type=F32) + bsp_ref[:, cols]
            zb = zb_ref[0, rows, cols]
            y_scr[rows, M_WIDTH + g * GC:M_WIDTH + (g + 1) * GC] = (
                u_ref[0, rows, cols] * mixed.astype(BF16) * silu(zb))
    out = out + project(M_WIDTH // kc)
    for c in range(tm // CHUNK):
        rows = slice(c * CHUNK, (c + 1) * CHUNK)
        for g in range(GROUPS):
            cols = slice(g * GC, (g + 1) * GC)
            zc = zc_ref[0, rows, cols]
            y_scr[rows, M_WIDTH + S_WIDTH + g * GC:M_WIDTH + S_WIDTH + (g + 1) * GC] = (
                ym_ref[0, g, c * OUT_PITCH:c * OUT_PITCH + CHUNK, :].astype(BF16) * silu(zc))
    out = out + project((M_WIDTH + S_WIDTH) // kc)
    on = out * jax.lax.rsqrt(jnp.mean(out * out, axis=-1, keepdims=True) + EPS) * gpost_ref[...]
    xo_ref[0] = x_ref[0] + gt_ref[0] * on


def _outproj(hf, hb, o, za, u, vs, zb, ym, zc, x, gt, w_out, ghn, gsgu, gpost, wsp, bsp, *, tm):
    bsz, t, d = x.shape
    wide = pl.BlockSpec((1, tm, M_WIDTH), lambda b, i: (b, i, 0))
    half = pl.BlockSpec((1, tm, S_WIDTH), lambda b, i: (b, i, 0))
    const2 = lambda shape: pl.BlockSpec(shape, lambda b, i: (0,) * len(shape))
    return pl.pallas_call(
        _outproj_kernel,
        grid=(bsz, t // tm),
        in_specs=[wide, wide, wide, wide, half, half, half,
                  pl.BlockSpec((1, GROUPS, tm // CHUNK * OUT_PITCH, GC), lambda b, i: (b, 0, i, 0)),
                  half, wide,
                  pl.BlockSpec((1, 1, d), lambda b, i: (b, 0, 0)),
                  const2((2 * D_MODEL, d)), const2((1, M_WIDTH)), const2((1, S_WIDTH)), const2((1, d)),
                  const2((GROUPS, CHUNK, CHUNK)), const2((CHUNK, S_WIDTH))],
        out_specs=wide,
        out_shape=jax.ShapeDtypeStruct((bsz, t, d), F32),
        scratch_shapes=[pltpu.VMEM((tm, 2 * D_MODEL), BF16)],
        compiler_params=_cparams("parallel", "parallel"),
        name="outproj",
    )(hf, hb, o, za, u, vs, zb, ym, zc, x, gt, w_out, ghn, gsgu, gpost, wsp, bsp)


_W_IN_SPLITS = np.cumsum([0, QK_W, QK_W, M_WIDTH, 4 * HEADS, M_WIDTH, M_WIDTH, S_WIDTH, S_WIDTH, S_WIDTH, F_WIDTH,
                          F_WIDTH])
_N_NAT = sum(_wrows(w) for _, w in _FULL_PIECES)
_GATE_ROWS = 4 * HEADS


def _wprep_plan():
    c = _W_IN_SPLITS
    starts, q_blocks, gate_block, f_block = [], None, None, None
    for i in (1, 0, 2, 3, 4, 5, 6, 7, 8, 9, 10):
        if i == 3:
            gate_block = len(starts)
            starts.append(0)
            continue
        first = len(starts)
        if i == 9:
            f_block = first
            starts += list(range(int(c[i]), int(c[i + 1]), _WT_BLK // 2))
            continue
        starts += list(range(int(c[i]), int(c[i + 1]), _WT_BLK))
        if i == 0:
            q_blocks = (first, len(starts))
    return starts, q_blocks, gate_block, f_block


F_STEP_GROUPS = _WT_BLK // (2 * GC)


def _wprep_kernel(starts_ref, w_ref, g_ref, cst_ref, wf_ref, o_ref, *, q_blocks, gate_block, f_block):
    del starts_ref
    s = pl.program_id(1)
    n_f = GROUPS // F_STEP_GROUPS

    @pl.when((s != gate_block) & ((s < f_block) | (s >= f_block + n_f)))
    def _():
        scale = jnp.where((s >= q_blocks[0]) & (s < q_blocks[1]), DQK ** -0.5, 1.0)
        o_ref[0] = (w_ref[0] * scale).astype(BF16)

    for k in range(n_f):
        @pl.when(s == f_block + k)
        def _(k=k):
            outs = []
            for gg in range(F_STEP_GROUPS):
                a = w_ref[0, gg * GC:(gg + 1) * GC, :].astype(BF16)
                wc = jnp.dot(cst_ref[...], wf_ref[0, k * F_STEP_GROUPS + gg].astype(BF16),
                             preferred_element_type=F32)
                for half in range(2):
                    cw_t = wc[half * GC:(half + 1) * GC].T.astype(BF16)
                    outs.append(jnp.dot(cw_t, a, preferred_element_type=F32))
            o_ref[0] = jnp.concatenate(outs, axis=0).astype(BF16)

    @pl.when(s == gate_block)
    def _():
        g = g_ref[0]
        zero = jnp.zeros((LANES - UNITS, g.shape[1]), F32)
        tail = jnp.zeros((_WT_BLK - 2 * LANES, g.shape[1]), F32)
        o_ref[0] = jnp.concatenate([g[0:8], g[16:24], zero,
                                    g[8:16], g[24:32], zero,
                                    tail], axis=0).astype(BF16)


def _wprep(w_in, cst, w_fno):
    depth, d, p_in = w_in.shape
    w_t = jnp.swapaxes(w_in, 1, 2)
    starts, q_blocks, gate_block, f_block = _wprep_plan()
    assert _W_IN_SPLITS[3] % _GATE_ROWS == 0 and len(starts) * _WT_BLK == QK_W + _N_NAT
    grid_spec = pltpu.PrefetchScalarGridSpec(
        num_scalar_prefetch=1,
        grid=(depth, len(starts)),
        in_specs=[pl.BlockSpec((pl.Element(1), pl.Element(_WT_BLK), pl.Element(d)),
                               lambda l, s, st: (l, st[s] * 8, 0)),
                  pl.BlockSpec((1, _GATE_ROWS, d), lambda l, s, st: (l, int(_W_IN_SPLITS[3]) // _GATE_ROWS, 0)),
                  pl.BlockSpec((2 * GC, GC), lambda l, s, st: (0, 0)),
                  pl.BlockSpec((1, GROUPS, GC, GC), lambda l, s, st: (l, 0, 0, 0))],
        out_specs=pl.BlockSpec((1, _WT_BLK, d), lambda l, s, st: (l, s, 0)),
    )
    return pl.pallas_call(
        functools.partial(_wprep_kernel, q_blocks=q_blocks, gate_block=gate_block, f_block=f_block),
        grid_spec=grid_spec,
        out_shape=jax.ShapeDtypeStruct((depth, QK_W + _N_NAT, d), BF16),
        compiler_params=_cparams("parallel", "arbitrary"),
        name="wprep",
    )(jnp.asarray(starts, jnp.int32) // 8, w_t, w_t, cst, w_fno)


def _gate_bias(b_gate_l):
    bpad = jnp.zeros((LANES - UNITS,), b_gate_l.dtype)
    b_i = jnp.concatenate([b_gate_l[0:8], b_gate_l[16:24], bpad]).reshape(1, LANES)
    b_f = jnp.concatenate([b_gate_l[8:16], b_gate_l[24:32], bpad]).reshape(1, LANES)
    return b_i, b_f


def kernel(x, c, ctx, c_ctx, w_mod, b_mod, g_pre, g_post, w_in, b_gate, g_hnorm, g_sgu, w_sp, b_sp, w_fno, b_fno, w_out):
    bsz, t_lat, d = x.shape
    t_ctx = ctx.shape[1]
    depth = w_mod.shape[0]
    assert d == D_MODEL and t_lat % (2 * CHUNK) == 0 and t_ctx % (2 * CHUNK) == 0 and bsz + 1 <= 8

    cc = jnp.concatenate([c, c_ctx[None, :], jnp.zeros((8 - bsz - 1, d), c.dtype)], axis=0)
    mod = _modulation(cc, w_mod, b_mod)
    consts_lat, consts_ctx = _dft_consts(t_lat), _dft_consts(t_ctx)
    w_t_all = _wprep(w_in, jnp.asarray(consts_lat["cs"]).astype(BF16), w_fno)

    def fourier(xr, xi, consts, bfno):
        if "dense" in consts:
            return _fourier_dense(xr, xi, jnp.asarray(consts["dense"]).astype(BF16), bfno)
        return _fourier_fused(xr, xi, jnp.asarray(consts["f1"]).astype(BF16),
                              jnp.asarray(consts["mtab"]).astype(BF16), bfno, n1=CHUNK)

    cn_zero = jnp.zeros((bsz, UNITS, DQK, 2 * DV), F32)
    m_zero = jnp.zeros((bsz, UNITS, LANES), F32)
    xc = ctx
    for l in range(depth):
        sh_l, sc_l, gt_l = (mod[l, :bsz, i * d:(i + 1) * d].reshape(bsz, 1, d) for i in range(3))
        sh_c, sc_c, gt_c = (jnp.broadcast_to(mod[l, bsz, i * d:(i + 1) * d].reshape(1, 1, d), (bsz, 1, d))
                            for i in range(3))
        b_i, b_f = _gate_bias(b_gate[l])
        gpre = g_pre[l].reshape(1, d)
        wsp = w_sp[l].astype(BF16)
        bsp = jnp.broadcast_to(b_sp[l].T[:, :, None], (CHUNK, GROUPS, GC)).reshape(CHUNK, S_WIDTH)
        wo = w_out[l].astype(BF16)
        tail = (wo, g_hnorm[l].reshape(1, M_WIDTH), g_sgu[l].reshape(1, S_WIDTH), g_post[l].reshape(1, d), wsp, bsp)
        front = functools.partial(_inproj, g_pre=gpre, w_t=w_t_all, b_i=b_i, b_f=b_f, layer=l)

        last = l == depth - 1
        pc = front(xc, sc_c, sh_c, full=not last, tm=256)
        hf_c, hb_c, cn_c, m_c = _mlstm(pc["q"], pc["kT"], pc["v"], pc["cola"], pc["colb"], pc["rowa"], cn_zero, m_zero)
        p = front(x, sc_l, sh_l, full=True, tm=512)
        hf, hb, _, _ = _mlstm(p["q"], p["kT"], p["v"], p["cola"], p["colb"], p["rowa"], cn_c, m_c)
        ym = fourier(p["xr"], p["xi"], consts_lat, b_fno[l])
        x = _outproj(hf, hb, p["o"], p["za"], p["u"], p["vs"], p["zb"], ym, p["zc"], x, gt_l, *tail, tm=512)
        if not last:
            ymc = fourier(pc["xr"], pc["xi"], consts_ctx, b_fno[l])
            xc = _outproj(hf_c, hb_c, pc["o"], pc["za"], pc["u"], pc["vs"], pc["zb"], ymc, pc["zc"], xc, gt_c,
                          *tail, tm=256)
    return x
```

```python
import functools

import numpy as np
import jax
import jax.numpy as jnp
from jax.experimental import pallas as pl
from jax.experimental.pallas import tpu as pltpu

EPS = 1e-6
LOG2E = 1.4426950408889634
D_MODEL = 1024
HEADS = 8
DV = 128
DQK = 64
QK_W = HEADS * DQK
M_WIDTH = HEADS * DV
CHUNK = 128
S_WIDTH = 512
F_WIDTH = 512
GROUPS = 4
GC = 128
UNITS = 2 * HEADS
LANES = 128

VMEM_LIMIT = 56 * 1024 * 1024

F32 = jnp.float32
BF16 = jnp.bfloat16


def _cparams(*sem):
    return pltpu.CompilerParams(dimension_semantics=sem, vmem_limit_bytes=VMEM_LIMIT)


def _sigmoid(x):
    return 1.0 / (1.0 + jnp.exp(-x))


def _silu(x):
    return x * _sigmoid(x)


def _mod_kernel(cc_ref, w_ref, b_ref, o_ref):
    s = _silu(cc_ref[...]).astype(BF16)
    w = w_ref[0].astype(BF16)
    o_ref[0] = jnp.dot(s, w, preferred_element_type=F32) + b_ref[0]


def _modulation(cc, w_mod, b_mod):
    depth, d, d3 = w_mod.shape
    return pl.pallas_call(
        _mod_kernel,
        grid=(depth,),
        in_specs=[pl.BlockSpec((8, d), lambda l: (0, 0)),
                  pl.BlockSpec((1, d, d3), lambda l: (l, 0, 0)),
                  pl.BlockSpec((1, 1, d3), lambda l: (l, 0, 0))],
        out_specs=pl.BlockSpec((1, 8, d3), lambda l: (l, 0, 0)),
        out_shape=jax.ShapeDtypeStruct((depth, 8, d3), F32),
        compiler_params=_cparams("arbitrary"),
        name="modulation",
    )(cc, w_mod, b_mod.reshape(depth, 1, d3))


def _log_sigmoid(x):
    return jnp.minimum(x, 0.0) - jnp.log1p(jnp.exp(-jnp.abs(x)))


def _gate_scans(gi, gf, cola_ref, colb_ref, rowa_ref):
    tm = gi.shape[0]
    row = jax.lax.broadcasted_iota(jnp.int32, (CHUNK, LANES), 0)
    fwd = jax.lax.broadcasted_iota(jnp.int32, (CHUNK, LANES), 1) < HEADS
    lf = _log_sigmoid(gf)

    def scan(x, op):
        pre, suf = x, x
        k = 1
        while k < CHUNK:
            sh = pltpu.roll(pre, k, axis=0)
            pre = jnp.where(row >= k, op(pre, sh), pre)
            sh = pltpu.roll(suf, CHUNK - k, axis=0)
            suf = jnp.where(row < CHUNK - k, op(suf, sh), suf)
            k *= 2
        return jnp.where(fwd, pre, suf)

    for c in range(tm // CHUNK):
        rows = slice(c * CHUNK, (c + 1) * CHUNK)
        b = scan(lf[rows], jnp.add)
        a = (gi[rows] - b) * LOG2E
        cola_ref[0, rows, :] = scan(a, jnp.maximum)
        colb_ref[0, rows, :] = b * LOG2E
        rowa_ref[0, :, rows] = a.T[0:UNITS, :]


def _pitch(n):
    p = n
    while (p // 8) % 2 == 0:
        p += 8
    return p


def _store_slabs(ref, g, z, n2):
    pin = _pitch(n2)
    for r in range(z.shape[0] // n2):
        ref[0, g, r * pin:r * pin + n2, :] = z[r * n2:(r + 1) * n2]
        ref[0, g, r * pin + n2:(r + 1) * pin, :] = jnp.zeros((pin - n2, GC), F32)


def _inproj_kernel(x_ref, sc_ref, sh_ref, g_ref, wt_ref, bi_ref, bf_ref, *out_refs, pieces, slab_n2):
    nt = (((1,), (1,)), ((), ()))
    x = x_ref[0]
    y = x * jax.lax.rsqrt(jnp.mean(x * x, axis=-1, keepdims=True) + EPS)
    h = (y * g_ref[...]) * (1.0 + sc_ref[0]) + sh_ref[0]
    hb = h.astype(BF16)

    def silu(z):
        t = z * 0.5
        return t + t * jnp.tanh(t)

    held = {}
    oi = 0
    for name, off, width in pieces:
        if name == "kT":
            r = jax.lax.dot_general(wt_ref[0:QK_W, :], hb, nt, preferred_element_type=F32)
            out_refs[oi][0] = r.astype(BF16)
            oi += 1
            continue
        r = jax.lax.dot_general(hb, wt_ref[QK_W + off:QK_W + off + width, :], nt, preferred_element_type=F32)
        if name == "gates":
            _gate_scans(r[:, :LANES] + bi_ref[...], r[:, LANES:] + bf_ref[...], *out_refs[oi:oi + 3])
            oi += 3
        elif name == "f":
            xr_ref, xi_ref = out_refs[oi], out_refs[oi + 1]
            for g in range(GROUPS):
                zr, zi = r[:, 2 * g * GC:(2 * g + 1) * GC], r[:, (2 * g + 1) * GC:(2 * g + 2) * GC]
                if slab_n2 is None:
                    xr_ref[0, :, g * GC:(g + 1) * GC] = zr.astype(BF16)
                    xi_ref[0, :, g * GC:(g + 1) * GC] = zi.astype(BF16)
                else:
                    _store_slabs(xr_ref, g, zr, slab_n2)
                    _store_slabs(xi_ref, g, zi, slab_n2)
            oi += 2
        elif name in _HELD:
            held[name] = r
        else:
            if name == "za":
                r = (0.5 + 0.5 * jnp.tanh(held["o"] * 0.5)) * silu(r)
            elif name == "zb":
                r = held["u"] * silu(r)
            elif name == "zc":
                r = silu(r)
            out_refs[oi][0] = r.astype(BF16)
            oi += 1


_FULL_PIECES = (("q", QK_W), ("v", M_WIDTH), ("gates", 2 * LANES), ("o", M_WIDTH), ("za", M_WIDTH),
                ("u", S_WIDTH), ("vs", S_WIDTH), ("zb", S_WIDTH), ("f", 2 * F_WIDTH), ("zc", F_WIDTH))
_STATE_PIECES = _FULL_PIECES[:3]
_HELD = ("o", "u")
_GATED_NAME = {"za": "ga", "zb": "gb", "zc": "szc"}
_WT_BLK = 512


def _wrows(width):
    return -(-width // _WT_BLK) * _WT_BLK


def _inproj(x, sc, sh, g_pre, w_t, b_i, b_f, *, layer, full, tm):
    bsz, t, d = x.shape
    names = _FULL_PIECES if full else _STATE_PIECES
    slab_n2 = t // CHUNK if t > 2 * CHUNK else None
    row_spec = lambda width: pl.BlockSpec((1, tm, width), lambda b, i: (b, i, 0))
    pieces, off = [], 0
    out_names, out_shapes, out_specs = [], [], []
    for name, width in names:
        pieces.append((name, off, width))
        off += _wrows(width)
        if name == "gates":
            out_names += ["cola", "colb", "rowa"]
            out_shapes += [jax.ShapeDtypeStruct((bsz, t, LANES), F32)] * 2 + [jax.ShapeDtypeStruct((bsz, UNITS, t), F32)]
            out_specs += [row_spec(LANES), row_spec(LANES), pl.BlockSpec((1, UNITS, tm), lambda b, i: (b, 0, i))]
        elif name == "f":
            out_names += ["xr", "xi"]
            if slab_n2 is None:
                out_shapes += [jax.ShapeDtypeStruct((bsz, t, F_WIDTH), BF16)] * 2
                out_specs += [row_spec(F_WIDTH)] * 2
            else:
                assert tm % slab_n2 == 0
                pin = _pitch(slab_n2)
                out_shapes += [jax.ShapeDtypeStruct((bsz, GROUPS, CHUNK * pin, GC), F32)] * 2
                out_specs += [pl.BlockSpec((1, GROUPS, tm // slab_n2 * pin, GC), lambda b, i: (b, 0, i, 0))] * 2
        elif name not in _HELD:
            out_names.append(_GATED_NAME.get(name, name))
            out_shapes.append(jax.ShapeDtypeStruct((bsz, t, width), BF16))
            out_specs.append(row_spec(width))
    n_nat = off
    pieces.append(("kT", 0, QK_W))
    out_names.append("kT")
    out_shapes.append(jax.ShapeDtypeStruct((bsz, QK_W, t), BF16))
    out_specs.append(pl.BlockSpec((1, QK_W, tm), lambda b, i: (b, 0, i)))
    const = lambda shape, **kw: pl.BlockSpec(shape, lambda b, i: (0,) * len(shape), **kw)
    outs = pl.pallas_call(
        functools.partial(_inproj_kernel, pieces=tuple(pieces), slab_n2=slab_n2),
        grid=(bsz, t // tm),
        in_specs=[pl.BlockSpec((1, tm, d), lambda b, i: (b, i, 0)),
                  pl.BlockSpec((1, 1, d), lambda b, i: (b, 0, 0)),
                  pl.BlockSpec((1, 1, d), lambda b, i: (b, 0, 0)),
                  const((1, d)),
                  pl.BlockSpec((None, QK_W + n_nat, d), lambda b, i: (layer, 0, 0), pipeline_mode=pl.Buffered(1)),
                  const((1, LANES)), const((1, LANES))],
        out_specs=out_specs,
        out_shape=out_shapes,
        compiler_params=_cparams("parallel", "parallel"),
        name="inproj_full" if full else "inproj_state",
    )(x, sc, sh, g_pre, w_t, b_i, b_f)
    return dict(zip(out_names, outs))


def _mlstm_unit(h, d, sub, q_ref, kt_ref, v_ref, rowa_ref, m_diag, g_cols, c_cols, g_ends, hout_ref, cn_ref, mask):
    i = d * HEADS + h
    pair = h // 2
    rows = slice(sub * CHUNK, (sub + 1) * CHUNK)
    q_pair = q_ref[0, rows, pair * LANES:(pair + 1) * LANES]
    kt_h = kt_ref[0, h * DQK:(h + 1) * DQK, rows]
    zk = jnp.zeros((DQK, CHUNK), BF16)
    kt_ext = jnp.concatenate([kt_h, zk] if h % 2 == 0 else [zk, kt_h], axis=0)
    v_h = v_ref[0, rows, h * DV:(h + 1) * DV]
    vaug = jnp.concatenate([v_h, jnp.ones((CHUNK, DV), BF16)], axis=1)
    g = jnp.broadcast_to(g_cols[:, i:i + 1], (CHUNK, CHUNK))
    c = jnp.broadcast_to(c_cols[:, i:i + 1], (CHUNK, CHUNK))
    g_end = jnp.broadcast_to(g_ends[:, i:i + 1], (1, LANES))
    m_row = jnp.broadcast_to(m_diag[:, i:i + 1], (1, LANES))
    a_row = rowa_ref[0, h:h + 1, rows]
    cn = cn_ref[0, i]

    dmat = jnp.where(mask, jnp.exp2(a_row - g), 0.0)
    s = jnp.dot(q_pair, kt_ext, preferred_element_type=F32)
    p = (s * dmat).astype(BF16)
    qs = q_pair * jnp.exp2(m_row - g).astype(BF16)
    lhs = jnp.concatenate([p, qs], axis=1)
    zc = jnp.zeros((DQK, 2 * DV), BF16)
    cnb = cn.astype(BF16)
    cn_ext = jnp.concatenate([cnb, zc] if h % 2 == 0 else [zc, cnb], axis=0)
    rhs = jnp.concatenate([vaug, cn_ext], axis=0)
    out = jnp.dot(lhs, rhs, preferred_element_type=F32)
    num, den = out[:, :DV], out[:, DV:]
    hval = num / jnp.maximum(jnp.abs(den), jnp.exp2(c))
    hout_ref[0, rows, h * DV:(h + 1) * DV] = hval.astype(hout_ref.dtype)

    kts = kt_h * jnp.exp2(a_row - g_end).astype(BF16)
    upd = jnp.dot(kts, vaug, preferred_element_type=F32)
    decay = jnp.exp2(m_row - g_end)
    cn_ref[0, i] = jnp.concatenate([decay, decay], axis=1) * cn + upd


MLSTM_SUB = 8


def _mlstm_kernel(qf_ref, ktf_ref, vf_ref, caf_ref, cbf_ref, raf_ref,
                  qb_ref, ktb_ref, vb_ref, cab_ref, cbb_ref, rab_ref,
                  cn0_ref, m0_ref, hf_ref, hb_ref, cn_ref, m_ref, *, n_sub):
    @pl.when(pl.program_id(1) == 0)
    def _():
        cn_ref[...] = cn0_ref[...]
        m_ref[...] = m0_ref[...]

    t_idx = jax.lax.broadcasted_iota(jnp.int32, (CHUNK, CHUNK), 0)
    s_idx = jax.lax.broadcasted_iota(jnp.int32, (CHUNK, CHUNK), 1)
    unit_row = jax.lax.broadcasted_iota(jnp.int32, (UNITS, LANES), 0)
    unit_lane = jax.lax.broadcasted_iota(jnp.int32, (UNITS, LANES), 1)
    m_diag = jnp.sum(jnp.where(unit_row == unit_lane, m_ref[0], 0.0), axis=0, keepdims=True)
    fwd_lane = jax.lax.broadcasted_iota(jnp.int32, (1, LANES), 1) < HEADS

    def columns(ca_ref, cb_ref, sub, end):
        rows = slice(sub * CHUNK, (sub + 1) * CHUNK)
        g_cols = jnp.maximum(m_diag, ca_ref[0, rows, :])
        g_ends = g_cols[end:end + 1, :]
        m_new = cb_ref[0, sub * CHUNK + end:sub * CHUNK + end + 1, :] + g_ends
        return (g_cols, -(cb_ref[0, rows, :] + g_cols), g_ends), m_new

    for step in range(n_sub):
        sub_f, sub_b = step, n_sub - 1 - step
        cols_f, m_new_f = columns(caf_ref, cbf_ref, sub_f, CHUNK - 1)
        cols_b, m_new_b = columns(cab_ref, cbb_ref, sub_b, 0)
        for h in range(HEADS):
            _mlstm_unit(h, 0, sub_f, qf_ref, ktf_ref, vf_ref, raf_ref, m_diag, *cols_f, hf_ref, cn_ref, s_idx <= t_idx)
            _mlstm_unit(h, 1, sub_b, qb_ref, ktb_ref, vb_ref, rab_ref, m_diag, *cols_b, hb_ref, cn_ref, s_idx >= t_idx)
        m_diag = jnp.where(fwd_lane, m_new_f, m_new_b)
    m_ref[0] = jnp.where(unit_row == unit_lane, jnp.broadcast_to(m_diag, (UNITS, LANES)), 0.0)


def _mlstm(q, kt, v, cola, colb, rowa, cn0, m0):
    bsz, t, _ = q.shape
    n_sub = min(MLSTM_SUB, t // CHUNK)
    blk = n_sub * CHUNK
    nc = t // blk

    def specs(rev):
        cj = (lambda j: nc - 1 - j) if rev else (lambda j: j)
        d = 1 if rev else 0
        return [pl.BlockSpec((1, blk, QK_W), lambda b, j: (b, cj(j), 0)),
                pl.BlockSpec((1, QK_W, blk), lambda b, j: (b, 0, cj(j))),
                pl.BlockSpec((1, blk, M_WIDTH), lambda b, j: (b, cj(j), 0)),
                pl.BlockSpec((1, blk, LANES), lambda b, j: (b, cj(j), 0)),
                pl.BlockSpec((1, blk, LANES), lambda b, j: (b, cj(j), 0)),
                pl.BlockSpec((1, HEADS, blk), lambda b, j: (b, d, cj(j)))]

    cn_spec = pl.BlockSpec((1, UNITS, DQK, 2 * DV), lambda b, j: (b, 0, 0, 0))
    m_spec = pl.BlockSpec((1, UNITS, LANES), lambda b, j: (b, 0, 0))
    args = (q, kt, v, cola, colb, rowa)
    return pl.pallas_call(
        functools.partial(_mlstm_kernel, n_sub=n_sub),
        grid=(bsz, nc),
        in_specs=specs(False) + specs(True) + [cn_spec, m_spec],
        out_specs=[pl.BlockSpec((1, blk, M_WIDTH), lambda b, j: (b, j, 0)),
                   pl.BlockSpec((1, blk, M_WIDTH), lambda b, j: (b, nc - 1 - j, 0)),
                   cn_spec, m_spec],
        out_shape=[jax.ShapeDtypeStruct((bsz, t, M_WIDTH), BF16),
                   jax.ShapeDtypeStruct((bsz, t, M_WIDTH), BF16),
                   jax.ShapeDtypeStruct((bsz, UNITS, DQK, 2 * DV), F32),
                   jax.ShapeDtypeStruct((bsz, UNITS, LANES), F32)],
        compiler_params=_cparams("parallel", "arbitrary"),
        name="mlstm",
    )(*args, *args, cn0, m0)


OUT_PITCH = _pitch(CHUNK)


def _fourier_dense_kernel(xr_ref, xi_ref, cst_ref, bf_ref, o_ref):
    t = xr_ref.shape[1]
    xx = jnp.concatenate([xr_ref[0], xi_ref[0]], axis=0)
    yr = jnp.dot(cst_ref[...], xx, preferred_element_type=F32)
    for g in range(GROUPS):
        for c in range(t // CHUNK):
            o_ref[0, g, c * OUT_PITCH:c * OUT_PITCH + CHUNK, :] = (
                yr[c * CHUNK:(c + 1) * CHUNK, g * GC:(g + 1) * GC] + bf_ref[g])
            o_ref[0, g, c * OUT_PITCH + CHUNK:(c + 1) * OUT_PITCH, :] = jnp.zeros((OUT_PITCH - CHUNK, GC), F32)


def _fourier_dense(xr, xi, cst, bfno):
    bsz, t, w = xr.shape
    blk = pl.BlockSpec((1, t, w), lambda b: (b, 0, 0))
    rows = t // CHUNK * OUT_PITCH
    return pl.pallas_call(
        _fourier_dense_kernel,
        grid=(bsz,),
        in_specs=[blk, blk,
                  pl.BlockSpec((t, 2 * t), lambda b: (0, 0)),
                  pl.BlockSpec((GROUPS, 1, GC), lambda b: (0, 0, 0))],
        out_specs=pl.BlockSpec((1, GROUPS, rows, GC), lambda b: (b, 0, 0, 0)),
        out_shape=jax.ShapeDtypeStruct((bsz, GROUPS, rows, GC), F32),
        compiler_params=_cparams("parallel"),
        name="fourier_dense",
    )(xr, xi, cst, bfno.reshape(GROUPS, 1, GC))


def _fourier_fused_kernel(xr_ref, xi_ref, f1_ref, m_ref, bf_ref, o_ref, gr_scr, gi_scr, *, n1, n2):
    pin = _pitch(n2)

    def stage1(j, carry):
        t2 = 2 * j

        def ld(ref, s):
            return ref[0, 0, pl.ds(s, n1, stride=pin), :].astype(BF16)

        top = jnp.concatenate([ld(xr_ref, t2), ld(xr_ref, t2 + 1)], axis=1)
        bot = jnp.concatenate([ld(xi_ref, t2), ld(xi_ref, t2 + 1)], axis=1)
        g = jnp.dot(f1_ref[...], jnp.concatenate([top, bot], axis=0), preferred_element_type=F32)
        gr_scr[pl.ds(t2, n1, stride=pin), :] = g[:n1, :GC]
        gr_scr[pl.ds(t2 + 1, n1, stride=pin), :] = g[:n1, GC:]
        gi_scr[pl.ds(t2, n1, stride=pin), :] = g[n1:, :GC]
        gi_scr[pl.ds(t2 + 1, n1, stride=pin), :] = g[n1:, GC:]
        return carry

    jax.lax.fori_loop(0, n2 // 2, stage1, 0, unroll=min(4, n2 // 2))

    def stage2(k1, carry):
        base = pl.multiple_of(k1 * pin, 8)
        gg = jnp.concatenate([gr_scr[pl.ds(base, n2), :], gi_scr[pl.ds(base, n2), :]], axis=0).astype(BF16)
        yr = jnp.dot(m_ref[k1], gg, preferred_element_type=F32) + bf_ref[0]
        o_ref[0, 0, pl.ds(k1, n2, stride=OUT_PITCH), :] = yr
        return carry

    jax.lax.fori_loop(0, n1, stage2, 0, unroll=8)
    for k2 in range(n2):
        o_ref[0, 0, k2 * OUT_PITCH + n1:(k2 + 1) * OUT_PITCH, :] = jnp.zeros((OUT_PITCH - n1, GC), F32)


def _fourier_fused(xr, xi, f1, mtab, bfno, *, n1):
    bsz, _, rows_in, _ = xr.shape
    n2 = mtab.shape[1]
    pin = _pitch(n2)
    assert rows_in == n1 * pin and n1 == CHUNK
    in_blk = pl.BlockSpec((1, 1, rows_in, GC), lambda b, g: (b, g, 0, 0))
    return pl.pallas_call(
        functools.partial(_fourier_fused_kernel, n1=n1, n2=n2),
        grid=(bsz, GROUPS),
        in_specs=[in_blk, in_blk,
                  pl.BlockSpec((2 * n1, 2 * n1), lambda b, g: (0, 0)),
                  pl.BlockSpec((n1, n2, 2 * n2), lambda b, g: (0, 0, 0)),
                  pl.BlockSpec((1, 1, GC), lambda b, g: (g, 0, 0))],
        out_specs=pl.BlockSpec((1, 1, n2 * OUT_PITCH, GC), lambda b, g: (b, g, 0, 0)),
        out_shape=jax.ShapeDtypeStruct((bsz, GROUPS, n2 * OUT_PITCH, GC), F32),
        scratch_shapes=[pltpu.VMEM((n1 * pin, GC), F32), pltpu.VMEM((n1 * pin, GC), F32)],
        compiler_params=_cparams("parallel", "parallel"),
        name="fourier_fused",
    )(xr, xi, f1, mtab, bfno.reshape(GROUPS, 1, GC))


def _dft_consts(t):
    ang = 2.0 * np.pi / GC * np.outer(np.arange(GC), np.arange(GC))
    cs = np.concatenate([np.cos(ang), -np.sin(ang)], axis=0) / np.sqrt(GC)
    out = {"cs": cs.astype(np.float32)}
    if t <= 2 * CHUNK:
        ang = 2.0 * np.pi / t * np.mod(np.outer(np.arange(t), np.arange(t)), t)
        out["dense"] = (np.concatenate([np.cos(ang), np.sin(ang)], axis=1) / np.sqrt(t)).astype(np.float32)
    else:
        n1 = CHUNK
        n2 = t // n1
        ang = 2.0 * np.pi / n1 * np.mod(np.outer(np.arange(n1), np.arange(n1)), n1)
        c, s = np.cos(ang), np.sin(ang)
        out["f1"] = (np.block([[c, s], [-s, c]]) / np.sqrt(n1)).astype(np.float32)
        k = np.arange(n1)[:, None, None] + n1 * np.arange(n2)[None, :, None]
        ang = 2.0 * np.pi / t * np.mod(k * np.arange(n2)[None, None, :], t)
        out["mtab"] = (np.concatenate([np.cos(ang), np.sin(ang)], axis=2) / np.sqrt(n2)).astype(np.float32)
    return out


def _outproj_kernel(hf_ref, hb_ref, ga_ref, vs_ref, gb_ref, ym_ref, szc_ref, x_ref, gt_ref,
                    wout_ref, ghn_ref, gsgu_ref, gpost_ref, wsp_ref, bsp_ref, xo_ref, y_scr):
    tm = x_ref.shape[1]
    kc = S_WIDTH

    def project(k):
        return jnp.dot(y_scr[:, k * kc:(k + 1) * kc], wout_ref[k * kc:(k + 1) * kc, :], preferred_element_type=F32)

    out = None
    for h in range(HEADS):
        sl = slice(h * DV, (h + 1) * DV)
        hh = (hf_ref[0, :, sl] + hb_ref[0, :, sl]).astype(F32)
        hn = hh * jax.lax.rsqrt(jnp.mean(hh * hh, axis=-1, keepdims=True) + EPS) * ghn_ref[:, sl]
        y_scr[:, sl] = hn.astype(BF16) * ga_ref[0, :, sl]
        if (h + 1) * DV % kc == 0:
            part = project((h + 1) * DV // kc - 1)
            out = part if out is None else out + part
    vs = vs_ref[0].astype(F32)
    vn = (vs * jax.lax.rsqrt(jnp.mean(vs * vs, axis=-1, keepdims=True) + EPS) * gsgu_ref[...]).astype(BF16)
    for c in range(tm // CHUNK):
        rows = slice(c * CHUNK, (c + 1) * CHUNK)
        for g in range(GROUPS):
            cols = slice(g * GC, (g + 1) * GC)
            mixed = jnp.dot(wsp_ref[g], vn[rows, cols], preferred_element_type=F32) + bsp_ref[:, cols]
            y_scr[rows, M_WIDTH + g * GC:M_WIDTH + (g + 1) * GC] = gb_ref[0, rows, cols] * mixed.astype(BF16)
    out = out + project(M_WIDTH // kc)
    for c in range(tm // CHUNK):
        rows = slice(c * CHUNK, (c + 1) * CHUNK)
        for g in range(GROUPS):
            cols = slice(g * GC, (g + 1) * GC)
            y_scr[rows, M_WIDTH + S_WIDTH + g * GC:M_WIDTH + S_WIDTH + (g + 1) * GC] = (
                ym_ref[0, g, c * OUT_PITCH:c * OUT_PITCH + CHUNK, :].astype(BF16) * szc_ref[0, rows, cols])
    out = out + project((M_WIDTH + S_WIDTH) // kc)
    on = out * jax.lax.rsqrt(jnp.mean(out * out, axis=-1, keepdims=True) + EPS) * gpost_ref[...]
    xo_ref[0] = x_ref[0] + gt_ref[0] * on


def _outproj(hf, hb, ga, vs, gb, ym, szc, x, gt, w_out, ghn, gsgu, gpost, wsp, bsp, *, tm):
    bsz, t, d = x.shape
    wide = pl.BlockSpec((1, tm, M_WIDTH), lambda b, i: (b, i, 0))
    half = pl.BlockSpec((1, tm, S_WIDTH), lambda b, i: (b, i, 0))
    const2 = lambda shape: pl.BlockSpec(shape, lambda b, i: (0,) * len(shape))
    return pl.pallas_call(
        _outproj_kernel,
        grid=(bsz, t // tm),
        in_specs=[wide, wide, wide, half, half,
                  pl.BlockSpec((1, GROUPS, tm // CHUNK * OUT_PITCH, GC), lambda b, i: (b, 0, i, 0)),
                  half, wide,
                  pl.BlockSpec((1, 1, d), lambda b, i: (b, 0, 0)),
                  const2((2 * D_MODEL, d)), const2((1, M_WIDTH)), const2((1, S_WIDTH)), const2((1, d)),
                  const2((GROUPS, CHUNK, CHUNK)), const2((CHUNK, S_WIDTH))],
        out_specs=wide,
        out_shape=jax.ShapeDtypeStruct((bsz, t, d), F32),
        scratch_shapes=[pltpu.VMEM((tm, 2 * D_MODEL), BF16)],
        compiler_params=_cparams("parallel", "parallel"),
        name="outproj",
    )(hf, hb, ga, vs, gb, ym, szc, x, gt, w_out, ghn, gsgu, gpost, wsp, bsp)


_W_IN_SPLITS = np.cumsum([0, QK_W, QK_W, M_WIDTH, 4 * HEADS, M_WIDTH, M_WIDTH, S_WIDTH, S_WIDTH, S_WIDTH, F_WIDTH,
                          F_WIDTH])
_N_NAT = sum(_wrows(w) for _, w in _FULL_PIECES)
_GATE_ROWS = 4 * HEADS


def _wprep_plan():
    c = _W_IN_SPLITS
    starts, q_blocks, gate_block, f_block = [], None, None, None
    for i in (1, 0, 2, 3, 4, 5, 6, 7, 8, 9, 10):
        if i == 3:
            gate_block = len(starts)
            starts.append(0)
            continue
        first = len(starts)
        if i == 9:
            f_block = first
            starts += list(range(int(c[i]), int(c[i + 1]), _WT_BLK // 2))
            continue
        starts += list(range(int(c[i]), int(c[i + 1]), _WT_BLK))
        if i == 0:
            q_blocks = (first, len(starts))
    return starts, q_blocks, gate_block, f_block


F_STEP_GROUPS = _WT_BLK // (2 * GC)


def _wprep_kernel(starts_ref, w_ref, g_ref, cst_ref, wf_ref, o_ref, *, q_blocks, gate_block, f_block):
    del starts_ref
    s = pl.program_id(1)
    n_f = GROUPS // F_STEP_GROUPS

    @pl.when((s != gate_block) & ((s < f_block) | (s >= f_block + n_f)))
    def _():
        scale = jnp.where((s >= q_blocks[0]) & (s < q_blocks[1]), DQK ** -0.5, 1.0)
        o_ref[0] = (w_ref[0] * scale).astype(BF16)

    for k in range(n_f):
        @pl.when(s == f_block + k)
        def _(k=k):
            outs = []
            for gg in range(F_STEP_GROUPS):
                a = w_ref[0, gg * GC:(gg + 1) * GC, :].astype(BF16)
                wc = jnp.dot(cst_ref[...], wf_ref[0, k * F_STEP_GROUPS + gg].astype(BF16),
                             preferred_element_type=F32)
                for half in range(2):
                    cw_t = wc[half * GC:(half + 1) * GC].T.astype(BF16)
                    outs.append(jnp.dot(cw_t, a, preferred_element_type=F32))
            o_ref[0] = jnp.concatenate(outs, axis=0).astype(BF16)

    @pl.when(s == gate_block)
    def _():
        g = g_ref[0]
        zero = jnp.zeros((LANES - UNITS, g.shape[1]), F32)
        tail = jnp.zeros((_WT_BLK - 2 * LANES, g.shape[1]), F32)
        o_ref[0] = jnp.concatenate([g[0:8], g[16:24], zero,
                                    g[8:16], g[24:32], zero,
                                    tail], axis=0).astype(BF16)


def _wprep(w_in, cst, w_fno):
    depth, d, p_in = w_in.shape
    w_t = jnp.swapaxes(w_in, 1, 2)
    starts, q_blocks, gate_block, f_block = _wprep_plan()
    assert _W_IN_SPLITS[3] % _GATE_ROWS == 0 and len(starts) * _WT_BLK == QK_W + _N_NAT
    grid_spec = pltpu.PrefetchScalarGridSpec(
        num_scalar_prefetch=1,
        grid=(depth, len(starts)),
        in_specs=[pl.BlockSpec((pl.Element(1), pl.Element(_WT_BLK), pl.Element(d)),
                               lambda l, s, st: (l, st[s] * 8, 0)),
                  pl.BlockSpec((1, _GATE_ROWS, d), lambda l, s, st: (l, int(_W_IN_SPLITS[3]) // _GATE_ROWS, 0)),
                  pl.BlockSpec((2 * GC, GC), lambda l, s, st: (0, 0)),
                  pl.BlockSpec((1, GROUPS, GC, GC), lambda l, s, st: (l, 0, 0, 0))],
        out_specs=pl.BlockSpec((1, _WT_BLK, d), lambda l, s, st: (l, s, 0)),
    )
    return pl.pallas_call(
        functools.partial(_wprep_kernel, q_blocks=q_blocks, gate_block=gate_block, f_block=f_block),
        grid_spec=grid_spec,
        out_shape=jax.ShapeDtypeStruct((depth, QK_W + _N_NAT, d), BF16),
        compiler_params=_cparams("parallel", "arbitrary"),
        name="wprep",
    )(jnp.asarray(starts, jnp.int32) // 8, w_t, w_t, cst, w_fno)


def _gate_bias(b_gate_l):
    bpad = jnp.zeros((LANES - UNITS,), b_gate_l.dtype)
    b_i = jnp.concatenate([b_gate_l[0:8], b_gate_l[16:24], bpad]).reshape(1, LANES)
    b_f = jnp.concatenate([b_gate_l[8:16], b_gate_l[24:32], bpad]).reshape(1, LANES)
    return b_i, b_f


def kernel(x, c, ctx, c_ctx, w_mod, b_mod, g_pre, g_post, w_in, b_gate, g_hnorm, g_sgu, w_sp, b_sp, w_fno, b_fno, w_out):
    bsz, t_lat, d = x.shape
    t_ctx = ctx.shape[1]
    depth = w_mod.shape[0]
    assert d == D_MODEL and t_lat % (2 * CHUNK) == 0 and t_ctx % (2 * CHUNK) == 0 and bsz + 1 <= 8

    cc = jnp.concatenate([c, c_ctx[None, :], jnp.zeros((8 - bsz - 1, d), c.dtype)], axis=0)
    mod = _modulation(cc, w_mod, b_mod)
    consts_lat, consts_ctx = _dft_consts(t_lat), _dft_consts(t_ctx)
    w_t_all = _wprep(w_in, jnp.asarray(consts_lat["cs"]).astype(BF16), w_fno)

    def fourier(xr, xi, consts, bfno):
        if "dense" in consts:
            return _fourier_dense(xr, xi, jnp.asarray(consts["dense"]).astype(BF16), bfno)
        return _fourier_fused(xr, xi, jnp.asarray(consts["f1"]).astype(BF16),
                              jnp.asarray(consts["mtab"]).astype(BF16), bfno, n1=CHUNK)

    cn_zero = jnp.zeros((bsz, UNITS, DQK, 2 * DV), F32)
    m_zero = jnp.zeros((bsz, UNITS, LANES), F32)
    xc = ctx
    for l in range(depth):
        sh_l, sc_l, gt_l = (mod[l, :bsz, i * d:(i + 1) * d].reshape(bsz, 1, d) for i in range(3))
        sh_c, sc_c, gt_c = (jnp.broadcast_to(mod[l, bsz, i * d:(i + 1) * d].reshape(1, 1, d), (bsz, 1, d))
                            for i in range(3))
        b_i, b_f = _gate_bias(b_gate[l])
        gpre = g_pre[l].reshape(1, d)
        wsp = w_sp[l].astype(BF16)
        bsp = jnp.broadcast_to(b_sp[l].T[:, :, None], (CHUNK, GROUPS, GC)).reshape(CHUNK, S_WIDTH)
        wo = w_out[l].astype(BF16)
        tail = (wo, g_hnorm[l].reshape(1, M_WIDTH), g_sgu[l].reshape(1, S_WIDTH), g_post[l].reshape(1, d), wsp, bsp)
        front = functools.partial(_inproj, g_pre=gpre, w_t=w_t_all, b_i=b_i, b_f=b_f, layer=l)

        last = l == depth - 1
        pc = front(xc, sc_c, sh_c, full=not last, tm=256)
        hf_c, hb_c, cn_c, m_c = _mlstm(pc["q"], pc["kT"], pc["v"], pc["cola"], pc["colb"], pc["rowa"], cn_zero, m_zero)
        p = front(x, sc_l, sh_l, full=True, tm=512)
        hf, hb, _, _ = _mlstm(p["q"], p["kT"], p["v"], p["cola"], p["colb"], p["rowa"], cn_c, m_c)
        ym = fourier(p["xr"], p["xi"], consts_lat, b_fno[l])
        x = _outproj(hf, hb, p["ga"], p["vs"], p["gb"], ym, p["szc"], x, gt_l, *tail, tm=512)
        if not last:
            ymc = fourier(pc["xr"], pc["xi"], consts_ctx, b_fno[l])
            xc = _outproj(hf_c, hb_c, pc["ga"], pc["vs"], pc["gb"], ymc, pc["szc"], xc, gt_c, *tail, tm=256)
    return x
```

```python
import functools

import numpy as np
import jax
import jax.numpy as jnp
from jax.experimental import pallas as pl
from jax.experimental.pallas import tpu as pltpu

EPS = 1e-6
LOG2E = 1.4426950408889634
D_MODEL = 1024
HEADS = 8
DV = 128
DQK = 64
QK_W = HEADS * DQK
M_WIDTH = HEADS * DV
CHUNK = 128
S_WIDTH = 512
F_WIDTH = 512
GROUPS = 4
GC = 128
UNITS = 2 * HEADS
LANES = 128

VMEM_LIMIT = 56 * 1024 * 1024

F32 = jnp.float32
BF16 = jnp.bfloat16


def _cparams(*sem):
    return pltpu.CompilerParams(dimension_semantics=sem, vmem_limit_bytes=VMEM_LIMIT)


def _sigmoid(x):
    return 1.0 / (1.0 + jnp.exp(-x))


def _silu(x):
    return x * _sigmoid(x)


def _mod_kernel(cc_ref, w_ref, b_ref, o_ref):
    s = _silu(cc_ref[...]).astype(BF16)
    w = w_ref[0].astype(BF16)
    o_ref[0] = jnp.dot(s, w, preferred_element_type=F32) + b_ref[0]


def _modulation(cc, w_mod, b_mod):
    depth, d, d3 = w_mod.shape
    return pl.pallas_call(
        _mod_kernel,
        grid=(depth,),
        in_specs=[pl.BlockSpec((8, d), lambda l: (0, 0)),
                  pl.BlockSpec((1, d, d3), lambda l: (l, 0, 0)),
                  pl.BlockSpec((1, 1, d3), lambda l: (l, 0, 0))],
        out_specs=pl.BlockSpec((1, 8, d3), lambda l: (l, 0, 0)),
        out_shape=jax.ShapeDtypeStruct((depth, 8, d3), F32),
        compiler_params=_cparams("arbitrary"),
        name="modulation",
    )(cc, w_mod, b_mod.reshape(depth, 1, d3))


def _log_sigmoid(x):
    return jnp.minimum(x, 0.0) - jnp.log1p(jnp.exp(-jnp.abs(x)))


def _gate_scans(gi, gf, cola_ref, colb_ref, rowa_ref):
    tm = gi.shape[0]
    n_c = tm // CHUNK
    assert n_c * UNITS <= LANES
    row = jax.lax.broadcasted_iota(jnp.int32, (CHUNK, LANES), 0)
    lane = jax.lax.broadcasted_iota(jnp.int32, (CHUNK, LANES), 1)
    fwd = lane % UNITS < HEADS
    lf = _log_sigmoid(gf)

    def pack(x):
        out = x[0:CHUNK]
        for c in range(1, n_c):
            out = jnp.where(lane // UNITS == c, pltpu.roll(x[c * CHUNK:(c + 1) * CHUNK], c * UNITS, axis=1), out)
        return out

    def scan(x, op):
        pre, suf = x, x
        k = 1
        while k < CHUNK:
            sh = pltpu.roll(pre, k, axis=0)
            pre = jnp.where(row >= k, op(pre, sh), pre)
            sh = pltpu.roll(suf, CHUNK - k, axis=0)
            suf = jnp.where(row < CHUNK - k, op(suf, sh), suf)
            k *= 2
        return jnp.where(fwd, pre, suf)

    b = scan(pack(lf), jnp.add)
    a = (pack(gi) - b) * LOG2E
    amax = scan(a, jnp.maximum)
    b = b * LOG2E
    a_t = a.T
    for c in range(n_c):
        rows = slice(c * CHUNK, (c + 1) * CHUNK)
        back = (LANES - c * UNITS) % LANES
        cola_ref[0, rows, :] = pltpu.roll(amax, back, axis=1) if c else amax
        colb_ref[0, rows, :] = pltpu.roll(b, back, axis=1) if c else b
        rowa_ref[0, :, rows] = a_t[c * UNITS:(c + 1) * UNITS, :]


def _pitch(n):
    p = n
    while (p // 8) % 2 == 0:
        p += 8
    return p


def _store_slabs(ref, g, z, n2):
    pin = _pitch(n2)
    for r in range(z.shape[0] // n2):
        ref[0, g, r * pin:r * pin + n2, :] = z[r * n2:(r + 1) * n2]
        ref[0, g, r * pin + n2:(r + 1) * pin, :] = jnp.zeros((pin - n2, GC), F32)


def _inproj_kernel(x_ref, sc_ref, sh_ref, g_ref, wt_ref, bi_ref, bf_ref, *out_refs, pieces, slab_n2):
    nt = (((1,), (1,)), ((), ()))
    x = x_ref[0]
    y = x * jax.lax.rsqrt(jnp.mean(x * x, axis=-1, keepdims=True) + EPS)
    h = (y * g_ref[...]) * (1.0 + sc_ref[0]) + sh_ref[0]
    hb = h.astype(BF16)

    def silu(z):
        t = z * 0.5
        return t + t * jnp.tanh(t)

    held = {}
    oi = 0
    for name, off, width in pieces:
        if name == "kT":
            r = jax.lax.dot_general(wt_ref[0:QK_W, :], hb, nt, preferred_element_type=F32)
            out_refs[oi][0] = r.astype(BF16)
            oi += 1
            continue
        r = jax.lax.dot_general(hb, wt_ref[QK_W + off:QK_W + off + width, :], nt, preferred_element_type=F32)
        if name == "gates":
            _gate_scans(r[:, :LANES] + bi_ref[...], r[:, LANES:] + bf_ref[...], *out_refs[oi:oi + 3])
            oi += 3
        elif name == "f":
            xr_ref, xi_ref = out_refs[oi], out_refs[oi + 1]
            for g in range(GROUPS):
                zr, zi = r[:, 2 * g * GC:(2 * g + 1) * GC], r[:, (2 * g + 1) * GC:(2 * g + 2) * GC]
                if slab_n2 is None:
                    xr_ref[0, :, g * GC:(g + 1) * GC] = zr.astype(BF16)
                    xi_ref[0, :, g * GC:(g + 1) * GC] = zi.astype(BF16)
                else:
                    _store_slabs(xr_ref, g, zr, slab_n2)
                    _store_slabs(xi_ref, g, zi, slab_n2)
            oi += 2
        elif name in _HELD:
            held[name] = r
        else:
            if name == "za":
                r = (0.5 + 0.5 * jnp.tanh(held["o"] * 0.5)) * silu(r)
            elif name == "zb":
                r = held["u"] * silu(r)
            elif name == "zc":
                r = silu(r)
            out_refs[oi][0] = r.astype(BF16)
            oi += 1


_FULL_PIECES = (("q", QK_W), ("v", M_WIDTH), ("gates", 2 * LANES), ("o", M_WIDTH), ("za", M_WIDTH),
                ("u", S_WIDTH), ("vs", S_WIDTH), ("zb", S_WIDTH), ("f", 2 * F_WIDTH), ("zc", F_WIDTH))
_STATE_PIECES = _FULL_PIECES[:3]
_HELD = ("o", "u")
_GATED_NAME = {"za": "ga", "zb": "gb", "zc": "szc"}
_WT_BLK = 512


def _wrows(width):
    return -(-width // _WT_BLK) * _WT_BLK


def _inproj(x, sc, sh, g_pre, w_t, b_i, b_f, *, layer, full, tm):
    bsz, t, d = x.shape
    names = _FULL_PIECES if full else _STATE_PIECES
    slab_n2 = t // CHUNK if t > 2 * CHUNK else None
    row_spec = lambda width: pl.BlockSpec((1, tm, width), lambda b, i: (b, i, 0))
    pieces, off = [], 0
    out_names, out_shapes, out_specs = [], [], []
    for name, width in names:
        pieces.append((name, off, width))
        off += _wrows(width)
        if name == "gates":
            out_names += ["cola", "colb", "rowa"]
            out_shapes += [jax.ShapeDtypeStruct((bsz, t, LANES), F32)] * 2 + [jax.ShapeDtypeStruct((bsz, UNITS, t), F32)]
            out_specs += [row_spec(LANES), row_spec(LANES), pl.BlockSpec((1, UNITS, tm), lambda b, i: (b, 0, i))]
        elif name == "f":
            out_names += ["xr", "xi"]
            if slab_n2 is None:
                out_shapes += [jax.ShapeDtypeStruct((bsz, t, F_WIDTH), BF16)] * 2
                out_specs += [row_spec(F_WIDTH)] * 2
            else:
                assert tm % slab_n2 == 0
                pin = _pitch(slab_n2)
                out_shapes += [jax.ShapeDtypeStruct((bsz, GROUPS, CHUNK * pin, GC), F32)] * 2
                out_specs += [pl.BlockSpec((1, GROUPS, tm // slab_n2 * pin, GC), lambda b, i: (b, 0, i, 0))] * 2
        elif name not in _HELD:
            out_names.append(_GATED_NAME.get(name, name))
            out_shapes.append(jax.ShapeDtypeStruct((bsz, t, width), BF16))
            out_specs.append(row_spec(width))
    n_nat = off
    pieces.append(("kT", 0, QK_W))
    out_names.append("kT")
    out_shapes.append(jax.ShapeDtypeStruct((bsz, QK_W, t), BF16))
    out_specs.append(pl.BlockSpec((1, QK_W, tm), lambda b, i: (b, 0, i)))
    const = lambda shape, **kw: pl.BlockSpec(shape, lambda b, i: (0,) * len(shape), **kw)
    outs = pl.pallas_call(
        functools.partial(_inproj_kernel, pieces=tuple(pieces), slab_n2=slab_n2),
        grid=(bsz, t // tm),
        in_specs=[pl.BlockSpec((1, tm, d), lambda b, i: (b, i, 0)),
                  pl.BlockSpec((1, 1, d), lambda b, i: (b, 0, 0)),
                  pl.BlockSpec((1, 1, d), lambda b, i: (b, 0, 0)),
                  const((1, d)),
                  pl.BlockSpec((None, QK_W + n_nat, d), lambda b, i: (layer, 0, 0), pipeline_mode=pl.Buffered(1)),
                  const((1, LANES)), const((1, LANES))],
        out_specs=out_specs,
        out_shape=out_shapes,
        compiler_params=_cparams("parallel", "parallel"),
        name="inproj_full" if full else "inproj_state",
    )(x, sc, sh, g_pre, w_t, b_i, b_f)
    return dict(zip(out_names, outs))


def _mlstm_unit(h, d, sub, q_ref, kt_ref, v_ref, rowa_ref, m_diag, g_cols, c_cols, g_ends, hout_ref, cn_ref, mask):
    i = d * HEADS + h
    pair = h // 2
    rows = slice(sub * CHUNK, (sub + 1) * CHUNK)
    q_pair = q_ref[0, rows, pair * LANES:(pair + 1) * LANES]
    kt_h = kt_ref[0, h * DQK:(h + 1) * DQK, rows]
    zk = jnp.zeros((DQK, CHUNK), BF16)
    kt_ext = jnp.concatenate([kt_h, zk] if h % 2 == 0 else [zk, kt_h], axis=0)
    v_h = v_ref[0, rows, h * DV:(h + 1) * DV]
    vaug = jnp.concatenate([v_h, jnp.ones((CHUNK, DV), BF16)], axis=1)
    g = jnp.broadcast_to(g_cols[:, i:i + 1], (CHUNK, CHUNK))
    c = jnp.broadcast_to(c_cols[:, i:i + 1], (CHUNK, CHUNK))
    g_end = jnp.broadcast_to(g_ends[:, i:i + 1], (1, LANES))
    m_row = jnp.broadcast_to(m_diag[:, i:i + 1], (1, LANES))
    a_row = rowa_ref[0, h:h + 1, rows]
    cn = cn_ref[0, i]

    dmat = jnp.where(mask, jnp.exp2(a_row - g), 0.0)
    s = jnp.dot(q_pair, kt_ext, preferred_element_type=F32)
    p = (s * dmat).astype(BF16)
    qs = q_pair * jnp.exp2(m_row - g).astype(BF16)
    lhs = jnp.concatenate([p, qs], axis=1)
    zc = jnp.zeros((DQK, 2 * DV), BF16)
    cnb = cn.astype(BF16)
    cn_ext = jnp.concatenate([cnb, zc] if h % 2 == 0 else [zc, cnb], axis=0)
    rhs = jnp.concatenate([vaug, cn_ext], axis=0)
    out = jnp.dot(lhs, rhs, preferred_element_type=F32)
    num, den = out[:, :DV], out[:, DV:]
    hval = num / jnp.maximum(jnp.abs(den), jnp.exp2(c))
    hout_ref[0, rows, h * DV:(h + 1) * DV] = hval.astype(hout_ref.dtype)

    kts = kt_h * jnp.exp2(a_row - g_end).astype(BF16)
    upd = jnp.dot(kts, vaug, preferred_element_type=F32)
    decay = jnp.exp2(m_row - g_end)
    cn_ref[0, i] = jnp.concatenate([decay, decay], axis=1) * cn + upd


MLSTM_SUB = 8


def _mlstm_kernel(qf_ref, ktf_ref, vf_ref, caf_ref, cbf_ref, raf_ref,
                  qb_ref, ktb_ref, vb_ref, cab_ref, cbb_ref, rab_ref,
                  cn0_ref, m0_ref, hf_ref, hb_ref, cn_ref, m_ref, *, n_sub):
    @pl.when(pl.program_id(1) == 0)
    def _():
        cn_ref[...] = cn0_ref[...]
        m_ref[...] = m0_ref[...]

    t_idx = jax.lax.broadcasted_iota(jnp.int32, (CHUNK, CHUNK), 0)
    s_idx = jax.lax.broadcasted_iota(jnp.int32, (CHUNK, CHUNK), 1)
    unit_row = jax.lax.broadcasted_iota(jnp.int32, (UNITS, LANES), 0)
    unit_lane = jax.lax.broadcasted_iota(jnp.int32, (UNITS, LANES), 1)
    m_diag = jnp.sum(jnp.where(unit_row == unit_lane, m_ref[0], 0.0), axis=0, keepdims=True)
    fwd_lane = jax.lax.broadcasted_iota(jnp.int32, (1, LANES), 1) < HEADS

    def columns(ca_ref, cb_ref, sub, end):
        rows = slice(sub * CHUNK, (sub + 1) * CHUNK)
        g_cols = jnp.maximum(m_diag, ca_ref[0, rows, :])
        g_ends = g_cols[end:end + 1, :]
        m_new = cb_ref[0, sub * CHUNK + end:sub * CHUNK + end + 1, :] + g_ends
        return (g_cols, -(cb_ref[0, rows, :] + g_cols), g_ends), m_new

    for step in range(n_sub):
        sub_f, sub_b = step, n_sub - 1 - step
        cols_f, m_new_f = columns(caf_ref, cbf_ref, sub_f, CHUNK - 1)
        cols_b, m_new_b = columns(cab_ref, cbb_ref, sub_b, 0)
        for h in range(HEADS):
            _mlstm_unit(h, 0, sub_f, qf_ref, ktf_ref, vf_ref, raf_ref, m_diag, *cols_f, hf_ref, cn_ref, s_idx <= t_idx)
            _mlstm_unit(h, 1, sub_b, qb_ref, ktb_ref, vb_ref, rab_ref, m_diag, *cols_b, hb_ref, cn_ref, s_idx >= t_idx)
        m_diag = jnp.where(fwd_lane, m_new_f, m_new_b)
    m_ref[0] = jnp.where(unit_row == unit_lane, jnp.broadcast_to(m_diag, (UNITS, LANES)), 0.0)


def _mlstm(q, kt, v, cola, colb, rowa, cn0, m0):
    bsz, t, _ = q.shape
    n_sub = min(MLSTM_SUB, t // CHUNK)
    blk = n_sub * CHUNK
    nc = t // blk

    def specs(rev):
        cj = (lambda j: nc - 1 - j) if rev else (lambda j: j)
        d = 1 if rev else 0
        return [pl.BlockSpec((1, blk, QK_W), lambda b, j: (b, cj(j), 0)),
                pl.BlockSpec((1, QK_W, blk), lambda b, j: (b, 0, cj(j))),
                pl.BlockSpec((1, blk, M_WIDTH), lambda b, j: (b, cj(j), 0)),
                pl.BlockSpec((1, blk, LANES), lambda b, j: (b, cj(j), 0)),
                pl.BlockSpec((1, blk, LANES), lambda b, j: (b, cj(j), 0)),
                pl.BlockSpec((1, HEADS, blk), lambda b, j: (b, d, cj(j)))]

    cn_spec = pl.BlockSpec((1, UNITS, DQK, 2 * DV), lambda b, j: (b, 0, 0, 0))
    m_spec = pl.BlockSpec((1, UNITS, LANES), lambda b, j: (b, 0, 0))
    args = (q, kt, v, cola, colb, rowa)
    return pl.pallas_call(
        functools.partial(_mlstm_kernel, n_sub=n_sub),
        grid=(bsz, nc),
        in_specs=specs(False) + specs(True) + [cn_spec, m_spec],
        out_specs=[pl.BlockSpec((1, blk, M_WIDTH), lambda b, j: (b, j, 0)),
                   pl.BlockSpec((1, blk, M_WIDTH), lambda b, j: (b, nc - 1 - j, 0)),
                   cn_spec, m_spec],
        out_shape=[jax.ShapeDtypeStruct((bsz, t, M_WIDTH), BF16),
                   jax.ShapeDtypeStruct((bsz, t, M_WIDTH), BF16),
                   jax.ShapeDtypeStruct((bsz, UNITS, DQK, 2 * DV), F32),
                   jax.ShapeDtypeStruct((bsz, UNITS, LANES), F32)],
        compiler_params=_cparams("parallel", "arbitrary"),
        name="mlstm",
    )(*args, *args, cn0, m0)


OUT_PITCH = _pitch(CHUNK)


def _fourier_dense_kernel(xr_ref, xi_ref, cst_ref, bf_ref, o_ref):
    t = xr_ref.shape[1]
    xx = jnp.concatenate([xr_ref[0], xi_ref[0]], axis=0)
    yr = jnp.dot(cst_ref[...], xx, preferred_element_type=F32)
    for g in range(GROUPS):
        for c in range(t // CHUNK):
            o_ref[0, g, c * OUT_PITCH:c * OUT_PITCH + CHUNK, :] = (
                yr[c * CHUNK:(c + 1) * CHUNK, g * GC:(g + 1) * GC] + bf_ref[g])
            o_ref[0, g, c * OUT_PITCH + CHUNK:(c + 1) * OUT_PITCH, :] = jnp.zeros((OUT_PITCH - CHUNK, GC), F32)


def _fourier_dense(xr, xi, cst, bfno):
    bsz, t, w = xr.shape
    blk = pl.BlockSpec((1, t, w), lambda b: (b, 0, 0))
    rows = t // CHUNK * OUT_PITCH
    return pl.pallas_call(
        _fourier_dense_kernel,
        grid=(bsz,),
        in_specs=[blk, blk,
                  pl.BlockSpec((t, 2 * t), lambda b: (0, 0)),
                  pl.BlockSpec((GROUPS, 1, GC), lambda b: (0, 0, 0))],
        out_specs=pl.BlockSpec((1, GROUPS, rows, GC), lambda b: (b, 0, 0, 0)),
        out_shape=jax.ShapeDtypeStruct((bsz, GROUPS, rows, GC), F32),
        compiler_params=_cparams("parallel"),
        name="fourier_dense",
    )(xr, xi, cst, bfno.reshape(GROUPS, 1, GC))


def _fourier_fused_kernel(xr_ref, xi_ref, f1_ref, m_ref, bf_ref, o_ref, gr_scr, gi_scr, *, n1, n2):
    pin = _pitch(n2)

    def stage1(j, carry):
        t2 = 2 * j

        def ld(ref, s):
            return ref[0, 0, pl.ds(s, n1, stride=pin), :].astype(BF16)

        top = jnp.concatenate([ld(xr_ref, t2), ld(xr_ref, t2 + 1)], axis=1)
        bot = jnp.concatenate([ld(xi_ref, t2), ld(xi_ref, t2 + 1)], axis=1)
        g = jnp.dot(f1_ref[...], jnp.concatenate([top, bot], axis=0), preferred_element_type=F32)
        gr_scr[pl.ds(t2, n1, stride=pin), :] = g[:n1, :GC]
        gr_scr[pl.ds(t2 + 1, n1, stride=pin), :] = g[:n1, GC:]
        gi_scr[pl.ds(t2, n1, stride=pin), :] = g[n1:, :GC]
        gi_scr[pl.ds(t2 + 1, n1, stride=pin), :] = g[n1:, GC:]
        return carry

    jax.lax.fori_loop(0, n2 // 2, stage1, 0, unroll=min(16, n2 // 2))

    def stage2(k1, carry):
        base = pl.multiple_of(k1 * pin, 8)
        gg = jnp.concatenate([gr_scr[pl.ds(base, n2), :], gi_scr[pl.ds(base, n2), :]], axis=0).astype(BF16)
        yr = jnp.dot(m_ref[k1], gg, preferred_element_type=F32) + bf_ref[0]
        o_ref[0, 0, pl.ds(k1, n2, stride=OUT_PITCH), :] = yr
        return carry

    jax.lax.fori_loop(0, n1, stage2, 0, unroll=32)
    for k2 in range(n2):
        o_ref[0, 0, k2 * OUT_PITCH + n1:(k2 + 1) * OUT_PITCH, :] = jnp.zeros((OUT_PITCH - n1, GC), F32)


def _fourier_fused(xr, xi, f1, mtab, bfno, *, n1):
    bsz, _, rows_in, _ = xr.shape
    n2 = mtab.shape[1]
    pin = _pitch(n2)
    assert rows_in == n1 * pin and n1 == CHUNK
    in_blk = pl.BlockSpec((1, 1, rows_in, GC), lambda b, g: (b, g, 0, 0))
    return pl.pallas_call(
        functools.partial(_fourier_fused_kernel, n1=n1, n2=n2),
        grid=(bsz, GROUPS),
        in_specs=[in_blk, in_blk,
                  pl.BlockSpec((2 * n1, 2 * n1), lambda b, g: (0, 0)),
                  pl.BlockSpec((n1, n2, 2 * n2), lambda b, g: (0, 0, 0)),
                  pl.BlockSpec((1, 1, GC), lambda b, g: (g, 0, 0))],
        out_specs=pl.BlockSpec((1, 1, n2 * OUT_PITCH, GC), lambda b, g: (b, g, 0, 0)),
        out_shape=jax.ShapeDtypeStruct((bsz, GROUPS, n2 * OUT_PITCH, GC), F32),
        scratch_shapes=[pltpu.VMEM((n1 * pin, GC), F32), pltpu.VMEM((n1 * pin, GC), F32)],
        compiler_params=_cparams("parallel", "parallel"),
        name="fourier_fused",
    )(xr, xi, f1, mtab, bfno.reshape(GROUPS, 1, GC))


def _dft_consts(t):
    ang = 2.0 * np.pi / GC * np.outer(np.arange(GC), np.arange(GC))
    cs = np.concatenate([np.cos(ang), -np.sin(ang)], axis=0) / np.sqrt(GC)
    out = {"cs": cs.astype(np.float32)}
    if t <= 2 * CHUNK:
        ang = 2.0 * np.pi / t * np.mod(np.outer(np.arange(t), np.arange(t)), t)
        out["dense"] = (np.concatenate([np.cos(ang), np.sin(ang)], axis=1) / np.sqrt(t)).astype(np.float32)
    else:
        n1 = CHUNK
        n2 = t // n1
        ang = 2.0 * np.pi / n1 * np.mod(np.outer(np.arange(n1), np.arange(n1)), n1)
        c, s = np.cos(ang), np.sin(ang)
        out["f1"] = (np.block([[c, s], [-s, c]]) / np.sqrt(n1)).astype(np.float32)
        k = np.arange(n1)[:, None, None] + n1 * np.arange(n2)[None, :, None]
        ang = 2.0 * np.pi / t * np.mod(k * np.arange(n2)[None, None, :], t)
        out["mtab"] = (np.concatenate([np.cos(ang), np.sin(ang)], axis=2) / np.sqrt(n2)).astype(np.float32)
    return out


def _outproj_kernel(hf_ref, hb_ref, ga_ref, vs_ref, gb_ref, ym_ref, szc_ref, x_ref, gt_ref,
                    wout_ref, ghn_ref, gsgu_ref, gpost_ref, wsp_ref, bsp_ref, xo_ref, y_scr):
    tm = x_ref.shape[1]
    kc = S_WIDTH

    def project(k):
        return jnp.dot(y_scr[:, k * kc:(k + 1) * kc], wout_ref[k * kc:(k + 1) * kc, :], preferred_element_type=F32)

    out = None
    for h in range(HEADS):
        sl = slice(h * DV, (h + 1) * DV)
        hh = (hf_ref[0, :, sl] + hb_ref[0, :, sl]).astype(F32)
        hn = hh * jax.lax.rsqrt(jnp.mean(hh * hh, axis=-1, keepdims=True) + EPS) * ghn_ref[:, sl]
        y_scr[:, sl] = hn.astype(BF16) * ga_ref[0, :, sl]
        if (h + 1) * DV % kc == 0:
            part = project((h + 1) * DV // kc - 1)
            out = part if out is None else out + part
    vs = vs_ref[0].astype(F32)
    vn = (vs * jax.lax.rsqrt(jnp.mean(vs * vs, axis=-1, keepdims=True) + EPS) * gsgu_ref[...]).astype(BF16)
    for c in range(tm // CHUNK):
        rows = slice(c * CHUNK, (c + 1) * CHUNK)
        for g in range(GROUPS):
            cols = slice(g * GC, (g + 1) * GC)
            mixed = jnp.dot(wsp_ref[g], vn[rows, cols], preferred_element_type=F32) + bsp_ref[:, cols]
            y_scr[rows, M_WIDTH + g * GC:M_WIDTH + (g + 1) * GC] = gb_ref[0, rows, cols] * mixed.astype(BF16)
    out = out + project(M_WIDTH // kc)
    for c in range(tm // CHUNK):
        rows = slice(c * CHUNK, (c + 1) * CHUNK)
        for g in range(GROUPS):
            cols = slice(g * GC, (g + 1) * GC)
            y_scr[rows, M_WIDTH + S_WIDTH + g * GC:M_WIDTH + S_WIDTH + (g + 1) * GC] = (
                ym_ref[0, g, c * OUT_PITCH:c * OUT_PITCH + CHUNK, :].astype(BF16) * szc_ref[0, rows, cols])
    out = out + project((M_WIDTH + S_WIDTH) // kc)
    on = out * jax.lax.rsqrt(jnp.mean(out * out, axis=-1, keepdims=True) + EPS) * gpost_ref[...]
    xo_ref[0] = x_ref[0] + gt_ref[0] * on


def _outproj(hf, hb, ga, vs, gb, ym, szc, x, gt, w_out, ghn, gsgu, gpost, wsp, bsp, *, tm):
    bsz, t, d = x.shape
    wide = pl.BlockSpec((1, tm, M_WIDTH), lambda b, i: (b, i, 0))
    half = pl.BlockSpec((1, tm, S_WIDTH), lambda b, i: (b, i, 0))
    const2 = lambda shape: pl.BlockSpec(shape, lambda b, i: (0,) * len(shape))
    return pl.pallas_call(
        _outproj_kernel,
        grid=(bsz, t // tm),
        in_specs=[wide, wide, wide, half, half,
                  pl.BlockSpec((1, GROUPS, tm // CHUNK * OUT_PITCH, GC), lambda b, i: (b, 0, i, 0)),
                  half, wide,
                  pl.BlockSpec((1, 1, d), lambda b, i: (b, 0, 0)),
                  const2((2 * D_MODEL, d)), const2((1, M_WIDTH)), const2((1, S_WIDTH)), const2((1, d)),
                  const2((GROUPS, CHUNK, CHUNK)), const2((CHUNK, S_WIDTH))],
        out_specs=wide,
        out_shape=jax.ShapeDtypeStruct((bsz, t, d), F32),
        scratch_shapes=[pltpu.VMEM((tm, 2 * D_MODEL), BF16)],
        compiler_params=_cparams("parallel", "parallel"),
        name="outproj",
    )(hf, hb, ga, vs, gb, ym, szc, x, gt, w_out, ghn, gsgu, gpost, wsp, bsp)


_W_IN_SPLITS = np.cumsum([0, QK_W, QK_W, M_WIDTH, 4 * HEADS, M_WIDTH, M_WIDTH, S_WIDTH, S_WIDTH, S_WIDTH, F_WIDTH,
                          F_WIDTH])
_N_NAT = sum(_wrows(w) for _, w in _FULL_PIECES)
_GATE_ROWS = 4 * HEADS


def _wprep_plan():
    c = _W_IN_SPLITS
    starts, q_blocks, gate_block, f_block = [], None, None, None
    for i in (1, 0, 2, 3, 4, 5, 6, 7, 8, 9, 10):
        if i == 3:
            gate_block = len(starts)
            starts.append(0)
            continue
        first = len(starts)
        if i == 9:
            f_block = first
            starts += list(range(int(c[i]), int(c[i + 1]), _WT_BLK // 2))
            continue
        starts += list(range(int(c[i]), int(c[i + 1]), _WT_BLK))
        if i == 0:
            q_blocks = (first, len(starts))
    return starts, q_blocks, gate_block, f_block


F_STEP_GROUPS = _WT_BLK // (2 * GC)


def _wprep_kernel(starts_ref, w_ref, g_ref, cst_ref, wf_ref, o_ref, *, q_blocks, gate_block, f_block):
    del starts_ref
    s = pl.program_id(1)
    n_f = GROUPS // F_STEP_GROUPS

    @pl.when((s != gate_block) & ((s < f_block) | (s >= f_block + n_f)))
    def _():
        scale = jnp.where((s >= q_blocks[0]) & (s < q_blocks[1]), DQK ** -0.5, 1.0)
        o_ref[0] = (w_ref[0] * scale).astype(BF16)

    for k in range(n_f):
        @pl.when(s == f_block + k)
        def _(k=k):
            outs = []
            for gg in range(F_STEP_GROUPS):
                a = w_ref[0, gg * GC:(gg + 1) * GC, :].astype(BF16)
                wc = jnp.dot(cst_ref[...], wf_ref[0, k * F_STEP_GROUPS + gg].astype(BF16),
                             preferred_element_type=F32)
                for half in range(2):
                    cw_t = wc[half * GC:(half + 1) * GC].T.astype(BF16)
                    outs.append(jnp.dot(cw_t, a, preferred_element_type=F32))
            o_ref[0] = jnp.concatenate(outs, axis=0).astype(BF16)

    @pl.when(s == gate_block)
    def _():
        g = g_ref[0]
        zero = jnp.zeros((LANES - UNITS, g.shape[1]), F32)
        tail = jnp.zeros((_WT_BLK - 2 * LANES, g.shape[1]), F32)
        o_ref[0] = jnp.concatenate([g[0:8], g[16:24], zero,
                                    g[8:16], g[24:32], zero,
                                    tail], axis=0).astype(BF16)


def _wprep(w_in, cst, w_fno):
    depth, d, p_in = w_in.shape
    w_t = jnp.swapaxes(w_in, 1, 2)
    starts, q_blocks, gate_block, f_block = _wprep_plan()
    assert _W_IN_SPLITS[3] % _GATE_ROWS == 0 and len(starts) * _WT_BLK == QK_W + _N_NAT
    grid_spec = pltpu.PrefetchScalarGridSpec(
        num_scalar_prefetch=1,
        grid=(depth, len(starts)),
        in_specs=[pl.BlockSpec((pl.Element(1), pl.Element(_WT_BLK), pl.Element(d)),
                               lambda l, s, st: (l, st[s] * 8, 0)),
                  pl.BlockSpec((1, _GATE_ROWS, d), lambda l, s, st: (l, int(_W_IN_SPLITS[3]) // _GATE_ROWS, 0)),
                  pl.BlockSpec((2 * GC, GC), lambda l, s, st: (0, 0)),
                  pl.BlockSpec((1, GROUPS, GC, GC), lambda l, s, st: (l, 0, 0, 0))],
        out_specs=pl.BlockSpec((1, _WT_BLK, d), lambda l, s, st: (l, s, 0)),
    )
    return pl.pallas_call(
        functools.partial(_wprep_kernel, q_blocks=q_blocks, gate_block=gate_block, f_block=f_block),
        grid_spec=grid_spec,
        out_shape=jax.ShapeDtypeStruct((depth, QK_W + _N_NAT, d), BF16),
        compiler_params=_cparams("parallel", "arbitrary"),
        name="wprep",
    )(jnp.asarray(starts, jnp.int32) // 8, w_t, w_t, cst, w_fno)


def _gate_bias(b_gate_l):
    bpad = jnp.zeros((LANES - UNITS,), b_gate_l.dtype)
    b_i = jnp.concatenate([b_gate_l[0:8], b_gate_l[16:24], bpad]).reshape(1, LANES)
    b_f = jnp.concatenate([b_gate_l[8:16], b_gate_l[24:32], bpad]).reshape(1, LANES)
    return b_i, b_f


def kernel(x, c, ctx, c_ctx, w_mod, b_mod, g_pre, g_post, w_in, b_gate, g_hnorm, g_sgu, w_sp, b_sp, w_fno, b_fno, w_out):
    bsz, t_lat, d = x.shape
    t_ctx = ctx.shape[1]
    depth = w_mod.shape[0]
    assert d == D_MODEL and t_lat % (2 * CHUNK) == 0 and t_ctx % (2 * CHUNK) == 0 and bsz + 1 <= 8

    cc = jnp.concatenate([c, c_ctx[None, :], jnp.zeros((8 - bsz - 1, d), c.dtype)], axis=0)
    mod = _modulation(cc, w_mod, b_mod)
    consts_lat, consts_ctx = _dft_consts(t_lat), _dft_consts(t_ctx)
    w_t_all = _wprep(w_in, jnp.asarray(consts_lat["cs"]).astype(BF16), w_fno)

    def fourier(xr, xi, consts, bfno):
        if "dense" in consts:
            return _fourier_dense(xr, xi, jnp.asarray(consts["dense"]).astype(BF16), bfno)
        return _fourier_fused(xr, xi, jnp.asarray(consts["f1"]).astype(BF16),
                              jnp.asarray(consts["mtab"]).astype(BF16), bfno, n1=CHUNK)

    cn_zero = jnp.zeros((bsz, UNITS, DQK, 2 * DV), F32)
    m_zero = jnp.zeros((bsz, UNITS, LANES), F32)
    xc = ctx
    for l in range(depth):
        sh_l, sc_l, gt_l = (mod[l, :bsz, i * d:(i + 1) * d].reshape(bsz, 1, d) for i in range(3))
        sh_c, sc_c, gt_c = (jnp.broadcast_to(mod[l, bsz, i * d:(i + 1) * d].reshape(1, 1, d), (bsz, 1, d))
                            for i in range(3))
        b_i, b_f = _gate_bias(b_gate[l])
        gpre = g_pre[l].reshape(1, d)
        wsp = w_sp[l].astype(BF16)
        bsp = jnp.broadcast_to(b_sp[l].T[:, :, None], (CHUNK, GROUPS, GC)).reshape(CHUNK, S_WIDTH)
        wo = w_out[l].astype(BF16)
        tail = (wo, g_hnorm[l].reshape(1, M_WIDTH), g_sgu[l].reshape(1, S_WIDTH), g_post[l].reshape(1, d), wsp, bsp)
        front = functools.partial(_inproj, g_pre=gpre, w_t=w_t_all, b_i=b_i, b_f=b_f, layer=l)

        last = l == depth - 1
        pc = front(xc, sc_c, sh_c, full=not last, tm=256)
        hf_c, hb_c, cn_c, m_c = _mlstm(pc["q"], pc["kT"], pc["v"], pc["cola"], pc["colb"], pc["rowa"], cn_zero, m_zero)
        p = front(x, sc_l, sh_l, full=True, tm=512)
        hf, hb, _, _ = _mlstm(p["q"], p["kT"], p["v"], p["cola"], p["colb"], p["rowa"], cn_c, m_c)
        ym = fourier(p["xr"], p["xi"], consts_lat, b_fno[l])
        x = _outproj(hf, hb, p["ga"], p["vs"], p["gb"], ym, p["szc"], x, gt_l, *tail, tm=512)
        if not last:
            ymc = fourier(pc["xr"], pc["xi"], consts_ctx, b_fno[l])
            xc = _outproj(hf_c, hb_c, pc["ga"], pc["vs"], pc["gb"], ymc, pc["szc"], xc, gt_c, *tail, tm=256)
    return x
```

```python
import functools

import numpy as np
import jax
import jax.numpy as jnp
from jax.experimental import pallas as pl
from jax.experimental.pallas import tpu as pltpu

EPS = 1e-6
LOG2E = 1.4426950408889634
D_MODEL = 1024
HEADS = 8
DV = 128
DQK = 64
QK_W = HEADS * DQK
M_WIDTH = HEADS * DV
CHUNK = 128
S_WIDTH = 512
F_WIDTH = 512
GROUPS = 4
GC = 128
UNITS = 2 * HEADS
LANES = 128

VMEM_LIMIT = 56 * 1024 * 1024

F32 = jnp.float32
BF16 = jnp.bfloat16


def _cparams(*sem):
    return pltpu.CompilerParams(dimension_semantics=sem, vmem_limit_bytes=VMEM_LIMIT)


def _sigmoid(x):
    return 1.0 / (1.0 + jnp.exp(-x))


def _silu(x):
    return x * _sigmoid(x)


def _mod_kernel(cc_ref, w_ref, b_ref, o_ref):
    s = _silu(cc_ref[...]).astype(BF16)
    w = w_ref[0].astype(BF16)
    o_ref[0] = jnp.dot(s, w, preferred_element_type=F32) + b_ref[0]


def _modulation(cc, w_mod, b_mod):
    depth, d, d3 = w_mod.shape
    return pl.pallas_call(
        _mod_kernel,
        grid=(depth,),
        in_specs=[pl.BlockSpec((8, d), lambda l: (0, 0)),
                  pl.BlockSpec((1, d, d3), lambda l: (l, 0, 0)),
                  pl.BlockSpec((1, 1, d3), lambda l: (l, 0, 0))],
        out_specs=pl.BlockSpec((1, 8, d3), lambda l: (l, 0, 0)),
        out_shape=jax.ShapeDtypeStruct((depth, 8, d3), F32),
        compiler_params=_cparams("arbitrary"),
        name="modulation",
    )(cc, w_mod, b_mod.reshape(depth, 1, d3))


def _log_sigmoid(x):
    return jnp.minimum(x, 0.0) - jnp.log1p(jnp.exp(-jnp.abs(x)))


def _gate_scans(gi, gf, cola_ref, colb_ref, rowa_ref):
    tm = gi.shape[0]
    n_c = tm // CHUNK
    assert n_c * UNITS <= LANES
    row = jax.lax.broadcasted_iota(jnp.int32, (CHUNK, LANES), 0)
    lane = jax.lax.broadcasted_iota(jnp.int32, (CHUNK, LANES), 1)
    fwd = lane % UNITS < HEADS
    lf = _log_sigmoid(gf)

    def pack(x):
        out = x[0:CHUNK]
        for c in range(1, n_c):
            out = jnp.where(lane // UNITS == c, pltpu.roll(x[c * CHUNK:(c + 1) * CHUNK], c * UNITS, axis=1), out)
        return out

    def scan(x, op):
        pre, suf = x, x
        k = 1
        while k < CHUNK:
            sh = pltpu.roll(pre, k, axis=0)
            pre = jnp.where(row >= k, op(pre, sh), pre)
            sh = pltpu.roll(suf, CHUNK - k, axis=0)
            suf = jnp.where(row < CHUNK - k, op(suf, sh), suf)
            k *= 2
        return jnp.where(fwd, pre, suf)

    b = scan(pack(lf), jnp.add)
    a = (pack(gi) - b) * LOG2E
    amax = scan(a, jnp.maximum)
    b = b * LOG2E
    a_t = a.T
    for c in range(n_c):
        rows = slice(c * CHUNK, (c + 1) * CHUNK)
        back = (LANES - c * UNITS) % LANES
        cola_ref[0, rows, :] = pltpu.roll(amax, back, axis=1) if c else amax
        colb_ref[0, rows, :] = pltpu.roll(b, back, axis=1) if c else b
        rowa_ref[0, :, rows] = a_t[c * UNITS:(c + 1) * UNITS, :]


def _pitch(n):
    p = n
    while (p // 8) % 2 == 0:
        p += 8
    return p


def _store_slabs(ref, g, z, n2):
    pin = _pitch(n2)
    for r in range(z.shape[0] // n2):
        ref[0, g, r * pin:r * pin + n2, :] = z[r * n2:(r + 1) * n2]
        ref[0, g, r * pin + n2:(r + 1) * pin, :] = jnp.zeros((pin - n2, GC), F32)


def _inproj_kernel(x_ref, sc_ref, sh_ref, g_ref, wt_ref, bi_ref, bf_ref, *out_refs, pieces, slab_n2):
    nt = (((1,), (1,)), ((), ()))
    x = x_ref[0]
    y = x * jax.lax.rsqrt(jnp.mean(x * x, axis=-1, keepdims=True) + EPS)
    h = y * (g_ref[...] * (1.0 + sc_ref[0])) + sh_ref[0]
    hb = h.astype(BF16)

    def silu(z):
        t = z * 0.5
        return t + t * jnp.tanh(t)

    held = {}
    oi = 0
    for name, off, width in pieces:
        if name == "kT":
            r = jax.lax.dot_general(wt_ref[0:QK_W, :], hb, nt, preferred_element_type=F32)
            out_refs[oi][0] = r.astype(BF16)
            oi += 1
            continue
        r = jax.lax.dot_general(hb, wt_ref[QK_W + off:QK_W + off + width, :], nt, preferred_element_type=F32)
        if name == "gates":
            _gate_scans(r[:, :LANES] + bi_ref[...], r[:, LANES:] + bf_ref[...], *out_refs[oi:oi + 3])
            oi += 3
        elif name == "f":
            xr_ref, xi_ref = out_refs[oi], out_refs[oi + 1]
            for g in range(GROUPS):
                zr, zi = r[:, 2 * g * GC:(2 * g + 1) * GC], r[:, (2 * g + 1) * GC:(2 * g + 2) * GC]
                if slab_n2 is None:
                    xr_ref[0, :, g * GC:(g + 1) * GC] = zr.astype(BF16)
                    xi_ref[0, :, g * GC:(g + 1) * GC] = zi.astype(BF16)
                else:
                    _store_slabs(xr_ref, g, zr, slab_n2)
                    _store_slabs(xi_ref, g, zi, slab_n2)
            oi += 2
        elif name in _HELD:
            held[name] = r
        else:
            if name == "za":
                r = (0.5 + 0.5 * jnp.tanh(held["o"] * 0.5)) * silu(r)
            elif name == "zb":
                r = held["u"] * silu(r)
            elif name == "zc":
                r = silu(r)
            out_refs[oi][0] = r.astype(BF16)
            oi += 1


_FULL_PIECES = (("q", QK_W), ("v", M_WIDTH), ("gates", 2 * LANES), ("o", M_WIDTH), ("za", M_WIDTH),
                ("u", S_WIDTH), ("vs", S_WIDTH), ("zb", S_WIDTH), ("f", 2 * F_WIDTH), ("zc", F_WIDTH))
_STATE_PIECES = _FULL_PIECES[:3]
_HELD = ("o", "u")
_GATED_NAME = {"za": "ga", "zb": "gb", "zc": "szc"}
_WT_BLK = 512


def _wrows(width):
    return -(-width // _WT_BLK) * _WT_BLK


def _inproj(x, sc, sh, g_pre, w_t, b_i, b_f, *, layer, full, tm):
    bsz, t, d = x.shape
    names = _FULL_PIECES if full else _STATE_PIECES
    slab_n2 = t // CHUNK if t > 2 * CHUNK else None
    row_spec = lambda width: pl.BlockSpec((1, tm, width), lambda b, i: (b, i, 0))
    pieces, off = [], 0
    out_names, out_shapes, out_specs = [], [], []
    for name, width in names:
        pieces.append((name, off, width))
        off += _wrows(width)
        if name == "gates":
            out_names += ["cola", "colb", "rowa"]
            out_shapes += [jax.ShapeDtypeStruct((bsz, t, LANES), F32)] * 2 + [jax.ShapeDtypeStruct((bsz, UNITS, t), F32)]
            out_specs += [row_spec(LANES), row_spec(LANES), pl.BlockSpec((1, UNITS, tm), lambda b, i: (b, 0, i))]
        elif name == "f":
            out_names += ["xr", "xi"]
            if slab_n2 is None:
                out_shapes += [jax.ShapeDtypeStruct((bsz, t, F_WIDTH), BF16)] * 2
                out_specs += [row_spec(F_WIDTH)] * 2
            else:
                assert tm % slab_n2 == 0
                pin = _pitch(slab_n2)
                out_shapes += [jax.ShapeDtypeStruct((bsz, GROUPS, CHUNK * pin, GC), F32)] * 2
                out_specs += [pl.BlockSpec((1, GROUPS, tm // slab_n2 * pin, GC), lambda b, i: (b, 0, i, 0))] * 2
        elif name not in _HELD:
            out_names.append(_GATED_NAME.get(name, name))
            out_shapes.append(jax.ShapeDtypeStruct((bsz, t, width), BF16))
            out_specs.append(row_spec(width))
    n_nat = off
    pieces.append(("kT", 0, QK_W))
    out_names.append("kT")
    out_shapes.append(jax.ShapeDtypeStruct((bsz, QK_W, t), BF16))
    out_specs.append(pl.BlockSpec((1, QK_W, tm), lambda b, i: (b, 0, i)))
    const = lambda shape, **kw: pl.BlockSpec(shape, lambda b, i: (0,) * len(shape), **kw)
    outs = pl.pallas_call(
        functools.partial(_inproj_kernel, pieces=tuple(pieces), slab_n2=slab_n2),
        grid=(bsz, t // tm),
        in_specs=[pl.BlockSpec((1, tm, d), lambda b, i: (b, i, 0)),
                  pl.BlockSpec((1, 1, d), lambda b, i: (b, 0, 0)),
                  pl.BlockSpec((1, 1, d), lambda b, i: (b, 0, 0)),
                  const((1, d)),
                  pl.BlockSpec((None, QK_W + n_nat, d), lambda b, i: (layer, 0, 0), pipeline_mode=pl.Buffered(1)),
                  const((1, LANES)), const((1, LANES))],
        out_specs=out_specs,
        out_shape=out_shapes,
        compiler_params=_cparams("parallel", "parallel"),
        name="inproj_full" if full else "inproj_state",
    )(x, sc, sh, g_pre, w_t, b_i, b_f)
    return dict(zip(out_names, outs))


def _mlstm_unit(h, d, sub, q_ref, kt_ref, v_ref, rowa_ref, m_diag, g_cols, c_cols, g_ends, hout_ref, cn_ref, mask):
    i = d * HEADS + h
    pair = h // 2
    rows = slice(sub * CHUNK, (sub + 1) * CHUNK)
    q_pair = q_ref[0, rows, pair * LANES:(pair + 1) * LANES]
    kt_h = kt_ref[0, h * DQK:(h + 1) * DQK, rows]
    zk = jnp.zeros((DQK, CHUNK), BF16)
    kt_ext = jnp.concatenate([kt_h, zk] if h % 2 == 0 else [zk, kt_h], axis=0)
    v_h = v_ref[0, rows, h * DV:(h + 1) * DV]
    vaug = jnp.concatenate([v_h, jnp.ones((CHUNK, DV), BF16)], axis=1)
    g = jnp.broadcast_to(g_cols[:, i:i + 1], (CHUNK, CHUNK))
    c = jnp.broadcast_to(c_cols[:, i:i + 1], (CHUNK, CHUNK))
    g_end = jnp.broadcast_to(g_ends[:, i:i + 1], (1, LANES))
    m_row = jnp.broadcast_to(m_diag[:, i:i + 1], (1, LANES))
    a_row = rowa_ref[0, h:h + 1, rows]
    cn = cn_ref[0, i]

    dmat = jnp.where(mask, jnp.exp2(a_row - g), 0.0)
    s = jnp.dot(q_pair, kt_ext, preferred_element_type=F32)
    p = (s * dmat).astype(BF16)
    qs = q_pair * jnp.exp2(m_row - g).astype(BF16)
    lhs = jnp.concatenate([p, qs], axis=1)
    zc = jnp.zeros((DQK, 2 * DV), BF16)
    cnb = cn.astype(BF16)
    cn_ext = jnp.concatenate([cnb, zc] if h % 2 == 0 else [zc, cnb], axis=0)
    rhs = jnp.concatenate([vaug, cn_ext], axis=0)
    out = jnp.dot(lhs, rhs, preferred_element_type=F32)
    num, den = out[:, :DV], out[:, DV:]
    hval = num / jnp.maximum(jnp.abs(den), jnp.exp2(c))
    hout_ref[0, rows, h * DV:(h + 1) * DV] = hval.astype(hout_ref.dtype)

    kts = kt_h * jnp.exp2(a_row - g_end).astype(BF16)
    upd = jnp.dot(kts, vaug, preferred_element_type=F32)
    decay = jnp.exp2(m_row - g_end)
    cn_ref[0, i] = jnp.concatenate([decay, decay], axis=1) * cn + upd


MLSTM_SUB = 8


def _mlstm_kernel(qf_ref, ktf_ref, vf_ref, caf_ref, cbf_ref, raf_ref,
                  qb_ref, ktb_ref, vb_ref, cab_ref, cbb_ref, rab_ref,
                  cn0_ref, m0_ref, hf_ref, hb_ref, cn_ref, m_ref, *, n_sub):
    @pl.when(pl.program_id(1) == 0)
    def _():
        cn_ref[...] = cn0_ref[...]
        m_ref[...] = m0_ref[...]

    t_idx = jax.lax.broadcasted_iota(jnp.int32, (CHUNK, CHUNK), 0)
    s_idx = jax.lax.broadcasted_iota(jnp.int32, (CHUNK, CHUNK), 1)
    unit_row = jax.lax.broadcasted_iota(jnp.int32, (UNITS, LANES), 0)
    unit_lane = jax.lax.broadcasted_iota(jnp.int32, (UNITS, LANES), 1)
    m_diag = jnp.sum(jnp.where(unit_row == unit_lane, m_ref[0], 0.0), axis=0, keepdims=True)
    fwd_lane = jax.lax.broadcasted_iota(jnp.int32, (1, LANES), 1) < HEADS

    def columns(ca_ref, cb_ref, sub, end):
        rows = slice(sub * CHUNK, (sub + 1) * CHUNK)
        g_cols = jnp.maximum(m_diag, ca_ref[0, rows, :])
        g_ends = g_cols[end:end + 1, :]
        m_new = cb_ref[0, sub * CHUNK + end:sub * CHUNK + end + 1, :] + g_ends
        return (g_cols, -(cb_ref[0, rows, :] + g_cols), g_ends), m_new

    for step in range(n_sub):
        sub_f, sub_b = step, n_sub - 1 - step
        cols_f, m_new_f = columns(caf_ref, cbf_ref, sub_f, CHUNK - 1)
        cols_b, m_new_b = columns(cab_ref, cbb_ref, sub_b, 0)
        for h in range(HEADS):
            _mlstm_unit(h, 0, sub_f, qf_ref, ktf_ref, vf_ref, raf_ref, m_diag, *cols_f, hf_ref, cn_ref, s_idx <= t_idx)
            _mlstm_unit(h, 1, sub_b, qb_ref, ktb_ref, vb_ref, rab_ref, m_diag, *cols_b, hb_ref, cn_ref, s_idx >= t_idx)
        m_diag = jnp.where(fwd_lane, m_new_f, m_new_b)
    m_ref[0] = jnp.where(unit_row == unit_lane, jnp.broadcast_to(m_diag, (UNITS, LANES)), 0.0)


def _mlstm(q, kt, v, cola, colb, rowa, cn0, m0):
    bsz, t, _ = q.shape
    n_sub = min(MLSTM_SUB, t // CHUNK)
    blk = n_sub * CHUNK
    nc = t // blk

    def specs(rev):
        cj = (lambda j: nc - 1 - j) if rev else (lambda j: j)
        d = 1 if rev else 0
        return [pl.BlockSpec((1, blk, QK_W), lambda b, j: (b, cj(j), 0)),
                pl.BlockSpec((1, QK_W, blk), lambda b, j: (b, 0, cj(j))),
                pl.BlockSpec((1, blk, M_WIDTH), lambda b, j: (b, cj(j), 0)),
                pl.BlockSpec((1, blk, LANES), lambda b, j: (b, cj(j), 0)),
                pl.BlockSpec((1, blk, LANES), lambda b, j: (b, cj(j), 0)),
                pl.BlockSpec((1, HEADS, blk), lambda b, j: (b, d, cj(j)))]

    cn_spec = pl.BlockSpec((1, UNITS, DQK, 2 * DV), lambda b, j: (b, 0, 0, 0))
    m_spec = pl.BlockSpec((1, UNITS, LANES), lambda b, j: (b, 0, 0))
    args = (q, kt, v, cola, colb, rowa)
    return pl.pallas_call(
        functools.partial(_mlstm_kernel, n_sub=n_sub),
        grid=(bsz, nc),
        in_specs=specs(False) + specs(True) + [cn_spec, m_spec],
        out_specs=[pl.BlockSpec((1, blk, M_WIDTH), lambda b, j: (b, j, 0)),
                   pl.BlockSpec((1, blk, M_WIDTH), lambda b, j: (b, nc - 1 - j, 0)),
                   cn_spec, m_spec],
        out_shape=[jax.ShapeDtypeStruct((bsz, t, M_WIDTH), BF16),
                   jax.ShapeDtypeStruct((bsz, t, M_WIDTH), BF16),
                   jax.ShapeDtypeStruct((bsz, UNITS, DQK, 2 * DV), F32),
                   jax.ShapeDtypeStruct((bsz, UNITS, LANES), F32)],
        compiler_params=_cparams("parallel", "arbitrary"),
        name="mlstm",
    )(*args, *args, cn0, m0)


OUT_PITCH = _pitch(CHUNK)


def _fourier_dense_kernel(xr_ref, xi_ref, cst_ref, bf_ref, o_ref):
    t = xr_ref.shape[1]
    xx = jnp.concatenate([xr_ref[0], xi_ref[0]], axis=0)
    yr = jnp.dot(cst_ref[...], xx, preferred_element_type=F32)
    for g in range(GROUPS):
        for c in range(t // CHUNK):
            o_ref[0, g, c * OUT_PITCH:c * OUT_PITCH + CHUNK, :] = (
                yr[c * CHUNK:(c + 1) * CHUNK, g * GC:(g + 1) * GC] + bf_ref[g])
            o_ref[0, g, c * OUT_PITCH + CHUNK:(c + 1) * OUT_PITCH, :] = jnp.zeros((OUT_PITCH - CHUNK, GC), F32)


def _fourier_dense(xr, xi, cst, bfno):
    bsz, t, w = xr.shape
    blk = pl.BlockSpec((1, t, w), lambda b: (b, 0, 0))
    rows = t // CHUNK * OUT_PITCH
    return pl.pallas_call(
        _fourier_dense_kernel,
        grid=(bsz,),
        in_specs=[blk, blk,
                  pl.BlockSpec((t, 2 * t), lambda b: (0, 0)),
                  pl.BlockSpec((GROUPS, 1, GC), lambda b: (0, 0, 0))],
        out_specs=pl.BlockSpec((1, GROUPS, rows, GC), lambda b: (b, 0, 0, 0)),
        out_shape=jax.ShapeDtypeStruct((bsz, GROUPS, rows, GC), F32),
        compiler_params=_cparams("parallel"),
        name="fourier_dense",
    )(xr, xi, cst, bfno.reshape(GROUPS, 1, GC))


def _fourier_fused_kernel(xr_ref, xi_ref, f1_ref, m_ref, bf_ref, o_ref, gr_scr, gi_scr, *, n1, n2):
    pin = _pitch(n2)

    def stage1(j, carry):
        t2 = 2 * j

        def ld(ref, s):
            return ref[0, 0, pl.ds(s, n1, stride=pin), :].astype(BF16)

        top = jnp.concatenate([ld(xr_ref, t2), ld(xr_ref, t2 + 1)], axis=1)
        bot = jnp.concatenate([ld(xi_ref, t2), ld(xi_ref, t2 + 1)], axis=1)
        g = jnp.dot(f1_ref[...], jnp.concatenate([top, bot], axis=0), preferred_element_type=F32)
        gr_scr[pl.ds(t2, n1, stride=pin), :] = g[:n1, :GC]
        gr_scr[pl.ds(t2 + 1, n1, stride=pin), :] = g[:n1, GC:]
        gi_scr[pl.ds(t2, n1, stride=pin), :] = g[n1:, :GC]
        gi_scr[pl.ds(t2 + 1, n1, stride=pin), :] = g[n1:, GC:]
        return carry

    jax.lax.fori_loop(0, n2 // 2, stage1, 0, unroll=True)

    def stage2(k1, carry):
        base = pl.multiple_of(k1 * pin, 8)
        gg = jnp.concatenate([gr_scr[pl.ds(base, n2), :], gi_scr[pl.ds(base, n2), :]], axis=0).astype(BF16)
        yr = jnp.dot(m_ref[k1], gg, preferred_element_type=F32) + bf_ref[0]
        o_ref[0, 0, pl.ds(k1, n2, stride=OUT_PITCH), :] = yr
        return carry

    jax.lax.fori_loop(0, n1, stage2, 0, unroll=True)
    for k2 in range(n2):
        o_ref[0, 0, k2 * OUT_PITCH + n1:(k2 + 1) * OUT_PITCH, :] = jnp.zeros((OUT_PITCH - n1, GC), F32)


def _fourier_fused(xr, xi, f1, mtab, bfno, *, n1):
    bsz, _, rows_in, _ = xr.shape
    n2 = mtab.shape[1]
    pin = _pitch(n2)
    assert rows_in == n1 * pin and n1 == CHUNK
    in_blk = pl.BlockSpec((1, 1, rows_in, GC), lambda b, g: (b, g, 0, 0))
    return pl.pallas_call(
        functools.partial(_fourier_fused_kernel, n1=n1, n2=n2),
        grid=(bsz, GROUPS),
        in_specs=[in_blk, in_blk,
                  pl.BlockSpec((2 * n1, 2 * n1), lambda b, g: (0, 0)),
                  pl.BlockSpec((n1, n2, 2 * n2), lambda b, g: (0, 0, 0)),
                  pl.BlockSpec((1, 1, GC), lambda b, g: (g, 0, 0))],
        out_specs=pl.BlockSpec((1, 1, n2 * OUT_PITCH, GC), lambda b, g: (b, g, 0, 0)),
        out_shape=jax.ShapeDtypeStruct((bsz, GROUPS, n2 * OUT_PITCH, GC), F32),
        scratch_shapes=[pltpu.VMEM((n1 * pin, GC), F32), pltpu.VMEM((n1 * pin, GC), F32)],
        compiler_params=_cparams("parallel", "parallel"),
        name="fourier_fused",
    )(xr, xi, f1, mtab, bfno.reshape(GROUPS, 1, GC))


def _dft_consts(t):
    ang = 2.0 * np.pi / GC * np.outer(np.arange(GC), np.arange(GC))
    cs = np.concatenate([np.cos(ang), -np.sin(ang)], axis=0) / np.sqrt(GC)
    out = {"cs": cs.astype(np.float32)}
    if t <= 2 * CHUNK:
        ang = 2.0 * np.pi / t * np.mod(np.outer(np.arange(t), np.arange(t)), t)
        out["dense"] = (np.concatenate([np.cos(ang), np.sin(ang)], axis=1) / np.sqrt(t)).astype(np.float32)
    else:
        n1 = CHUNK
        n2 = t // n1
        ang = 2.0 * np.pi / n1 * np.mod(np.outer(np.arange(n1), np.arange(n1)), n1)
        c, s = np.cos(ang), np.sin(ang)
        out["f1"] = (np.block([[c, s], [-s, c]]) / np.sqrt(n1)).astype(np.float32)
        k = np.arange(n1)[:, None, None] + n1 * np.arange(n2)[None, :, None]
        ang = 2.0 * np.pi / t * np.mod(k * np.arange(n2)[None, None, :], t)
        out["mtab"] = (np.concatenate([np.cos(ang), np.sin(ang)], axis=2) / np.sqrt(n2)).astype(np.float32)
    return out


def _outproj_kernel(hf_ref, hb_ref, ga_ref, vs_ref, gb_ref, ym_ref, szc_ref, x_ref, gt_ref,
                    wout_ref, ghn_ref, gsgu_ref, gpost_ref, wsp_ref, bsp_ref, xo_ref, y_scr):
    tm = x_ref.shape[1]
    kc = S_WIDTH

    def project(k):
        return jnp.dot(y_scr[:, k * kc:(k + 1) * kc], wout_ref[k * kc:(k + 1) * kc, :], preferred_element_type=F32)

    out = None
    for h in range(HEADS):
        sl = slice(h * DV, (h + 1) * DV)
        hh = (hf_ref[0, :, sl] + hb_ref[0, :, sl]).astype(F32)
        hn = hh * jax.lax.rsqrt(jnp.mean(hh * hh, axis=-1, keepdims=True) + EPS) * ghn_ref[:, sl]
        y_scr[:, sl] = hn.astype(BF16) * ga_ref[0, :, sl]
        if (h + 1) * DV % kc == 0:
            part = project((h + 1) * DV // kc - 1)
            out = part if out is None else out + part
    vs = vs_ref[0].astype(F32)
    vn = (vs * jax.lax.rsqrt(jnp.mean(vs * vs, axis=-1, keepdims=True) + EPS) * gsgu_ref[...]).astype(BF16)
    for c in range(tm // CHUNK):
        rows = slice(c * CHUNK, (c + 1) * CHUNK)
        for g in range(GROUPS):
            cols = slice(g * GC, (g + 1) * GC)
            mixed = jnp.dot(wsp_ref[g], vn[rows, cols], preferred_element_type=F32) + bsp_ref[:, cols]
            y_scr[rows, M_WIDTH + g * GC:M_WIDTH + (g + 1) * GC] = gb_ref[0, rows, cols] * mixed.astype(BF16)
    out = out + project(M_WIDTH // kc)
    for c in range(tm // CHUNK):
        rows = slice(c * CHUNK, (c + 1) * CHUNK)
        for g in range(GROUPS):
            cols = slice(g * GC, (g + 1) * GC)
            y_scr[rows, M_WIDTH + S_WIDTH + g * GC:M_WIDTH + S_WIDTH + (g + 1) * GC] = (
                ym_ref[0, g, c * OUT_PITCH:c * OUT_PITCH + CHUNK, :].astype(BF16) * szc_ref[0, rows, cols])
    out = out + project((M_WIDTH + S_WIDTH) // kc)
    on = out * jax.lax.rsqrt(jnp.mean(out * out, axis=-1, keepdims=True) + EPS) * gpost_ref[...]
    xo_ref[0] = x_ref[0] + gt_ref[0] * on


def _outproj(hf, hb, ga, vs, gb, ym, szc, x, gt, w_out, ghn, gsgu, gpost, wsp, bsp, *, tm):
    bsz, t, d = x.shape
    wide = pl.BlockSpec((1, tm, M_WIDTH), lambda b, i: (b, i, 0))
    half = pl.BlockSpec((1, tm, S_WIDTH), lambda b, i: (b, i, 0))
    const2 = lambda shape: pl.BlockSpec(shape, lambda b, i: (0,) * len(shape))
    return pl.pallas_call(
        _outproj_kernel,
        grid=(bsz, t // tm),
        in_specs=[wide, wide, wide, half, half,
                  pl.BlockSpec((1, GROUPS, tm // CHUNK * OUT_PITCH, GC), lambda b, i: (b, 0, i, 0)),
                  half, wide,
                  pl.BlockSpec((1, 1, d), lambda b, i: (b, 0, 0)),
                  const2((2 * D_MODEL, d)), const2((1, M_WIDTH)), const2((1, S_WIDTH)), const2((1, d)),
                  const2((GROUPS, CHUNK, CHUNK)), const2((CHUNK, S_WIDTH))],
        out_specs=wide,
        out_shape=jax.ShapeDtypeStruct((bsz, t, d), F32),
        scratch_shapes=[pltpu.VMEM((tm, 2 * D_MODEL), BF16)],
        compiler_params=_cparams("parallel", "parallel"),
        name="outproj",
    )(hf, hb, ga, vs, gb, ym, szc, x, gt, w_out, ghn, gsgu, gpost, wsp, bsp)


_W_IN_SPLITS = np.cumsum([0, QK_W, QK_W, M_WIDTH, 4 * HEADS, M_WIDTH, M_WIDTH, S_WIDTH, S_WIDTH, S_WIDTH, F_WIDTH,
                          F_WIDTH])
_N_NAT = sum(_wrows(w) for _, w in _FULL_PIECES)
_GATE_ROWS = 4 * HEADS


def _wprep_plan():
    c = _W_IN_SPLITS
    starts, q_blocks, gate_block, f_block = [], None, None, None
    for i in (1, 0, 2, 3, 4, 5, 6, 7, 8, 9, 10):
        if i == 3:
            gate_block = len(starts)
            starts.append(0)
            continue
        first = len(starts)
        if i == 9:
            f_block = first
            starts += list(range(int(c[i]), int(c[i + 1]), _WT_BLK // 2))
            continue
        starts += list(range(int(c[i]), int(c[i + 1]), _WT_BLK))
        if i == 0:
            q_blocks = (first, len(starts))
    return starts, q_blocks, gate_block, f_block


F_STEP_GROUPS = _WT_BLK // (2 * GC)


def _wprep_kernel(starts_ref, w_ref, g_ref, cst_ref, wf_ref, o_ref, *, q_blocks, gate_block, f_block):
    del starts_ref
    s = pl.program_id(1)
    n_f = GROUPS // F_STEP_GROUPS

    @pl.when((s != gate_block) & ((s < f_block) | (s >= f_block + n_f)))
    def _():
        scale = jnp.where((s >= q_blocks[0]) & (s < q_blocks[1]), DQK ** -0.5, 1.0)
        o_ref[0] = (w_ref[0] * scale).astype(BF16)

    for k in range(n_f):
        @pl.when(s == f_block + k)
        def _(k=k):
            outs = []
            for gg in range(F_STEP_GROUPS):
                a = w_ref[0, gg * GC:(gg + 1) * GC, :].astype(BF16)
                wc = jnp.dot(cst_ref[...], wf_ref[0, k * F_STEP_GROUPS + gg].astype(BF16),
                             preferred_element_type=F32)
                for half in range(2):
                    cw_t = wc[half * GC:(half + 1) * GC].T.astype(BF16)
                    outs.append(jnp.dot(cw_t, a, preferred_element_type=F32))
            o_ref[0] = jnp.concatenate(outs, axis=0).astype(BF16)

    @pl.when(s == gate_block)
    def _():
        g = g_ref[0]
        zero = jnp.zeros((LANES - UNITS, g.shape[1]), F32)
        tail = jnp.zeros((_WT_BLK - 2 * LANES, g.shape[1]), F32)
        o_ref[0] = jnp.concatenate([g[0:8], g[16:24], zero,
                                    g[8:16], g[24:32], zero,
                                    tail], axis=0).astype(BF16)


def _wprep(w_in, cst, w_fno):
    depth, d, p_in = w_in.shape
    w_t = jnp.swapaxes(w_in, 1, 2)
    starts, q_blocks, gate_block, f_block = _wprep_plan()
    assert _W_IN_SPLITS[3] % _GATE_ROWS == 0 and len(starts) * _WT_BLK == QK_W + _N_NAT
    grid_spec = pltpu.PrefetchScalarGridSpec(
        num_scalar_prefetch=1,
        grid=(depth, len(starts)),
        in_specs=[pl.BlockSpec((pl.Element(1), pl.Element(_WT_BLK), pl.Element(d)),
                               lambda l, s, st: (l, st[s] * 8, 0)),
                  pl.BlockSpec((1, _GATE_ROWS, d), lambda l, s, st: (l, int(_W_IN_SPLITS[3]) // _GATE_ROWS, 0)),
                  pl.BlockSpec((2 * GC, GC), lambda l, s, st: (0, 0)),
                  pl.BlockSpec((1, GROUPS, GC, GC), lambda l, s, st: (l, 0, 0, 0))],
        out_specs=pl.BlockSpec((1, _WT_BLK, d), lambda l, s, st: (l, s, 0)),
    )
    return pl.pallas_call(
        functools.partial(_wprep_kernel, q_blocks=q_blocks, gate_block=gate_block, f_block=f_block),
        grid_spec=grid_spec,
        out_shape=jax.ShapeDtypeStruct((depth, QK_W + _N_NAT, d), BF16),
        compiler_params=_cparams("parallel", "arbitrary"),
        name="wprep",
    )(jnp.asarray(starts, jnp.int32) // 8, w_t, w_t, cst, w_fno)


def _gate_bias(b_gate_l):
    bpad = jnp.zeros((LANES - UNITS,), b_gate_l.dtype)
    b_i = jnp.concatenate([b_gate_l[0:8], b_gate_l[16:24], bpad]).reshape(1, LANES)
    b_f = jnp.concatenate([b_gate_l[8:16], b_gate_l[24:32], bpad]).reshape(1, LANES)
    return b_i, b_f


def kernel(x, c, ctx, c_ctx, w_mod, b_mod, g_pre, g_post, w_in, b_gate, g_hnorm, g_sgu, w_sp, b_sp, w_fno, b_fno, w_out):
    bsz, t_lat, d = x.shape
    t_ctx = ctx.shape[1]
    depth = w_mod.shape[0]
    assert d == D_MODEL and t_lat % (2 * CHUNK) == 0 and t_ctx % (2 * CHUNK) == 0 and bsz + 1 <= 8

    cc = jnp.concatenate([c, c_ctx[None, :], jnp.zeros((8 - bsz - 1, d), c.dtype)], axis=0)
    mod = _modulation(cc, w_mod, b_mod)
    consts_lat, consts_ctx = _dft_consts(t_lat), _dft_consts(t_ctx)
    w_t_all = _wprep(w_in, jnp.asarray(consts_lat["cs"]).astype(BF16), w_fno)

    def fourier(xr, xi, consts, bfno):
        if "dense" in consts:
            return _fourier_dense(xr, xi, jnp.asarray(consts["dense"]).astype(BF16), bfno)
        return _fourier_fused(xr, xi, jnp.asarray(consts["f1"]).astype(BF16),
                              jnp.asarray(consts["mtab"]).astype(BF16), bfno, n1=CHUNK)

    cn_zero = jnp.zeros((bsz, UNITS, DQK, 2 * DV), F32)
    m_zero = jnp.zeros((bsz, UNITS, LANES), F32)
    xc = ctx
    for l in range(depth):
        sh_l, sc_l, gt_l = (mod[l, :bsz, i * d:(i + 1) * d].reshape(bsz, 1, d) for i in range(3))
        sh_c, sc_c, gt_c = (jnp.broadcast_to(mod[l, bsz, i * d:(i + 1) * d].reshape(1, 1, d), (bsz, 1, d))
                            for i in range(3))
        b_i, b_f = _gate_bias(b_gate[l])
        gpre = g_pre[l].reshape(1, d)
        wsp = w_sp[l].astype(BF16)
        bsp = jnp.broadcast_to(b_sp[l].T[:, :, None], (CHUNK, GROUPS, GC)).reshape(CHUNK, S_WIDTH)
        wo = w_out[l].astype(BF16)
        tail = (wo, g_hnorm[l].reshape(1, M_WIDTH), g_sgu[l].reshape(1, S_WIDTH), g_post[l].reshape(1, d), wsp, bsp)
        front = functools.partial(_inproj, g_pre=gpre, w_t=w_t_all, b_i=b_i, b_f=b_f, layer=l)

        last = l == depth - 1
        pc = front(xc, sc_c, sh_c, full=not last, tm=256)
        hf_c, hb_c, cn_c, m_c = _mlstm(pc["q"], pc["kT"], pc["v"], pc["cola"], pc["colb"], pc["rowa"], cn_zero, m_zero)
        p = front(x, sc_l, sh_l, full=True, tm=512)
        hf, hb, _, _ = _mlstm(p["q"], p["kT"], p["v"], p["cola"], p["colb"], p["rowa"], cn_c, m_c)
        ym = fourier(p["xr"], p["xi"], consts_lat, b_fno[l])
        x = _outproj(hf, hb, p["ga"], p["vs"], p["gb"], ym, p["szc"], x, gt_l, *tail, tm=512)
        if not last:
            ymc = fourier(pc["xr"], pc["xi"], consts_ctx, b_fno[l])
            xc = _outproj(hf_c, hb_c, pc["ga"], pc["vs"], pc["gb"], ymc, pc["szc"], xc, gt_c, *tail, tm=256)
    return x
```

```python
import functools

import numpy as np
import jax
import jax.numpy as jnp
from jax.experimental import pallas as pl
from jax.experimental.pallas import tpu as pltpu

EPS = 1e-6
LOG2E = 1.4426950408889634
D_MODEL = 1024
HEADS = 8
DV = 128
DQK = 64
QK_W = HEADS * DQK
M_WIDTH = HEADS * DV
CHUNK = 128
S_WIDTH = 512
F_WIDTH = 512
GROUPS = 4
GC = 128
UNITS = 2 * HEADS
LANES = 128

VMEM_LIMIT = 56 * 1024 * 1024

F32 = jnp.float32
BF16 = jnp.bfloat16


def _cparams(*sem):
    return pltpu.CompilerParams(dimension_semantics=sem, vmem_limit_bytes=VMEM_LIMIT)


def _sigmoid(x):
    return 1.0 / (1.0 + jnp.exp(-x))


def _silu(x):
    return x * _sigmoid(x)


def _mod_kernel(cc_ref, w_ref, b_ref, o_ref):
    s = _silu(cc_ref[...]).astype(BF16)
    w = w_ref[0].astype(BF16)
    o_ref[0] = jnp.dot(s, w, preferred_element_type=F32) + b_ref[0]


def _modulation(cc, w_mod, b_mod):
    depth, d, d3 = w_mod.shape
    return pl.pallas_call(
        _mod_kernel,
        grid=(depth,),
        in_specs=[pl.BlockSpec((8, d), lambda l: (0, 0)),
                  pl.BlockSpec((1, d, d3), lambda l: (l, 0, 0)),
                  pl.BlockSpec((1, 1, d3), lambda l: (l, 0, 0))],
        out_specs=pl.BlockSpec((1, 8, d3), lambda l: (l, 0, 0)),
        out_shape=jax.ShapeDtypeStruct((depth, 8, d3), F32),
        compiler_params=_cparams("arbitrary"),
        name="modulation",
    )(cc, w_mod, b_mod.reshape(depth, 1, d3))


def _log_sigmoid(x):
    return jnp.minimum(x, 0.0) - jnp.log1p(jnp.exp(-jnp.abs(x)))


def _gate_scans(g_t, bi_ref, bf_ref, cola_ref, colb_ref, rowa_ref):
    tm = g_t.shape[1]
    n_c = tm // CHUNK
    assert n_c * UNITS <= LANES
    row = jax.lax.broadcasted_iota(jnp.int32, (CHUNK, LANES), 0)
    lane = jax.lax.broadcasted_iota(jnp.int32, (CHUNK, LANES), 1)
    fwd = lane % UNITS < HEADS
    gi = jnp.zeros((CHUNK, LANES), F32)
    gf = jnp.zeros((CHUNK, LANES), F32)
    pad = jnp.zeros((CHUNK - 2 * UNITS, CHUNK), F32)
    for c in range(n_c):
        t_c = jnp.concatenate([g_t[:, c * CHUNK:(c + 1) * CHUNK], pad], axis=0).T
        sel = lane // UNITS == c
        gi = jnp.where(sel, pltpu.roll(t_c, c * UNITS, axis=1) if c else t_c, gi)
        gf = jnp.where(sel, pltpu.roll(t_c, (c * UNITS - UNITS) % LANES, axis=1), gf)
    gi = gi + bi_ref[...]
    lf = _log_sigmoid(gf + bf_ref[...])

    def scan(x, op):
        pre, suf = x, x
        k = 1
        while k < CHUNK:
            sh = pltpu.roll(pre, k, axis=0)
            pre = jnp.where(row >= k, op(pre, sh), pre)
            sh = pltpu.roll(suf, CHUNK - k, axis=0)
            suf = jnp.where(row < CHUNK - k, op(suf, sh), suf)
            k *= 2
        return jnp.where(fwd, pre, suf)

    b = scan(lf, jnp.add)
    a = (gi - b) * LOG2E
    amax = scan(a, jnp.maximum)
    b = b * LOG2E
    a_t = a.T
    for c in range(n_c):
        rows = slice(c * CHUNK, (c + 1) * CHUNK)
        back = (LANES - c * UNITS) % LANES
        cola_ref[0, rows, :] = pltpu.roll(amax, back, axis=1) if c else amax
        colb_ref[0, rows, :] = pltpu.roll(b, back, axis=1) if c else b
        rowa_ref[0, :, rows] = a_t[c * UNITS:(c + 1) * UNITS, :]


def _pitch(n):
    p = n
    while (p // 8) % 2 == 0:
        p += 8
    return p


def _store_slabs(ref, g, z, n2):
    pin = _pitch(n2)
    for r in range(z.shape[0] // n2):
        ref[0, g, r * pin:r * pin + n2, :] = z[r * n2:(r + 1) * n2]
        if pin > n2:
            ref[0, g, r * pin + n2:(r + 1) * pin, :] = jnp.zeros((pin - n2, GC), F32)


def _inproj_kernel(x_ref, sc_ref, sh_ref, g_ref, wt_ref, bi_ref, bf_ref, *out_refs, pieces, slab_n2):
    nt = (((1,), (1,)), ((), ()))
    x = x_ref[0]
    y = x * jax.lax.rsqrt(jnp.mean(x * x, axis=-1, keepdims=True) + EPS)
    h = y * (g_ref[...] * (1.0 + sc_ref[0])) + sh_ref[0]
    hb = h.astype(BF16)

    def silu(z):
        t = z * 0.5
        return t + t * jnp.tanh(t)

    held = {}
    oi = 0
    for name, off, width in pieces:
        if name == "gates":
            gate_rows = QK_W + off
            w_kg = jnp.concatenate([wt_ref[0:QK_W, :], wt_ref[gate_rows:gate_rows + UNITS, :],
                                    wt_ref[gate_rows + LANES:gate_rows + LANES + UNITS, :]], axis=0)
            r = jax.lax.dot_general(w_kg, hb, nt, preferred_element_type=F32)
            out_refs[-1][0] = r[:QK_W].astype(BF16)
            _gate_scans(r[QK_W:], bi_ref, bf_ref, *out_refs[oi:oi + 3])
            oi += 3
            continue
        if name == "kT":
            continue
        r = jax.lax.dot_general(hb, wt_ref[QK_W + off:QK_W + off + width, :], nt, preferred_element_type=F32)
        if name == "f":
            xr_ref, xi_ref = out_refs[oi], out_refs[oi + 1]
            for g in range(GROUPS):
                zr, zi = r[:, 2 * g * GC:(2 * g + 1) * GC], r[:, (2 * g + 1) * GC:(2 * g + 2) * GC]
                if slab_n2 is None:
                    xr_ref[0, :, g * GC:(g + 1) * GC] = zr.astype(BF16)
                    xi_ref[0, :, g * GC:(g + 1) * GC] = zi.astype(BF16)
                else:
                    _store_slabs(xr_ref, g, zr, slab_n2)
                    _store_slabs(xi_ref, g, zi, slab_n2)
            oi += 2
        elif name in _HELD:
            held[name] = r
        else:
            if name == "za":
                r = (0.5 + 0.5 * jnp.tanh(held["o"] * 0.5)) * silu(r)
            elif name == "zb":
                r = held["u"] * silu(r)
            elif name == "zc":
                r = silu(r)
            out_refs[oi][0] = r.astype(BF16)
            oi += 1


_FULL_PIECES = (("q", QK_W), ("v", M_WIDTH), ("gates", 2 * LANES), ("o", M_WIDTH), ("za", M_WIDTH),
                ("u", S_WIDTH), ("vs", S_WIDTH), ("zb", S_WIDTH), ("f", 2 * F_WIDTH), ("zc", F_WIDTH))
_STATE_PIECES = _FULL_PIECES[:3]
_HELD = ("o", "u")
_GATED_NAME = {"za": "ga", "zb": "gb", "zc": "szc"}
_WT_BLK = 512


def _wrows(width):
    return -(-width // _WT_BLK) * _WT_BLK


def _inproj(x, sc, sh, g_pre, w_t, b_i, b_f, *, layer, full, tm):
    bsz, t, d = x.shape
    names = _FULL_PIECES if full else _STATE_PIECES
    slab_n2 = t // CHUNK if t > 2 * CHUNK else None
    row_spec = lambda width: pl.BlockSpec((1, tm, width), lambda b, i: (b, i, 0))
    pieces, off = [], 0
    out_names, out_shapes, out_specs = [], [], []
    for name, width in names:
        pieces.append((name, off, width))
        off += _wrows(width)
        if name == "gates":
            out_names += ["cola", "colb", "rowa"]
            out_shapes += [jax.ShapeDtypeStruct((bsz, t, LANES), F32)] * 2 + [jax.ShapeDtypeStruct((bsz, UNITS, t), F32)]
            out_specs += [row_spec(LANES), row_spec(LANES), pl.BlockSpec((1, UNITS, tm), lambda b, i: (b, 0, i))]
        elif name == "f":
            out_names += ["xr", "xi"]
            if slab_n2 is None:
                out_shapes += [jax.ShapeDtypeStruct((bsz, t, F_WIDTH), BF16)] * 2
                out_specs += [row_spec(F_WIDTH)] * 2
            else:
                assert tm % slab_n2 == 0
                pin = _pitch(slab_n2)
                out_shapes += [jax.ShapeDtypeStruct((bsz, GROUPS, CHUNK * pin, GC), F32)] * 2
                out_specs += [pl.BlockSpec((1, GROUPS, tm // slab_n2 * pin, GC), lambda b, i: (b, 0, i, 0))] * 2
        elif name not in _HELD:
            out_names.append(_GATED_NAME.get(name, name))
            out_shapes.append(jax.ShapeDtypeStruct((bsz, t, width), BF16))
            out_specs.append(row_spec(width))
    n_nat = off
    pieces.append(("kT", 0, QK_W))
    out_names.append("kT")
    out_shapes.append(jax.ShapeDtypeStruct((bsz, QK_W, t), BF16))
    out_specs.append(pl.BlockSpec((1, QK_W, tm), lambda b, i: (b, 0, i)))
    const = lambda shape, **kw: pl.BlockSpec(shape, lambda b, i: (0,) * len(shape), **kw)
    outs = pl.pallas_call(
        functools.partial(_inproj_kernel, pieces=tuple(pieces), slab_n2=slab_n2),
        grid=(bsz, t // tm),
        in_specs=[pl.BlockSpec((1, tm, d), lambda b, i: (b, i, 0)),
                  pl.BlockSpec((1, 1, d), lambda b, i: (b, 0, 0)),
                  pl.BlockSpec((1, 1, d), lambda b, i: (b, 0, 0)),
                  const((1, d)),
                  pl.BlockSpec((None, QK_W + n_nat, d), lambda b, i: (layer, 0, 0), pipeline_mode=pl.Buffered(1)),
                  const((1, LANES)), const((1, LANES))],
        out_specs=out_specs,
        out_shape=out_shapes,
        compiler_params=_cparams("parallel", "parallel"),
        name="inproj_full" if full else "inproj_state",
    )(x, sc, sh, g_pre, w_t, b_i, b_f)
    return dict(zip(out_names, outs))


def _mlstm_unit(h, d, sub, q_ref, kt_ref, v_ref, rowa_ref, m_diag, g_cols, c_cols, g_ends, hout_ref, cn_ref, mask):
    i = d * HEADS + h
    pair = h // 2
    rows = slice(sub * CHUNK, (sub + 1) * CHUNK)
    q_pair = q_ref[0, rows, pair * LANES:(pair + 1) * LANES]
    kt_h = kt_ref[0, h * DQK:(h + 1) * DQK, rows]
    zk = jnp.zeros((DQK, CHUNK), BF16)
    kt_ext = jnp.concatenate([kt_h, zk] if h % 2 == 0 else [zk, kt_h], axis=0)
    v_h = v_ref[0, rows, h * DV:(h + 1) * DV]
    vaug = jnp.concatenate([v_h, jnp.ones((CHUNK, DV), BF16)], axis=1)
    g = jnp.broadcast_to(g_cols[:, i:i + 1], (CHUNK, CHUNK))
    c = jnp.broadcast_to(c_cols[:, i:i + 1], (CHUNK, CHUNK))
    g_end = jnp.broadcast_to(g_ends[:, i:i + 1], (1, LANES))
    m_row = jnp.broadcast_to(m_diag[:, i:i + 1], (1, LANES))
    a_row = rowa_ref[0, h:h + 1, rows]
    cn = cn_ref[0, i]

    dmat = jnp.where(mask, jnp.exp2(a_row - g), 0.0)
    s = jnp.dot(q_pair, kt_ext, preferred_element_type=F32)
    p = (s * dmat).astype(BF16)
    qs = q_pair * jnp.exp2(m_row - g).astype(BF16)
    lhs = jnp.concatenate([p, qs], axis=1)
    zc = jnp.zeros((DQK, 2 * DV), BF16)
    cnb = cn.astype(BF16)
    cn_ext = jnp.concatenate([cnb, zc] if h % 2 == 0 else [zc, cnb], axis=0)
    rhs = jnp.concatenate([vaug, cn_ext], axis=0)
    out = jnp.dot(lhs, rhs, preferred_element_type=F32)
    num, den = out[:, :DV], out[:, DV:]
    hval = num / jnp.maximum(jnp.abs(den), jnp.exp2(c))
    hout_ref[0, rows, h * DV:(h + 1) * DV] = hval.astype(hout_ref.dtype)

    kts = kt_h * jnp.exp2(a_row - g_end).astype(BF16)
    upd = jnp.dot(kts, vaug, preferred_element_type=F32)
    decay = jnp.exp2(m_row - g_end)
    cn_ref[0, i] = jnp.concatenate([decay, decay], axis=1) * cn + upd


MLSTM_SUB = 8


def _mlstm_kernel(qf_ref, ktf_ref, vf_ref, caf_ref, cbf_ref, raf_ref,
                  qb_ref, ktb_ref, vb_ref, cab_ref, cbb_ref, rab_ref,
                  cn0_ref, m0_ref, hf_ref, hb_ref, cn_ref, m_ref, *, n_sub):
    @pl.when(pl.program_id(1) == 0)
    def _():
        cn_ref[...] = cn0_ref[...]
        m_ref[...] = m0_ref[...]

    t_idx = jax.lax.broadcasted_iota(jnp.int32, (CHUNK, CHUNK), 0)
    s_idx = jax.lax.broadcasted_iota(jnp.int32, (CHUNK, CHUNK), 1)
    unit_row = jax.lax.broadcasted_iota(jnp.int32, (UNITS, LANES), 0)
    unit_lane = jax.lax.broadcasted_iota(jnp.int32, (UNITS, LANES), 1)
    m_diag = jnp.sum(jnp.where(unit_row == unit_lane, m_ref[0], 0.0), axis=0, keepdims=True)
    fwd_lane = jax.lax.broadcasted_iota(jnp.int32, (1, LANES), 1) < HEADS

    def columns(ca_ref, cb_ref, sub, end):
        rows = slice(sub * CHUNK, (sub + 1) * CHUNK)
        g_cols = jnp.maximum(m_diag, ca_ref[0, rows, :])
        g_ends = g_cols[end:end + 1, :]
        m_new = cb_ref[0, sub * CHUNK + end:sub * CHUNK + end + 1, :] + g_ends
        return (g_cols, -(cb_ref[0, rows, :] + g_cols), g_ends), m_new

    for step in range(n_sub):
        sub_f, sub_b = step, n_sub - 1 - step
        cols_f, m_new_f = columns(caf_ref, cbf_ref, sub_f, CHUNK - 1)
        cols_b, m_new_b = columns(cab_ref, cbb_ref, sub_b, 0)
        for h in range(HEADS):
            _mlstm_unit(h, 0, sub_f, qf_ref, ktf_ref, vf_ref, raf_ref, m_diag, *cols_f, hf_ref, cn_ref, s_idx <= t_idx)
            _mlstm_unit(h, 1, sub_b, qb_ref, ktb_ref, vb_ref, rab_ref, m_diag, *cols_b, hb_ref, cn_ref, s_idx >= t_idx)
        m_diag = jnp.where(fwd_lane, m_new_f, m_new_b)
    m_ref[0] = jnp.where(unit_row == unit_lane, jnp.broadcast_to(m_diag, (UNITS, LANES)), 0.0)


def _mlstm(q, kt, v, cola, colb, rowa, cn0, m0):
    bsz, t, _ = q.shape
    n_sub = min(MLSTM_SUB, t // CHUNK)
    blk = n_sub * CHUNK
    nc = t // blk

    def specs(rev):
        cj = (lambda j: nc - 1 - j) if rev else (lambda j: j)
        d = 1 if rev else 0
        return [pl.BlockSpec((1, blk, QK_W), lambda b, j: (b, cj(j), 0)),
                pl.BlockSpec((1, QK_W, blk), lambda b, j: (b, 0, cj(j))),
                pl.BlockSpec((1, blk, M_WIDTH), lambda b, j: (b, cj(j), 0)),
                pl.BlockSpec((1, blk, LANES), lambda b, j: (b, cj(j), 0)),
                pl.BlockSpec((1, blk, LANES), lambda b, j: (b, cj(j), 0)),
                pl.BlockSpec((1, HEADS, blk), lambda b, j: (b, d, cj(j)))]

    cn_spec = pl.BlockSpec((1, UNITS, DQK, 2 * DV), lambda b, j: (b, 0, 0, 0))
    m_spec = pl.BlockSpec((1, UNITS, LANES), lambda b, j: (b, 0, 0))
    args = (q, kt, v, cola, colb, rowa)
    return pl.pallas_call(
        functools.partial(_mlstm_kernel, n_sub=n_sub),
        grid=(bsz, nc),
        in_specs=specs(False) + specs(True) + [cn_spec, m_spec],
        out_specs=[pl.BlockSpec((1, blk, M_WIDTH), lambda b, j: (b, j, 0)),
                   pl.BlockSpec((1, blk, M_WIDTH), lambda b, j: (b, nc - 1 - j, 0)),
                   cn_spec, m_spec],
        out_shape=[jax.ShapeDtypeStruct((bsz, t, M_WIDTH), BF16),
                   jax.ShapeDtypeStruct((bsz, t, M_WIDTH), BF16),
                   jax.ShapeDtypeStruct((bsz, UNITS, DQK, 2 * DV), F32),
                   jax.ShapeDtypeStruct((bsz, UNITS, LANES), F32)],
        compiler_params=_cparams("parallel", "arbitrary"),
        name="mlstm",
    )(*args, *args, cn0, m0)


OUT_PITCH = _pitch(CHUNK)


def _fourier_dense_kernel(xr_ref, xi_ref, cst_ref, bf_ref, o_ref):
    t = xr_ref.shape[1]
    xx = jnp.concatenate([xr_ref[0], xi_ref[0]], axis=0)
    yr = jnp.dot(cst_ref[...], xx, preferred_element_type=F32)
    for g in range(GROUPS):
        for c in range(t // CHUNK):
            o_ref[0, g, c * OUT_PITCH:c * OUT_PITCH + CHUNK, :] = (
                yr[c * CHUNK:(c + 1) * CHUNK, g * GC:(g + 1) * GC] + bf_ref[g])
            o_ref[0, g, c * OUT_PITCH + CHUNK:(c + 1) * OUT_PITCH, :] = jnp.zeros((OUT_PITCH - CHUNK, GC), F32)


def _fourier_dense(xr, xi, cst, bfno):
    bsz, t, w = xr.shape
    blk = pl.BlockSpec((1, t, w), lambda b: (b, 0, 0))
    rows = t // CHUNK * OUT_PITCH
    return pl.pallas_call(
        _fourier_dense_kernel,
        grid=(bsz,),
        in_specs=[blk, blk,
                  pl.BlockSpec((t, 2 * t), lambda b: (0, 0)),
                  pl.BlockSpec((GROUPS, 1, GC), lambda b: (0, 0, 0))],
        out_specs=pl.BlockSpec((1, GROUPS, rows, GC), lambda b: (b, 0, 0, 0)),
        out_shape=jax.ShapeDtypeStruct((bsz, GROUPS, rows, GC), F32),
        compiler_params=_cparams("parallel"),
        name="fourier_dense",
    )(xr, xi, cst, bfno.reshape(GROUPS, 1, GC))


def _fourier_fused_kernel(xr_ref, xi_ref, f1_ref, m_ref, bf_ref, o_ref, gr_scr, gi_scr, *, n1, n2):
    pin = _pitch(n2)

    def stage1(j, carry):
        t2 = 2 * j

        def ld(ref, s):
            return ref[0, 0, pl.ds(s, n1, stride=pin), :].astype(BF16)

        top = jnp.concatenate([ld(xr_ref, t2), ld(xr_ref, t2 + 1)], axis=1)
        bot = jnp.concatenate([ld(xi_ref, t2), ld(xi_ref, t2 + 1)], axis=1)
        g = jnp.dot(f1_ref[...], jnp.concatenate([top, bot], axis=0), preferred_element_type=F32)
        gr_scr[pl.ds(t2, n1, stride=pin), :] = g[:n1, :GC]
        gr_scr[pl.ds(t2 + 1, n1, stride=pin), :] = g[:n1, GC:]
        gi_scr[pl.ds(t2, n1, stride=pin), :] = g[n1:, :GC]
        gi_scr[pl.ds(t2 + 1, n1, stride=pin), :] = g[n1:, GC:]
        return carry

    jax.lax.fori_loop(0, n2 // 2, stage1, 0, unroll=True)

    def stage2(k1, carry):
        base = pl.multiple_of(k1 * pin, 8)
        gg = jnp.concatenate([gr_scr[pl.ds(base, n2), :], gi_scr[pl.ds(base, n2), :]], axis=0).astype(BF16)
        yr = jnp.dot(m_ref[k1], gg, preferred_element_type=F32) + bf_ref[0]
        o_ref[0, 0, pl.ds(k1, n2, stride=OUT_PITCH), :] = yr
        return carry

    jax.lax.fori_loop(0, n1, stage2, 0, unroll=True)
    for k2 in range(n2):
        o_ref[0, 0, k2 * OUT_PITCH + n1:(k2 + 1) * OUT_PITCH, :] = jnp.zeros((OUT_PITCH - n1, GC), F32)


def _fourier_fused(xr, xi, f1, mtab, bfno, *, n1):
    bsz, _, rows_in, _ = xr.shape
    n2 = mtab.shape[1]
    pin = _pitch(n2)
    assert rows_in == n1 * pin and n1 == CHUNK
    in_blk = pl.BlockSpec((1, 1, rows_in, GC), lambda b, g: (b, g, 0, 0))
    return pl.pallas_call(
        functools.partial(_fourier_fused_kernel, n1=n1, n2=n2),
        grid=(bsz, GROUPS),
        in_specs=[in_blk, in_blk,
                  pl.BlockSpec((2 * n1, 2 * n1), lambda b, g: (0, 0)),
                  pl.BlockSpec((n1, n2, 2 * n2), lambda b, g: (0, 0, 0)),
                  pl.BlockSpec((1, 1, GC), lambda b, g: (g, 0, 0))],
        out_specs=pl.BlockSpec((1, 1, n2 * OUT_PITCH, GC), lambda b, g: (b, g, 0, 0)),
        out_shape=jax.ShapeDtypeStruct((bsz, GROUPS, n2 * OUT_PITCH, GC), F32),
        scratch_shapes=[pltpu.VMEM((n1 * pin, GC), F32), pltpu.VMEM((n1 * pin, GC), F32)],
        compiler_params=_cparams("parallel", "parallel"),
        name="fourier_fused",
    )(xr, xi, f1, mtab, bfno.reshape(GROUPS, 1, GC))


def _dft_consts(t):
    ang = 2.0 * np.pi / GC * np.outer(np.arange(GC), np.arange(GC))
    cs = np.concatenate([np.cos(ang), -np.sin(ang)], axis=0) / np.sqrt(GC)
    out = {"cs": cs.astype(np.float32)}
    if t <= 2 * CHUNK:
        ang = 2.0 * np.pi / t * np.mod(np.outer(np.arange(t), np.arange(t)), t)
        out["dense"] = (np.concatenate([np.cos(ang), np.sin(ang)], axis=1) / np.sqrt(t)).astype(np.float32)
    else:
        n1 = CHUNK
        n2 = t // n1
        ang = 2.0 * np.pi / n1 * np.mod(np.outer(np.arange(n1), np.arange(n1)), n1)
        c, s = np.cos(ang), np.sin(ang)
        out["f1"] = (np.block([[c, s], [-s, c]]) / np.sqrt(n1)).astype(np.float32)
        k = np.arange(n1)[:, None, None] + n1 * np.arange(n2)[None, :, None]
        ang = 2.0 * np.pi / t * np.mod(k * np.arange(n2)[None, None, :], t)
        out["mtab"] = (np.concatenate([np.cos(ang), np.sin(ang)], axis=2) / np.sqrt(n2)).astype(np.float32)
    return out


def _outproj_kernel(hf_ref, hb_ref, ga_ref, vs_ref, gb_ref, ym_ref, szc_ref, x_ref, gt_ref,
                    wout_ref, ghn_ref, gsgu_ref, gpost_ref, wsp_ref, bsp_ref, xo_ref, y_scr):
    tm = x_ref.shape[1]
    kc = S_WIDTH

    def project(k):
        return jnp.dot(y_scr[:, k * kc:(k + 1) * kc], wout_ref[k * kc:(k + 1) * kc, :], preferred_element_type=F32)

    out = None
    for h in range(HEADS):
        sl = slice(h * DV, (h + 1) * DV)
        hh = (hf_ref[0, :, sl] + hb_ref[0, :, sl]).astype(F32)
        hn = hh * jax.lax.rsqrt(jnp.mean(hh * hh, axis=-1, keepdims=True) + EPS) * ghn_ref[:, sl]
        y_scr[:, sl] = hn.astype(BF16) * ga_ref[0, :, sl]
        if (h + 1) * DV % kc == 0:
            part = project((h + 1) * DV // kc - 1)
            out = part if out is None else out + part
    vs = vs_ref[0].astype(F32)
    vn = (vs * jax.lax.rsqrt(jnp.mean(vs * vs, axis=-1, keepdims=True) + EPS) * gsgu_ref[...]).astype(BF16)
    for c in range(tm // CHUNK):
        rows = slice(c * CHUNK, (c + 1) * CHUNK)
        for g in range(GROUPS):
            cols = slice(g * GC, (g + 1) * GC)
            mixed = jnp.dot(wsp_ref[g], vn[rows, cols], preferred_element_type=F32) + bsp_ref[:, cols]
            y_scr[rows, M_WIDTH + g * GC:M_WIDTH + (g + 1) * GC] = gb_ref[0, rows, cols] * mixed.astype(BF16)
    out = out + project(M_WIDTH // kc)
    for c in range(tm // CHUNK):
        rows = slice(c * CHUNK, (c + 1) * CHUNK)
        for g in range(GROUPS):
            cols = slice(g * GC, (g + 1) * GC)
            y_scr[rows, M_WIDTH + S_WIDTH + g * GC:M_WIDTH + S_WIDTH + (g + 1) * GC] = (
                ym_ref[0, g, c * OUT_PITCH:c * OUT_PITCH + CHUNK, :].astype(BF16) * szc_ref[0, rows, cols])
    out = out + project((M_WIDTH + S_WIDTH) // kc)
    on = out * jax.lax.rsqrt(jnp.mean(out * out, axis=-1, keepdims=True) + EPS) * gpost_ref[...]
    xo_ref[0] = x_ref[0] + gt_ref[0] * on


def _outproj(hf, hb, ga, vs, gb, ym, szc, x, gt, w_out, ghn, gsgu, gpost, wsp, bsp, *, tm):
    bsz, t, d = x.shape
    wide = pl.BlockSpec((1, tm, M_WIDTH), lambda b, i: (b, i, 0))
    half = pl.BlockSpec((1, tm, S_WIDTH), lambda b, i: (b, i, 0))
    const2 = lambda shape: pl.BlockSpec(shape, lambda b, i: (0,) * len(shape))
    return pl.pallas_call(
        _outproj_kernel,
        grid=(bsz, t // tm),
        in_specs=[wide, wide, wide, half, half,
                  pl.BlockSpec((1, GROUPS, tm // CHUNK * OUT_PITCH, GC), lambda b, i: (b, 0, i, 0)),
                  half, wide,
                  pl.BlockSpec((1, 1, d), lambda b, i: (b, 0, 0)),
                  const2((2 * D_MODEL, d)), const2((1, M_WIDTH)), const2((1, S_WIDTH)), const2((1, d)),
                  const2((GROUPS, CHUNK, CHUNK)), const2((CHUNK, S_WIDTH))],
        out_specs=wide,
        out_shape=jax.ShapeDtypeStruct((bsz, t, d), F32),
        scratch_shapes=[pltpu.VMEM((tm, 2 * D_MODEL), BF16)],
        compiler_params=_cparams("parallel", "parallel"),
        name="outproj",
    )(hf, hb, ga, vs, gb, ym, szc, x, gt, w_out, ghn, gsgu, gpost, wsp, bsp)


_W_IN_SPLITS = np.cumsum([0, QK_W, QK_W, M_WIDTH, 4 * HEADS, M_WIDTH, M_WIDTH, S_WIDTH, S_WIDTH, S_WIDTH, F_WIDTH,
                          F_WIDTH])
_N_NAT = sum(_wrows(w) for _, w in _FULL_PIECES)
_GATE_ROWS = 4 * HEADS


def _wprep_plan():
    c = _W_IN_SPLITS
    starts, q_blocks, gate_block, f_block = [], None, None, None
    for i in (1, 0, 2, 3, 4, 5, 6, 7, 8, 9, 10):
        if i == 3:
            gate_block = len(starts)
            starts.append(0)
            continue
        first = len(starts)
        if i == 9:
            f_block = first
            starts += list(range(int(c[i]), int(c[i + 1]), _WT_BLK // 2))
            continue
        starts += list(range(int(c[i]), int(c[i + 1]), _WT_BLK))
        if i == 0:
            q_blocks = (first, len(starts))
    return starts, q_blocks, gate_block, f_block


F_STEP_GROUPS = _WT_BLK // (2 * GC)


def _wprep_kernel(starts_ref, w_ref, g_ref, cst_ref, wf_ref, o_ref, *, q_blocks, gate_block, f_block):
    del starts_ref
    s = pl.program_id(1)
    n_f = GROUPS // F_STEP_GROUPS

    @pl.when((s != gate_block) & ((s < f_block) | (s >= f_block + n_f)))
    def _():
        scale = jnp.where((s >= q_blocks[0]) & (s < q_blocks[1]), DQK ** -0.5, 1.0)
        o_ref[0] = (w_ref[0] * scale).astype(BF16)

    for k in range(n_f):
        @pl.when(s == f_block + k)
        def _(k=k):
            outs = []
            for gg in range(F_STEP_GROUPS):
                a = w_ref[0, gg * GC:(gg + 1) * GC, :].astype(BF16)
                wc = jnp.dot(cst_ref[...], wf_ref[0, k * F_STEP_GROUPS + gg].astype(BF16),
                             preferred_element_type=F32)
                for half in range(2):
                    cw_t = wc[half * GC:(half + 1) * GC].T.astype(BF16)
                    outs.append(jnp.dot(cw_t, a, preferred_element_type=F32))
            o_ref[0] = jnp.concatenate(outs, axis=0).astype(BF16)

    @pl.when(s == gate_block)
    def _():
        g = g_ref[0]
        zero = jnp.zeros((LANES - UNITS, g.shape[1]), F32)
        tail = jnp.zeros((_WT_BLK - 2 * LANES, g.shape[1]), F32)
        o_ref[0] = jnp.concatenate([g[0:8], g[16:24], zero,
                                    g[8:16], g[24:32], zero,
                                    tail], axis=0).astype(BF16)


def _wprep(w_in, cst, w_fno):
    depth, d, p_in = w_in.shape
    w_t = jnp.swapaxes(w_in, 1, 2)
    starts, q_blocks, gate_block, f_block = _wprep_plan()
    assert _W_IN_SPLITS[3] % _GATE_ROWS == 0 and len(starts) * _WT_BLK == QK_W + _N_NAT
    grid_spec = pltpu.PrefetchScalarGridSpec(
        num_scalar_prefetch=1,
        grid=(depth, len(starts)),
        in_specs=[pl.BlockSpec((pl.Element(1), pl.Element(_WT_BLK), pl.Element(d)),
                               lambda l, s, st: (l, st[s] * 8, 0)),
                  pl.BlockSpec((1, _GATE_ROWS, d), lambda l, s, st: (l, int(_W_IN_SPLITS[3]) // _GATE_ROWS, 0)),
                  pl.BlockSpec((2 * GC, GC), lambda l, s, st: (0, 0)),
                  pl.BlockSpec((1, GROUPS, GC, GC), lambda l, s, st: (l, 0, 0, 0))],
        out_specs=pl.BlockSpec((1, _WT_BLK, d), lambda l, s, st: (l, s, 0)),
    )
    return pl.pallas_call(
        functools.partial(_wprep_kernel, q_blocks=q_blocks, gate_block=gate_block, f_block=f_block),
        grid_spec=grid_spec,
        out_shape=jax.ShapeDtypeStruct((depth, QK_W + _N_NAT, d), BF16),
        compiler_params=_cparams("parallel", "arbitrary"),
        name="wprep",
    )(jnp.asarray(starts, jnp.int32) // 8, w_t, w_t, cst, w_fno)


def _gate_bias(b_gate_l):
    b_i = jnp.tile(jnp.concatenate([b_gate_l[0:8], b_gate_l[16:24]]), LANES // UNITS).reshape(1, LANES)
    b_f = jnp.tile(jnp.concatenate([b_gate_l[8:16], b_gate_l[24:32]]), LANES // UNITS).reshape(1, LANES)
    return b_i, b_f


def kernel(x, c, ctx, c_ctx, w_mod, b_mod, g_pre, g_post, w_in, b_gate, g_hnorm, g_sgu, w_sp, b_sp, w_fno, b_fno, w_out):
    bsz, t_lat, d = x.shape
    t_ctx = ctx.shape[1]
    depth = w_mod.shape[0]
    assert d == D_MODEL and t_lat % (2 * CHUNK) == 0 and t_ctx % (2 * CHUNK) == 0 and bsz + 1 <= 8

    cc = jnp.concatenate([c, c_ctx[None, :], jnp.zeros((8 - bsz - 1, d), c.dtype)], axis=0)
    mod = _modulation(cc, w_mod, b_mod)
    consts_lat, consts_ctx = _dft_consts(t_lat), _dft_consts(t_ctx)
    w_t_all = _wprep(w_in, jnp.asarray(consts_lat["cs"]).astype(BF16), w_fno)

    def fourier(xr, xi, consts, bfno):
        if "dense" in consts:
            return _fourier_dense(xr, xi, jnp.asarray(consts["dense"]).astype(BF16), bfno)
        return _fourier_fused(xr, xi, jnp.asarray(consts["f1"]).astype(BF16),
                              jnp.asarray(consts["mtab"]).astype(BF16), bfno, n1=CHUNK)

    cn_zero = jnp.zeros((bsz, UNITS, DQK, 2 * DV), F32)
    m_zero = jnp.zeros((bsz, UNITS, LANES), F32)
    xc = ctx
    for l in range(depth):
        sh_l, sc_l, gt_l = (mod[l, :bsz, i * d:(i + 1) * d].reshape(bsz, 1, d) for i in range(3))
        sh_c, sc_c, gt_c = (jnp.broadcast_to(mod[l, bsz, i * d:(i + 1) * d].reshape(1, 1, d), (bsz, 1, d))
                            for i in range(3))
        b_i, b_f = _gate_bias(b_gate[l])
        gpre = g_pre[l].reshape(1, d)
        wsp = w_sp[l].astype(BF16)
        bsp = jnp.broadcast_to(b_sp[l].T[:, :, None], (CHUNK, GROUPS, GC)).reshape(CHUNK, S_WIDTH)
        wo = w_out[l].astype(BF16)
        tail = (wo, g_hnorm[l].reshape(1, M_WIDTH), g_sgu[l].reshape(1, S_WIDTH), g_post[l].reshape(1, d), wsp, bsp)
        front = functools.partial(_inproj, g_pre=gpre, w_t=w_t_all, b_i=b_i, b_f=b_f, layer=l)

        last = l == depth - 1
        pc = front(xc, sc_c, sh_c, full=not last, tm=256)
        hf_c, hb_c, cn_c, m_c = _mlstm(pc["q"], pc["kT"], pc["v"], pc["cola"], pc["colb"], pc["rowa"], cn_zero, m_zero)
        p = front(x, sc_l, sh_l, full=True, tm=512)
        hf, hb, _, _ = _mlstm(p["q"], p["kT"], p["v"], p["cola"], p["colb"], p["rowa"], cn_c, m_c)
        ym = fourier(p["xr"], p["xi"], consts_lat, b_fno[l])
        x = _outproj(hf, hb, p["ga"], p["vs"], p["gb"], ym, p["szc"], x, gt_l, *tail, tm=512)
        if not last:
            ymc = fourier(pc["xr"], pc["xi"], consts_ctx, b_fno[l])
            xc = _outproj(hf_c, hb_c, pc["ga"], pc["vs"], pc["gb"], ymc, pc["szc"], xc, gt_c, *tail, tm=256)
    return x
```

```python
import functools

import numpy as np
import jax
import jax.numpy as jnp
from jax.experimental import pallas as pl
from jax.experimental.pallas import tpu as pltpu

EPS = 1e-6
LOG2E = 1.4426950408889634
D_MODEL = 1024
HEADS = 8
DV = 128
DQK = 64
QK_W = HEADS * DQK
M_WIDTH = HEADS * DV
CHUNK = 128
S_WIDTH = 512
F_WIDTH = 512
GROUPS = 4
GC = 128
UNITS = 2 * HEADS
LANES = 128

VMEM_LIMIT = 56 * 1024 * 1024

F32 = jnp.float32
BF16 = jnp.bfloat16


def _cparams(*sem):
    return pltpu.CompilerParams(dimension_semantics=sem, vmem_limit_bytes=VMEM_LIMIT)


def _sigmoid(x):
    return 1.0 / (1.0 + jnp.exp(-x))


def _silu(x):
    return x * _sigmoid(x)


def _mod_kernel(cc_ref, w_ref, b_ref, o_ref):
    s = _silu(cc_ref[...]).astype(BF16)
    w = w_ref[0].astype(BF16)
    o_ref[0] = jnp.dot(s, w, preferred_element_type=F32) + b_ref[0]


def _modulation(cc, w_mod, b_mod):
    depth, d, d3 = w_mod.shape
    return pl.pallas_call(
        _mod_kernel,
        grid=(depth,),
        in_specs=[pl.BlockSpec((8, d), lambda l: (0, 0)),
                  pl.BlockSpec((1, d, d3), lambda l: (l, 0, 0)),
                  pl.BlockSpec((1, 1, d3), lambda l: (l, 0, 0))],
        out_specs=pl.BlockSpec((1, 8, d3), lambda l: (l, 0, 0)),
        out_shape=jax.ShapeDtypeStruct((depth, 8, d3), F32),
        compiler_params=_cparams("arbitrary"),
        name="modulation",
    )(cc, w_mod, b_mod.reshape(depth, 1, d3))


def _log_sigmoid(x):
    return jnp.minimum(x, 0.0) - jnp.log1p(jnp.exp(-jnp.abs(x)))


def _gate_scans(g_t, bi_ref, bf_ref, cola_ref, colb_ref, rowa_ref):
    tm = g_t.shape[1]
    n_c = tm // CHUNK
    assert n_c * UNITS <= LANES
    row = jax.lax.broadcasted_iota(jnp.int32, (CHUNK, LANES), 0)
    lane = jax.lax.broadcasted_iota(jnp.int32, (CHUNK, LANES), 1)
    fwd = lane % UNITS < HEADS
    gi = jnp.zeros((CHUNK, LANES), F32)
    gf = jnp.zeros((CHUNK, LANES), F32)
    pad = jnp.zeros((CHUNK - 2 * UNITS, CHUNK), F32)
    for c in range(n_c):
        t_c = jnp.concatenate([g_t[:, c * CHUNK:(c + 1) * CHUNK], pad], axis=0).T
        sel = lane // UNITS == c
        gi = jnp.where(sel, pltpu.roll(t_c, c * UNITS, axis=1) if c else t_c, gi)
        gf = jnp.where(sel, pltpu.roll(t_c, (c * UNITS - UNITS) % LANES, axis=1), gf)
    gi = gi + bi_ref[...]
    lf = _log_sigmoid(gf + bf_ref[...])

    def scan(x, op):
        pre, suf = x, x
        k = 1
        while k < CHUNK:
            sh = pltpu.roll(pre, k, axis=0)
            pre = jnp.where(row >= k, op(pre, sh), pre)
            sh = pltpu.roll(suf, CHUNK - k, axis=0)
            suf = jnp.where(row < CHUNK - k, op(suf, sh), suf)
            k *= 2
        return jnp.where(fwd, pre, suf)

    b = scan(lf, jnp.add)
    a = (gi - b) * LOG2E
    amax = scan(a, jnp.maximum)
    b = b * LOG2E
    a_t = a.T
    for c in range(n_c):
        rows = slice(c * CHUNK, (c + 1) * CHUNK)
        back = (LANES - c * UNITS) % LANES
        cola_ref[0, rows, :] = pltpu.roll(amax, back, axis=1) if c else amax
        colb_ref[0, rows, :] = pltpu.roll(b, back, axis=1) if c else b
        rowa_ref[0, :, rows] = a_t[c * UNITS:(c + 1) * UNITS, :]


def _pitch(n):
    p = n
    while (p // 8) % 2 == 0:
        p += 8
    return p


def _store_slabs(ref, g, z, n2):
    pin = _pitch(n2)
    for r in range(z.shape[0] // n2):
        ref[0, g, r * pin:r * pin + n2, :] = z[r * n2:(r + 1) * n2]
        if pin > n2:
            ref[0, g, r * pin + n2:(r + 1) * pin, :] = jnp.zeros((pin - n2, GC), F32)


def _inproj_kernel(x_ref, sc_ref, sh_ref, g_ref, wt_ref, bi_ref, bf_ref, *out_refs, pieces, slab_n2):
    nt = (((1,), (1,)), ((), ()))
    x = x_ref[0]
    y = x * jax.lax.rsqrt(jnp.mean(x * x, axis=-1, keepdims=True) + EPS)
    h = y * (g_ref[...] * (1.0 + sc_ref[0])) + sh_ref[0]
    hb = h.astype(BF16)

    def silu(z):
        t = z * 0.5
        return t + t * jnp.tanh(t)

    held = {}
    oi = 0
    for name, off, width in pieces:
        if name == "gates":
            gate_rows = QK_W + off
            w_kg = jnp.concatenate([wt_ref[0:QK_W, :], wt_ref[gate_rows:gate_rows + UNITS, :],
                                    wt_ref[gate_rows + LANES:gate_rows + LANES + UNITS, :]], axis=0)
            r = jax.lax.dot_general(w_kg, hb, nt, preferred_element_type=F32)
            out_refs[-1][0] = r[:QK_W].astype(BF16)
            _gate_scans(r[QK_W:], bi_ref, bf_ref, *out_refs[oi:oi + 3])
            oi += 3
            continue
        if name == "kT":
            continue
        r = jax.lax.dot_general(hb, wt_ref[QK_W + off:QK_W + off + width, :], nt, preferred_element_type=F32)
        if name == "f":
            xr_ref, xi_ref = out_refs[oi], out_refs[oi + 1]
            for g in range(GROUPS):
                zr, zi = r[:, 2 * g * GC:(2 * g + 1) * GC], r[:, (2 * g + 1) * GC:(2 * g + 2) * GC]
                if slab_n2 is None:
                    xr_ref[0, :, g * GC:(g + 1) * GC] = zr.astype(BF16)
                    xi_ref[0, :, g * GC:(g + 1) * GC] = zi.astype(BF16)
                else:
                    _store_slabs(xr_ref, g, zr, slab_n2)
                    _store_slabs(xi_ref, g, zi, slab_n2)
            oi += 2
        elif name in _HELD:
            held[name] = r
        else:
            if name == "za":
                r = (0.5 + 0.5 * jnp.tanh(held["o"] * 0.5)) * silu(r)
            elif name == "zb":
                r = held["u"] * silu(r)
            elif name == "zc":
                r = silu(r)
            out_refs[oi][0] = r.astype(BF16)
            oi += 1


_FULL_PIECES = (("q", QK_W), ("v", M_WIDTH), ("gates", 2 * LANES), ("o", M_WIDTH), ("za", M_WIDTH),
                ("u", S_WIDTH), ("vs", S_WIDTH), ("zb", S_WIDTH), ("f", 2 * F_WIDTH), ("zc", F_WIDTH))
_STATE_PIECES = _FULL_PIECES[:3]
_HELD = ("o", "u")
_GATED_NAME = {"za": "ga", "zb": "gb", "zc": "szc"}
_WT_BLK = 512


def _wrows(width):
    return -(-width // _WT_BLK) * _WT_BLK


def _inproj(x, sc, sh, g_pre, w_t, b_i, b_f, *, layer, full, tm):
    bsz, t, d = x.shape
    names = _FULL_PIECES if full else _STATE_PIECES
    slab_n2 = t // CHUNK if t > 2 * CHUNK else None
    row_spec = lambda width: pl.BlockSpec((1, tm, width), lambda b, i: (b, i, 0))
    pieces, off = [], 0
    out_names, out_shapes, out_specs = [], [], []
    for name, width in names:
        pieces.append((name, off, width))
        off += _wrows(width)
        if name == "gates":
            out_names += ["cola", "colb", "rowa"]
            out_shapes += [jax.ShapeDtypeStruct((bsz, t, LANES), F32)] * 2 + [jax.ShapeDtypeStruct((bsz, UNITS, t), F32)]
            out_specs += [row_spec(LANES), row_spec(LANES), pl.BlockSpec((1, UNITS, tm), lambda b, i: (b, 0, i))]
        elif name == "f":
            out_names += ["xr", "xi"]
            if slab_n2 is None:
                out_shapes += [jax.ShapeDtypeStruct((bsz, t, F_WIDTH), BF16)] * 2
                out_specs += [row_spec(F_WIDTH)] * 2
            else:
                assert tm % slab_n2 == 0
                pin = _pitch(slab_n2)
                out_shapes += [jax.ShapeDtypeStruct((bsz, GROUPS, CHUNK * pin, GC), F32)] * 2
                out_specs += [pl.BlockSpec((1, GROUPS, tm // slab_n2 * pin, GC), lambda b, i: (b, 0, i, 0))] * 2
        elif name not in _HELD:
            out_names.append(_GATED_NAME.get(name, name))
            out_shapes.append(jax.ShapeDtypeStruct((bsz, t, width), BF16))
            out_specs.append(row_spec(width))
    n_nat = off
    pieces.append(("kT", 0, QK_W))
    out_names.append("kT")
    out_shapes.append(jax.ShapeDtypeStruct((bsz, QK_W, t), BF16))
    out_specs.append(pl.BlockSpec((1, QK_W, tm), lambda b, i: (b, 0, i)))
    const = lambda shape, **kw: pl.BlockSpec(shape, lambda b, i: (0,) * len(shape), **kw)
    outs = pl.pallas_call(
        functools.partial(_inproj_kernel, pieces=tuple(pieces), slab_n2=slab_n2),
        grid=(bsz, t // tm),
        in_specs=[pl.BlockSpec((1, tm, d), lambda b, i: (b, i, 0)),
                  pl.BlockSpec((1, 1, d), lambda b, i: (b, 0, 0)),
                  pl.BlockSpec((1, 1, d), lambda b, i: (b, 0, 0)),
                  const((1, d)),
                  pl.BlockSpec((None, QK_W + n_nat, d), lambda b, i: (layer, 0, 0), pipeline_mode=pl.Buffered(1)),
                  const((1, LANES)), const((1, LANES))],
        out_specs=out_specs,
        out_shape=out_shapes,
        compiler_params=_cparams("parallel", "parallel"),
        name="inproj_full" if full else "inproj_state",
    )(x, sc, sh, g_pre, w_t, b_i, b_f)
    return dict(zip(out_names, outs))


def _mlstm_unit(h, d, sub, q_ref, kt_ref, v_ref, rowa_ref, m_diag, g_cols, c_cols, g_ends, hout_ref, cn_ref, mask):
    i = d * HEADS + h
    pair = h // 2
    rows = slice(sub * CHUNK, (sub + 1) * CHUNK)
    q_pair = q_ref[0, rows, pair * LANES:(pair + 1) * LANES]
    kt_h = kt_ref[0, h * DQK:(h + 1) * DQK, rows]
    zk = jnp.zeros((DQK, CHUNK), BF16)
    kt_ext = jnp.concatenate([kt_h, zk] if h % 2 == 0 else [zk, kt_h], axis=0)
    v_h = v_ref[0, rows, h * DV:(h + 1) * DV]
    vaug = jnp.concatenate([v_h, jnp.ones((CHUNK, DV), BF16)], axis=1)
    g = jnp.broadcast_to(g_cols[:, i:i + 1], (CHUNK, CHUNK))
    c = jnp.broadcast_to(c_cols[:, i:i + 1], (CHUNK, CHUNK))
    g_end = jnp.broadcast_to(g_ends[:, i:i + 1], (1, LANES))
    m_row = jnp.broadcast_to(m_diag[:, i:i + 1], (1, LANES))
    a_row = rowa_ref[0, h:h + 1, rows]
    cn = cn_ref[0, i]

    dmat = jnp.where(mask, jnp.exp2(a_row - g), 0.0)
    s = jnp.dot(q_pair, kt_ext, preferred_element_type=F32)
    p = (s * dmat).astype(BF16)
    qs = q_pair * jnp.exp2(m_row - g).astype(BF16)
    lhs = jnp.concatenate([p, qs], axis=1)
    zc = jnp.zeros((DQK, 2 * DV), BF16)
    cnb = cn.astype(BF16)
    cn_ext = jnp.concatenate([cnb, zc] if h % 2 == 0 else [zc, cnb], axis=0)
    rhs = jnp.concatenate([vaug, cn_ext], axis=0)
    out = jnp.dot(lhs, rhs, preferred_element_type=F32)
    num, den = out[:, :DV], out[:, DV:]
    hval = num / jnp.maximum(jnp.abs(den), jnp.exp2(c))
    hout_ref[0, rows, h * DV:(h + 1) * DV] = hval.astype(hout_ref.dtype)

    kts = kt_h * jnp.exp2(a_row - g_end).astype(BF16)
    upd = jnp.dot(kts, vaug, preferred_element_type=F32)
    decay = jnp.exp2(m_row - g_end)
    cn_ref[0, i] = jnp.concatenate([decay, decay], axis=1) * cn + upd


MLSTM_SUB = 8


def _mlstm_kernel(qf_ref, ktf_ref, vf_ref, caf_ref, cbf_ref, raf_ref,
                  qb_ref, ktb_ref, vb_ref, cab_ref, cbb_ref, rab_ref,
                  cn0_ref, m0_ref, hf_ref, hb_ref, cn_ref, m_ref, *, n_sub):
    @pl.when(pl.program_id(1) == 0)
    def _():
        cn_ref[...] = cn0_ref[...]
        m_ref[...] = m0_ref[...]

    t_idx = jax.lax.broadcasted_iota(jnp.int32, (CHUNK, CHUNK), 0)
    s_idx = jax.lax.broadcasted_iota(jnp.int32, (CHUNK, CHUNK), 1)
    unit_row = jax.lax.broadcasted_iota(jnp.int32, (UNITS, LANES), 0)
    unit_lane = jax.lax.broadcasted_iota(jnp.int32, (UNITS, LANES), 1)
    m_diag = jnp.sum(jnp.where(unit_row == unit_lane, m_ref[0], 0.0), axis=0, keepdims=True)
    fwd_lane = jax.lax.broadcasted_iota(jnp.int32, (1, LANES), 1) < HEADS

    def columns(ca_ref, cb_ref, sub, end):
        rows = slice(sub * CHUNK, (sub + 1) * CHUNK)
        g_cols = jnp.maximum(m_diag, ca_ref[0, rows, :])
        g_ends = g_cols[end:end + 1, :]
        m_new = cb_ref[0, sub * CHUNK + end:sub * CHUNK + end + 1, :] + g_ends
        return (g_cols, -(cb_ref[0, rows, :] + g_cols), g_ends), m_new

    for step in range(n_sub):
        sub_f, sub_b = step, n_sub - 1 - step
        cols_f, m_new_f = columns(caf_ref, cbf_ref, sub_f, CHUNK - 1)
        cols_b, m_new_b = columns(cab_ref, cbb_ref, sub_b, 0)
        for h in range(HEADS):
            _mlstm_unit(h, 0, sub_f, qf_ref, ktf_ref, vf_ref, raf_ref, m_diag, *cols_f, hf_ref, cn_ref, s_idx <= t_idx)
            _mlstm_unit(h, 1, sub_b, qb_ref, ktb_ref, vb_ref, rab_ref, m_diag, *cols_b, hb_ref, cn_ref, s_idx >= t_idx)
        m_diag = jnp.where(fwd_lane, m_new_f, m_new_b)
    m_ref[0] = jnp.where(unit_row == unit_lane, jnp.broadcast_to(m_diag, (UNITS, LANES)), 0.0)


def _mlstm(q, kt, v, cola, colb, rowa, cn0, m0):
    bsz, t, _ = q.shape
    n_sub = min(MLSTM_SUB, t // CHUNK)
    blk = n_sub * CHUNK
    nc = t // blk

    def specs(rev):
        cj = (lambda j: nc - 1 - j) if rev else (lambda j: j)
        d = 1 if rev else 0
        return [pl.BlockSpec((1, blk, QK_W), lambda b, j: (b, cj(j), 0)),
                pl.BlockSpec((1, QK_W, blk), lambda b, j: (b, 0, cj(j))),
                pl.BlockSpec((1, blk, M_WIDTH), lambda b, j: (b, cj(j), 0)),
                pl.BlockSpec((1, blk, LANES), lambda b, j: (b, cj(j), 0)),
                pl.BlockSpec((1, blk, LANES), lambda b, j: (b, cj(j), 0)),
                pl.BlockSpec((1, HEADS, blk), lambda b, j: (b, d, cj(j)))]

    cn_spec = pl.BlockSpec((1, UNITS, DQK, 2 * DV), lambda b, j: (b, 0, 0, 0))
    m_spec = pl.BlockSpec((1, UNITS, LANES), lambda b, j: (b, 0, 0))
    args = (q, kt, v, cola, colb, rowa)
    return pl.pallas_call(
        functools.partial(_mlstm_kernel, n_sub=n_sub),
        grid=(bsz, nc),
        in_specs=specs(False) + specs(True) + [cn_spec, m_spec],
        out_specs=[pl.BlockSpec((1, blk, M_WIDTH), lambda b, j: (b, j, 0)),
                   pl.BlockSpec((1, blk, M_WIDTH), lambda b, j: (b, nc - 1 - j, 0)),
                   cn_spec, m_spec],
        out_shape=[jax.ShapeDtypeStruct((bsz, t, M_WIDTH), BF16),
                   jax.ShapeDtypeStruct((bsz, t, M_WIDTH), BF16),
                   jax.ShapeDtypeStruct((bsz, UNITS, DQK, 2 * DV), F32),
                   jax.ShapeDtypeStruct((bsz, UNITS, LANES), F32)],
        compiler_params=_cparams("parallel", "arbitrary"),
        name="mlstm",
    )(*args, *args, cn0, m0)


OUT_PITCH = _pitch(CHUNK)


def _fourier_dense_kernel(xr_ref, xi_ref, cst_ref, bf_ref, o_ref):
    t = xr_ref.shape[1]
    xx = jnp.concatenate([xr_ref[0], xi_ref[0]], axis=0)
    yr = jnp.dot(cst_ref[...], xx, preferred_element_type=F32)
    for g in range(GROUPS):
        for c in range(t // CHUNK):
            o_ref[0, g, c * OUT_PITCH:c * OUT_PITCH + CHUNK, :] = (
                yr[c * CHUNK:(c + 1) * CHUNK, g * GC:(g + 1) * GC] + bf_ref[g])
            o_ref[0, g, c * OUT_PITCH + CHUNK:(c + 1) * OUT_PITCH, :] = jnp.zeros((OUT_PITCH - CHUNK, GC), F32)


def _fourier_dense(xr, xi, cst, bfno):
    bsz, t, w = xr.shape
    blk = pl.BlockSpec((1, t, w), lambda b: (b, 0, 0))
    rows = t // CHUNK * OUT_PITCH
    return pl.pallas_call(
        _fourier_dense_kernel,
        grid=(bsz,),
        in_specs=[blk, blk,
                  pl.BlockSpec((t, 2 * t), lambda b: (0, 0)),
                  pl.BlockSpec((GROUPS, 1, GC), lambda b: (0, 0, 0))],
        out_specs=pl.BlockSpec((1, GROUPS, rows, GC), lambda b: (b, 0, 0, 0)),
        out_shape=jax.ShapeDtypeStruct((bsz, GROUPS, rows, GC), F32),
        compiler_params=_cparams("parallel"),
        name="fourier_dense",
    )(xr, xi, cst, bfno.reshape(GROUPS, 1, GC))


def _fourier_fused_kernel(xr_ref, xi_ref, f1_ref, m_ref, bf_ref, o_ref, gr_scr, gi_scr, *, n1, n2):
    pin = _pitch(n2)

    def stage1(j, carry):
        t2 = 2 * j

        def ld(ref, s):
            return ref[0, 0, pl.ds(s, n1, stride=pin), :].astype(BF16)

        top = jnp.concatenate([ld(xr_ref, t2), ld(xr_ref, t2 + 1)], axis=1)
        bot = jnp.concatenate([ld(xi_ref, t2), ld(xi_ref, t2 + 1)], axis=1)
        g = jnp.dot(f1_ref[...], jnp.concatenate([top, bot], axis=0), preferred_element_type=F32)
        gr_scr[pl.ds(t2, n1, stride=pin), :] = g[:n1, :GC]
        gr_scr[pl.ds(t2 + 1, n1, stride=pin), :] = g[:n1, GC:]
        gi_scr[pl.ds(t2, n1, stride=pin), :] = g[n1:, :GC]
        gi_scr[pl.ds(t2 + 1, n1, stride=pin), :] = g[n1:, GC:]
        return carry

    jax.lax.fori_loop(0, n2 // 2, stage1, 0, unroll=True)

    def stage2(k1, carry):
        base = pl.multiple_of(k1 * pin, 8)
        gg = jnp.concatenate([gr_scr[pl.ds(base, n2), :], gi_scr[pl.ds(base, n2), :]], axis=0).astype(BF16)
        yr = jnp.dot(m_ref[k1], gg, preferred_element_type=F32) + bf_ref[0]
        o_ref[0, 0, pl.ds(k1, n2, stride=OUT_PITCH), :] = yr
        return carry

    jax.lax.fori_loop(0, n1, stage2, 0, unroll=True)
    for k2 in range(n2):
        o_ref[0, 0, k2 * OUT_PITCH + n1:(k2 + 1) * OUT_PITCH, :] = jnp.zeros((OUT_PITCH - n1, GC), F32)


def _fourier_fused(xr, xi, f1, mtab, bfno, *, n1):
    bsz, _, rows_in, _ = xr.shape
    n2 = mtab.shape[1]
    pin = _pitch(n2)
    assert rows_in == n1 * pin and n1 == CHUNK
    in_blk = pl.BlockSpec((1, 1, rows_in, GC), lambda b, g: (b, g, 0, 0))
    return pl.pallas_call(
        functools.partial(_fourier_fused_kernel, n1=n1, n2=n2),
        grid=(bsz, GROUPS),
        in_specs=[in_blk, in_blk,
                  pl.BlockSpec((2 * n1, 2 * n1), lambda b, g: (0, 0)),
                  pl.BlockSpec((n1, n2, 2 * n2), lambda b, g: (0, 0, 0)),
                  pl.BlockSpec((1, 1, GC), lambda b, g: (g, 0, 0))],
        out_specs=pl.BlockSpec((1, 1, n2 * OUT_PITCH, GC), lambda b, g: (b, g, 0, 0)),
        out_shape=jax.ShapeDtypeStruct((bsz, GROUPS, n2 * OUT_PITCH, GC), F32),
        scratch_shapes=[pltpu.VMEM((n1 * pin, GC), F32), pltpu.VMEM((n1 * pin, GC), F32)],
        compiler_params=_cparams("parallel", "parallel"),
        name="fourier_fused",
    )(xr, xi, f1, mtab, bfno.reshape(GROUPS, 1, GC))


def _dft_consts(t):
    ang = 2.0 * np.pi / GC * np.outer(np.arange(GC), np.arange(GC))
    cs = np.concatenate([np.cos(ang), -np.sin(ang)], axis=0) / np.sqrt(GC)
    out = {"cs": cs.astype(np.float32)}
    if t <= 2 * CHUNK:
        ang = 2.0 * np.pi / t * np.mod(np.outer(np.arange(t), np.arange(t)), t)
        out["dense"] = (np.concatenate([np.cos(ang), np.sin(ang)], axis=1) / np.sqrt(t)).astype(np.float32)
    else:
        n1 = CHUNK
        n2 = t // n1
        ang = 2.0 * np.pi / n1 * np.mod(np.outer(np.arange(n1), np.arange(n1)), n1)
        c, s = np.cos(ang), np.sin(ang)
        out["f1"] = (np.block([[c, s], [-s, c]]) / np.sqrt(n1)).astype(np.float32)
        k = np.arange(n1)[:, None, None] + n1 * np.arange(n2)[None, :, None]
        ang = 2.0 * np.pi / t * np.mod(k * np.arange(n2)[None, None, :], t)
        out["mtab"] = (np.concatenate([np.cos(ang), np.sin(ang)], axis=2) / np.sqrt(n2)).astype(np.float32)
    return out


def _outproj_kernel(hf_ref, hb_ref, ga_ref, vs_ref, gb_ref, ym_ref, szc_ref, x_ref, gt_ref,
                    wout_ref, ghn_ref, gsgu_ref, gpost_ref, wsp_ref, bsp_ref, xo_ref, y_scr):
    tm = x_ref.shape[1]
    kc = S_WIDTH

    def project(k):
        return jnp.dot(y_scr[:, k * kc:(k + 1) * kc], wout_ref[k * kc:(k + 1) * kc, :], preferred_element_type=F32)

    out = None
    for h in range(HEADS):
        sl = slice(h * DV, (h + 1) * DV)
        hh = (hf_ref[0, :, sl] + hb_ref[0, :, sl]).astype(F32)
        hn = hh * jax.lax.rsqrt(jnp.mean(hh * hh, axis=-1, keepdims=True) + EPS) * ghn_ref[:, sl]
        y_scr[:, sl] = hn.astype(BF16) * ga_ref[0, :, sl]
        if (h + 1) * DV % kc == 0:
            part = project((h + 1) * DV // kc - 1)
            out = part if out is None else out + part
    vs = vs_ref[0].astype(F32)
    vn = (vs * jax.lax.rsqrt(jnp.mean(vs * vs, axis=-1, keepdims=True) + EPS) * gsgu_ref[...]).astype(BF16)
    for c in range(tm // CHUNK):
        rows = slice(c * CHUNK, (c + 1) * CHUNK)
        for g in range(GROUPS):
            cols = slice(g * GC, (g + 1) * GC)
            mixed = jnp.dot(wsp_ref[g], vn[rows, cols], preferred_element_type=F32) + bsp_ref[:, cols]
            y_scr[rows, M_WIDTH + g * GC:M_WIDTH + (g + 1) * GC] = gb_ref[0, rows, cols] * mixed.astype(BF16)
    out = out + project(M_WIDTH // kc)
    for c in range(tm // CHUNK):
        rows = slice(c * CHUNK, (c + 1) * CHUNK)
        for g in range(GROUPS):
            cols = slice(g * GC, (g + 1) * GC)
            y_scr[rows, M_WIDTH + S_WIDTH + g * GC:M_WIDTH + S_WIDTH + (g + 1) * GC] = (
                ym_ref[0, g, c * OUT_PITCH:c * OUT_PITCH + CHUNK, :].astype(BF16) * szc_ref[0, rows, cols])
    out = out + project((M_WIDTH + S_WIDTH) // kc)
    on = out * jax.lax.rsqrt(jnp.mean(out * out, axis=-1, keepdims=True) + EPS) * gpost_ref[...]
    xo_ref[0] = x_ref[0] + gt_ref[0] * on


def _outproj(hf, hb, ga, vs, gb, ym, szc, x, gt, w_out, ghn, gsgu, gpost, wsp, bsp, *, tm):
    bsz, t, d = x.shape
    wide = pl.BlockSpec((1, tm, M_WIDTH), lambda b, i: (b, i, 0))
    half = pl.BlockSpec((1, tm, S_WIDTH), lambda b, i: (b, i, 0))
    const2 = lambda shape: pl.BlockSpec(shape, lambda b, i: (0,) * len(shape))
    return pl.pallas_call(
        _outproj_kernel,
        grid=(bsz, t // tm),
        in_specs=[wide, wide, wide, half, half,
                  pl.BlockSpec((1, GROUPS, tm // CHUNK * OUT_PITCH, GC), lambda b, i: (b, 0, i, 0)),
                  half, wide,
                  pl.BlockSpec((1, 1, d), lambda b, i: (b, 0, 0)),
                  const2((2 * D_MODEL, d)), const2((1, M_WIDTH)), const2((1, S_WIDTH)), const2((1, d)),
                  const2((GROUPS, CHUNK, CHUNK)), const2((CHUNK, S_WIDTH))],
        out_specs=wide,
        out_shape=jax.ShapeDtypeStruct((bsz, t, d), F32),
        scratch_shapes=[pltpu.VMEM((tm, 2 * D_MODEL), BF16)],
        compiler_params=_cparams("parallel", "parallel"),
        name="outproj",
    )(hf, hb, ga, vs, gb, ym, szc, x, gt, w_out, ghn, gsgu, gpost, wsp, bsp)


_W_IN_SPLITS = np.cumsum([0, QK_W, QK_W, M_WIDTH, 4 * HEADS, M_WIDTH, M_WIDTH, S_WIDTH, S_WIDTH, S_WIDTH, F_WIDTH,
                          F_WIDTH])
_N_NAT = sum(_wrows(w) for _, w in _FULL_PIECES)
_GATE_ROWS = 4 * HEADS


def _wprep_plan():
    c = _W_IN_SPLITS
    starts, q_blocks, gate_block, f_block = [], None, None, None
    for i in (1, 0, 2, 3, 4, 5, 6, 7, 8, 9, 10):
        if i == 3:
            gate_block = len(starts)
            starts.append(0)
            continue
        first = len(starts)
        if i == 9:
            f_block = first
            starts += list(range(int(c[i]), int(c[i + 1]), _WT_BLK // 2))
            continue
        starts += list(range(int(c[i]), int(c[i + 1]), _WT_BLK))
        if i == 0:
            q_blocks = (first, len(starts))
    return starts, q_blocks, gate_block, f_block


F_STEP_GROUPS = _WT_BLK // (2 * GC)


def _wprep_kernel(starts_ref, w_ref, g_ref, cst_ref, wf_ref, o_ref, *, q_blocks, gate_block, f_block):
    del starts_ref
    s = pl.program_id(1)
    n_f = GROUPS // F_STEP_GROUPS

    @pl.when((s != gate_block) & ((s < f_block) | (s >= f_block + n_f)))
    def _():
        scale = jnp.where((s >= q_blocks[0]) & (s < q_blocks[1]), DQK ** -0.5, 1.0)
        o_ref[0] = (w_ref[0] * scale).astype(BF16)

    for k in range(n_f):
        @pl.when(s == f_block + k)
        def _(k=k):
            outs = []
            for gg in range(F_STEP_GROUPS):
                a = w_ref[0, gg * GC:(gg + 1) * GC, :].astype(BF16)
                wc = jnp.dot(cst_ref[...], wf_ref[0, k * F_STEP_GROUPS + gg].astype(BF16),
                             preferred_element_type=F32)
                for half in range(2):
                    cw_t = wc[half * GC:(half + 1) * GC].T.astype(BF16)
                    outs.append(jnp.dot(cw_t, a, preferred_element_type=F32))
            o_ref[0] = jnp.concatenate(outs, axis=0).astype(BF16)

    @pl.when(s == gate_block)
    def _():
        g = g_ref[0]
        zero = jnp.zeros((LANES - UNITS, g.shape[1]), F32)
        tail = jnp.zeros((_WT_BLK - 2 * LANES, g.shape[1]), F32)
        o_ref[0] = jnp.concatenate([g[0:8], g[16:24], zero,
                                    g[8:16], g[24:32], zero,
                                    tail], axis=0).astype(BF16)


def _wprep(w_in, cst, w_fno):
    depth, d, p_in = w_in.shape
    w_t = jnp.swapaxes(w_in, 1, 2)
    starts, q_blocks, gate_block, f_block = _wprep_plan()
    assert _W_IN_SPLITS[3] % _GATE_ROWS == 0 and len(starts) * _WT_BLK == QK_W + _N_NAT
    grid_spec = pltpu.PrefetchScalarGridSpec(
        num_scalar_prefetch=1,
        grid=(depth, len(starts)),
        in_specs=[pl.BlockSpec((pl.Element(1), pl.Element(_WT_BLK), pl.Element(d)),
                               lambda l, s, st: (l, st[s] * 8, 0)),
                  pl.BlockSpec((1, _GATE_ROWS, d), lambda l, s, st: (l, int(_W_IN_SPLITS[3]) // _GATE_ROWS, 0)),
                  pl.BlockSpec((2 * GC, GC), lambda l, s, st: (0, 0)),
                  pl.BlockSpec((1, GROUPS, GC, GC), lambda l, s, st: (l, 0, 0, 0))],
        out_specs=pl.BlockSpec((1, _WT_BLK, d), lambda l, s, st: (l, s, 0)),
    )
    return pl.pallas_call(
        functools.partial(_wprep_kernel, q_blocks=q_blocks, gate_block=gate_block, f_block=f_block),
        grid_spec=grid_spec,
        out_shape=jax.ShapeDtypeStruct((depth, QK_W + _N_NAT, d), BF16),
        compiler_params=_cparams("parallel", "arbitrary"),
        name="wprep",
    )(jnp.asarray(starts, jnp.int32) // 8, w_t, w_t, cst, w_fno)


def _gate_bias(b_gate_l):
    b_i = jnp.tile(jnp.concatenate([b_gate_l[0:8], b_gate_l[16:24]]), LANES // UNITS).reshape(1, LANES)
    b_f = jnp.tile(jnp.concatenate([b_gate_l[8:16], b_gate_l[24:32]]), LANES // UNITS).reshape(1, LANES)
    return b_i, b_f


def kernel(x, c, ctx, c_ctx, w_mod, b_mod, g_pre, g_post, w_in, b_gate, g_hnorm, g_sgu, w_sp, b_sp, w_fno, b_fno, w_out):
    bsz, t_lat, d = x.shape
    t_ctx = ctx.shape[1]
    depth = w_mod.shape[0]
    assert d == D_MODEL and t_lat % (2 * CHUNK) == 0 and t_ctx % (2 * CHUNK) == 0 and bsz + 1 <= 8

    cc = jnp.concatenate([c, c_ctx[None, :], jnp.zeros((8 - bsz - 1, d), c.dtype)], axis=0)
    mod = _modulation(cc, w_mod, b_mod)
    consts_lat, consts_ctx = _dft_consts(t_lat), _dft_consts(t_ctx)
    w_t_all = _wprep(w_in, jnp.asarray(consts_lat["cs"]).astype(BF16), w_fno)

    def fourier(xr, xi, consts, bfno):
        if "dense" in consts:
            return _fourier_dense(xr, xi, jnp.asarray(consts["dense"]).astype(BF16), bfno)
        return _fourier_fused(xr, xi, jnp.asarray(consts["f1"]).astype(BF16),
                              jnp.asarray(consts["mtab"]).astype(BF16), bfno, n1=CHUNK)

    cn_zero = jnp.zeros((bsz, UNITS, DQK, 2 * DV), F32)
    m_zero = jnp.zeros((bsz, UNITS, LANES), F32)
    xc = ctx
    for l in range(depth):
        sh_l, sc_l, gt_l = (mod[l, :bsz, i * d:(i + 1) * d].reshape(bsz, 1, d) for i in range(3))
        sh_c, sc_c, gt_c = (jnp.broadcast_to(mod[l, bsz, i * d:(i + 1) * d].reshape(1, 1, d), (bsz, 1, d))
                            for i in range(3))
        b_i, b_f = _gate_bias(b_gate[l])
        gpre = g_pre[l].reshape(1, d)
        wsp = w_sp[l].astype(BF16)
        bsp = jnp.broadcast_to(b_sp[l].T[:, :, None], (CHUNK, GROUPS, GC)).reshape(CHUNK, S_WIDTH)
        wo = w_out[l].astype(BF16)
        tail = (wo, g_hnorm[l].reshape(1, M_WIDTH), g_sgu[l].reshape(1, S_WIDTH), g_post[l].reshape(1, d), wsp, bsp)
        front = functools.partial(_inproj, g_pre=gpre, w_t=w_t_all, b_i=b_i, b_f=b_f, layer=l)

        last = l == depth - 1
        pc = front(xc, sc_c, sh_c, full=not last, tm=256)
        hf_c, hb_c, cn_c, m_c = _mlstm(pc["q"], pc["kT"], pc["v"], pc["cola"], pc["colb"], pc["rowa"], cn_zero, m_zero)
        p = front(x, sc_l, sh_l, full=True, tm=512)
        hf, hb, _, _ = _mlstm(p["q"], p["kT"], p["v"], p["cola"], p["colb"], p["rowa"], cn_c, m_c)
        ym = fourier(p["xr"], p["xi"], consts_lat, b_fno[l])
        x = _outproj(hf, hb, p["ga"], p["vs"], p["gb"], ym, p["szc"], x, gt_l, *tail, tm=1024)
        if not last:
            ymc = fourier(pc["xr"], pc["xi"], consts_ctx, b_fno[l])
            xc = _outproj(hf_c, hb_c, pc["ga"], pc["vs"], pc["gb"], ymc, pc["szc"], xc, gt_c, *tail, tm=256)
    return x
```

```python
import functools

import numpy as np
import jax
import jax.numpy as jnp
from jax.experimental import pallas as pl
from jax.experimental.pallas import tpu as pltpu

EPS = 1e-6
LOG2E = 1.4426950408889634
D_MODEL = 1024
HEADS = 8
DV = 128
DQK = 64
QK_W = HEADS * DQK
M_WIDTH = HEADS * DV
CHUNK = 128
S_WIDTH = 512
F_WIDTH = 512
GROUPS = 4
GC = 128
UNITS = 2 * HEADS
LANES = 128

VMEM_LIMIT = 56 * 1024 * 1024

F32 = jnp.float32
BF16 = jnp.bfloat16


def _cparams(*sem):
    return pltpu.CompilerParams(dimension_semantics=sem, vmem_limit_bytes=VMEM_LIMIT)


def _sigmoid(x):
    return 1.0 / (1.0 + jnp.exp(-x))


def _silu(x):
    return x * _sigmoid(x)


def _mod_kernel(cc_ref, w_ref, b_ref, o_ref):
    s = _silu(cc_ref[...]).astype(BF16)
    w = w_ref[0].astype(BF16)
    o_ref[0] = jnp.dot(s, w, preferred_element_type=F32) + b_ref[0]


def _modulation(cc, w_mod, b_mod):
    depth, d, d3 = w_mod.shape
    return pl.pallas_call(
        _mod_kernel,
        grid=(depth,),
        in_specs=[pl.BlockSpec((8, d), lambda l: (0, 0)),
                  pl.BlockSpec((1, d, d3), lambda l: (l, 0, 0)),
                  pl.BlockSpec((1, 1, d3), lambda l: (l, 0, 0))],
        out_specs=pl.BlockSpec((1, 8, d3), lambda l: (l, 0, 0)),
        out_shape=jax.ShapeDtypeStruct((depth, 8, d3), F32),
        compiler_params=_cparams("arbitrary"),
        name="modulation",
    )(cc, w_mod, b_mod.reshape(depth, 1, d3))


def _log_sigmoid(x):
    return jnp.minimum(x, 0.0) - jnp.log1p(jnp.exp(-jnp.abs(x)))


def _gate_scans(g_t, bi_ref, bf_ref, cola_ref, colb_ref, rowa_ref):
    tm = g_t.shape[1]
    n_c = tm // CHUNK
    assert n_c * UNITS <= LANES
    row = jax.lax.broadcasted_iota(jnp.int32, (CHUNK, LANES), 0)
    lane = jax.lax.broadcasted_iota(jnp.int32, (CHUNK, LANES), 1)
    fwd = lane % UNITS < HEADS
    gi = jnp.zeros((CHUNK, LANES), F32)
    gf = jnp.zeros((CHUNK, LANES), F32)
    pad = jnp.zeros((CHUNK - 2 * UNITS, CHUNK), F32)
    for c in range(n_c):
        t_c = jnp.concatenate([g_t[:, c * CHUNK:(c + 1) * CHUNK], pad], axis=0).T
        sel = lane // UNITS == c
        gi = jnp.where(sel, pltpu.roll(t_c, c * UNITS, axis=1) if c else t_c, gi)
        gf = jnp.where(sel, pltpu.roll(t_c, (c * UNITS - UNITS) % LANES, axis=1), gf)
    gi = gi + bi_ref[...]
    lf = _log_sigmoid(gf + bf_ref[...])

    def scan(x, op):
        pre, suf = x, x
        k = 1
        while k < CHUNK:
            sh = pltpu.roll(pre, k, axis=0)
            pre = jnp.where(row >= k, op(pre, sh), pre)
            sh = pltpu.roll(suf, CHUNK - k, axis=0)
            suf = jnp.where(row < CHUNK - k, op(suf, sh), suf)
            k *= 2
        return jnp.where(fwd, pre, suf)

    b = scan(lf, jnp.add)
    a = (gi - b) * LOG2E
    amax = scan(a, jnp.maximum)
    b = b * LOG2E
    a_t = a.T
    for c in range(n_c):
        rows = slice(c * CHUNK, (c + 1) * CHUNK)
        back = (LANES - c * UNITS) % LANES
        cola_ref[0, rows, :] = pltpu.roll(amax, back, axis=1) if c else amax
        colb_ref[0, rows, :] = pltpu.roll(b, back, axis=1) if c else b
        rowa_ref[0, :, rows] = a_t[c * UNITS:(c + 1) * UNITS, :]


def _pitch(n):
    p = n
    while (p // 8) % 2 == 0:
        p += 8
    return p


def _store_slabs(ref, g, z, n2):
    pin = _pitch(n2)
    for r in range(z.shape[0] // n2):
        ref[0, g, r * pin:r * pin + n2, :] = z[r * n2:(r + 1) * n2]
        if pin > n2:
            ref[0, g, r * pin + n2:(r + 1) * pin, :] = jnp.zeros((pin - n2, GC), F32)


def _inproj_kernel(x_ref, sc_ref, sh_ref, g_ref, wt_ref, bi_ref, bf_ref, *out_refs, pieces, slab_n2):
    nt = (((1,), (1,)), ((), ()))
    x = x_ref[0]
    y = x * jax.lax.rsqrt(jnp.mean(x * x, axis=-1, keepdims=True) + EPS)
    h = y * (g_ref[...] * (1.0 + sc_ref[0])) + sh_ref[0]
    hb = h.astype(BF16)

    def silu(z):
        t = z * 0.5
        return t + t * jnp.tanh(t)

    held = {}
    oi = 0
    for name, off, width in pieces:
        if name == "gates":
            gate_rows = QK_W + off
            w_kg = jnp.concatenate([wt_ref[0:QK_W, :], wt_ref[gate_rows:gate_rows + UNITS, :],
                                    wt_ref[gate_rows + LANES:gate_rows + LANES + UNITS, :]], axis=0)
            r = jax.lax.dot_general(w_kg, hb, nt, preferred_element_type=F32)
            out_refs[-1][0] = r[:QK_W].astype(BF16)
            _gate_scans(r[QK_W:], bi_ref, bf_ref, *out_refs[oi:oi + 3])
            oi += 3
            continue
        if name == "kT":
            continue
        r = jax.lax.dot_general(hb, wt_ref[QK_W + off:QK_W + off + width, :], nt, preferred_element_type=F32)
        if name == "f":
            xr_ref, xi_ref = out_refs[oi], out_refs[oi + 1]
            for g in range(GROUPS):
                zr, zi = r[:, 2 * g * GC:(2 * g + 1) * GC], r[:, (2 * g + 1) * GC:(2 * g + 2) * GC]
                if slab_n2 is None:
                    xr_ref[0, :, g * GC:(g + 1) * GC] = zr.astype(BF16)
                    xi_ref[0, :, g * GC:(g + 1) * GC] = zi.astype(BF16)
                else:
                    _store_slabs(xr_ref, g, zr, slab_n2)
                    _store_slabs(xi_ref, g, zi, slab_n2)
            oi += 2
        elif name in _HELD:
            held[name] = r
        else:
            if name == "za":
                r = (0.5 + 0.5 * jnp.tanh(held["o"] * 0.5)) * silu(r)
            elif name == "zb":
                r = held["u"] * silu(r)
            elif name == "zc":
                r = silu(r)
            out_refs[oi][0] = r.astype(BF16)
            oi += 1


_FULL_PIECES = (("q", QK_W), ("v", M_WIDTH), ("gates", 2 * LANES), ("o", M_WIDTH), ("za", M_WIDTH),
                ("u", S_WIDTH), ("vs", S_WIDTH), ("zb", S_WIDTH), ("f", 2 * F_WIDTH), ("zc", F_WIDTH))
_STATE_PIECES = _FULL_PIECES[:3]
_HELD = ("o", "u")
_GATED_NAME = {"za": "ga", "zb": "gb", "zc": "szc"}
_WT_BLK = 512


def _wrows(width):
    return -(-width // _WT_BLK) * _WT_BLK


def _inproj(x, sc, sh, g_pre, w_t, b_i, b_f, *, layer, full, tm):
    bsz, t, d = x.shape
    names = _FULL_PIECES if full else _STATE_PIECES
    slab_n2 = t // CHUNK if t > 2 * CHUNK else None
    row_spec = lambda width: pl.BlockSpec((1, tm, width), lambda b, i: (b, i, 0))
    pieces, off = [], 0
    out_names, out_shapes, out_specs = [], [], []
    for name, width in names:
        pieces.append((name, off, width))
        off += _wrows(width)
        if name == "gates":
            out_names += ["cola", "colb", "rowa"]
            out_shapes += [jax.ShapeDtypeStruct((bsz, t, LANES), F32)] * 2 + [jax.ShapeDtypeStruct((bsz, UNITS, t), F32)]
            out_specs += [row_spec(LANES), row_spec(LANES), pl.BlockSpec((1, UNITS, tm), lambda b, i: (b, 0, i))]
        elif name == "f":
            out_names += ["xr", "xi"]
            if slab_n2 is None:
                out_shapes += [jax.ShapeDtypeStruct((bsz, t, F_WIDTH), BF16)] * 2
                out_specs += [row_spec(F_WIDTH)] * 2
            else:
                assert tm % slab_n2 == 0
                pin = _pitch(slab_n2)
                out_shapes += [jax.ShapeDtypeStruct((bsz, GROUPS, CHUNK * pin, GC), F32)] * 2
                out_specs += [pl.BlockSpec((1, GROUPS, tm // slab_n2 * pin, GC), lambda b, i: (b, 0, i, 0))] * 2
        elif name not in _HELD:
            out_names.append(_GATED_NAME.get(name, name))
            out_shapes.append(jax.ShapeDtypeStruct((bsz, t, width), BF16))
            out_specs.append(row_spec(width))
    n_nat = off
    pieces.append(("kT", 0, QK_W))
    out_names.append("kT")
    out_shapes.append(jax.ShapeDtypeStruct((bsz, QK_W, t), BF16))
    out_specs.append(pl.BlockSpec((1, QK_W, tm), lambda b, i: (b, 0, i)))
    const = lambda shape, **kw: pl.BlockSpec(shape, lambda b, i: (0,) * len(shape), **kw)
    outs = pl.pallas_call(
        functools.partial(_inproj_kernel, pieces=tuple(pieces), slab_n2=slab_n2),
        grid=(bsz, t // tm),
        in_specs=[pl.BlockSpec((1, tm, d), lambda b, i: (b, i, 0)),
                  pl.BlockSpec((1, 1, d), lambda b, i: (b, 0, 0)),
                  pl.BlockSpec((1, 1, d), lambda b, i: (b, 0, 0)),
                  const((1, d)),
                  pl.BlockSpec((None, QK_W + n_nat, d), lambda b, i: (layer, 0, 0), pipeline_mode=pl.Buffered(1)),
                  const((1, LANES)), const((1, LANES))],
        out_specs=out_specs,
        out_shape=out_shapes,
        compiler_params=_cparams("parallel", "parallel"),
        name="inproj_full" if full else "inproj_state",
    )(x, sc, sh, g_pre, w_t, b_i, b_f)
    return dict(zip(out_names, outs))


def _mlstm_unit(h, d, sub, q_ref, kt_ref, v_ref, rowa_ref, m_diag, g_cols, c_cols, g_ends, hout_ref, cn_ref, mask):
    i = d * HEADS + h
    pair = h // 2
    rows = slice(sub * CHUNK, (sub + 1) * CHUNK)
    q_pair = q_ref[0, rows, pair * LANES:(pair + 1) * LANES]
    kt_h = kt_ref[0, h * DQK:(h + 1) * DQK, rows]
    zk = jnp.zeros((DQK, CHUNK), BF16)
    kt_ext = jnp.concatenate([kt_h, zk] if h % 2 == 0 else [zk, kt_h], axis=0)
    v_h = v_ref[0, rows, h * DV:(h + 1) * DV]
    vaug = jnp.concatenate([v_h, jnp.ones((CHUNK, DV), BF16)], axis=1)
    g = jnp.broadcast_to(g_cols[:, i:i + 1], (CHUNK, CHUNK))
    c = jnp.broadcast_to(c_cols[:, i:i + 1], (CHUNK, CHUNK))
    g_end = jnp.broadcast_to(g_ends[:, i:i + 1], (1, LANES))
    m_row = jnp.broadcast_to(m_diag[:, i:i + 1], (1, LANES))
    a_row = rowa_ref[0, h:h + 1, rows]
    cn = cn_ref[0, i]

    dmat = jnp.where(mask, jnp.exp2(a_row - g), 0.0)
    s = jnp.dot(q_pair, kt_ext, preferred_element_type=F32)
    p = (s * dmat).astype(BF16)
    qs = q_pair * jnp.exp2(m_row - g).astype(BF16)
    lhs = jnp.concatenate([p, qs], axis=1)
    zc = jnp.zeros((DQK, 2 * DV), BF16)
    cnb = cn.astype(BF16)
    cn_ext = jnp.concatenate([cnb, zc] if h % 2 == 0 else [zc, cnb], axis=0)
    rhs = jnp.concatenate([vaug, cn_ext], axis=0)
    out = jnp.dot(lhs, rhs, preferred_element_type=F32)
    num, den = out[:, :DV], out[:, DV:]
    hval = num / jnp.maximum(jnp.abs(den), jnp.exp2(c))
    hout_ref[0, rows, h * DV:(h + 1) * DV] = hval.astype(hout_ref.dtype)

    kts = kt_h * jnp.exp2(a_row - g_end).astype(BF16)
    upd = jnp.dot(kts, vaug, preferred_element_type=F32)
    decay = jnp.exp2(m_row - g_end)
    cn_ref[0, i] = jnp.concatenate([decay, decay], axis=1) * cn + upd


MLSTM_SUB = 8


def _mlstm_kernel(qf_ref, ktf_ref, vf_ref, caf_ref, cbf_ref, raf_ref,
                  qb_ref, ktb_ref, vb_ref, cab_ref, cbb_ref, rab_ref,
                  cn0_ref, m0_ref, hf_ref, hb_ref, cn_ref, m_ref, *, n_sub):
    @pl.when(pl.program_id(1) == 0)
    def _():
        cn_ref[...] = cn0_ref[...]
        m_ref[...] = m0_ref[...]

    t_idx = jax.lax.broadcasted_iota(jnp.int32, (CHUNK, CHUNK), 0)
    s_idx = jax.lax.broadcasted_iota(jnp.int32, (CHUNK, CHUNK), 1)
    unit_row = jax.lax.broadcasted_iota(jnp.int32, (UNITS, LANES), 0)
    unit_lane = jax.lax.broadcasted_iota(jnp.int32, (UNITS, LANES), 1)
    m_diag = jnp.sum(jnp.where(unit_row == unit_lane, m_ref[0], 0.0), axis=0, keepdims=True)
    fwd_lane = jax.lax.broadcasted_iota(jnp.int32, (1, LANES), 1) < HEADS

    def columns(ca_ref, cb_ref, sub, end):
        rows = slice(sub * CHUNK, (sub + 1) * CHUNK)
        g_cols = jnp.maximum(m_diag, ca_ref[0, rows, :])
        g_ends = g_cols[end:end + 1, :]
        m_new = cb_ref[0, sub * CHUNK + end:sub * CHUNK + end + 1, :] + g_ends
        return (g_cols, -(cb_ref[0, rows, :] + g_cols), g_ends), m_new

    for step in range(n_sub):
        sub_f, sub_b = step, n_sub - 1 - step
        cols_f, m_new_f = columns(caf_ref, cbf_ref, sub_f, CHUNK - 1)
        cols_b, m_new_b = columns(cab_ref, cbb_ref, sub_b, 0)
        for h in range(HEADS):
            _mlstm_unit(h, 0, sub_f, qf_ref, ktf_ref, vf_ref, raf_ref, m_diag, *cols_f, hf_ref, cn_ref, s_idx <= t_idx)
            _mlstm_unit(h, 1, sub_b, qb_ref, ktb_ref, vb_ref, rab_ref, m_diag, *cols_b, hb_ref, cn_ref, s_idx >= t_idx)
        m_diag = jnp.where(fwd_lane, m_new_f, m_new_b)
    m_ref[0] = jnp.where(unit_row == unit_lane, jnp.broadcast_to(m_diag, (UNITS, LANES)), 0.0)


def _mlstm(q, kt, v, cola, colb, rowa, cn0, m0):
    bsz, t, _ = q.shape
    n_sub = min(MLSTM_SUB, t // CHUNK)
    blk = n_sub * CHUNK
    nc = t // blk

    def specs(rev):
        cj = (lambda j: nc - 1 - j) if rev else (lambda j: j)
        d = 1 if rev else 0
        return [pl.BlockSpec((1, blk, QK_W), lambda b, j: (b, cj(j), 0)),
                pl.BlockSpec((1, QK_W, blk), lambda b, j: (b, 0, cj(j))),
                pl.BlockSpec((1, blk, M_WIDTH), lambda b, j: (b, cj(j), 0)),
                pl.BlockSpec((1, blk, LANES), lambda b, j: (b, cj(j), 0)),
                pl.BlockSpec((1, blk, LANES), lambda b, j: (b, cj(j), 0)),
                pl.BlockSpec((1, HEADS, blk), lambda b, j: (b, d, cj(j)))]

    cn_spec = pl.BlockSpec((1, UNITS, DQK, 2 * DV), lambda b, j: (b, 0, 0, 0))
    m_spec = pl.BlockSpec((1, UNITS, LANES), lambda b, j: (b, 0, 0))
    args = (q, kt, v, cola, colb, rowa)
    return pl.pallas_call(
        functools.partial(_mlstm_kernel, n_sub=n_sub),
        grid=(bsz, nc),
        in_specs=specs(False) + specs(True) + [cn_spec, m_spec],
        out_specs=[pl.BlockSpec((1, blk, M_WIDTH), lambda b, j: (b, j, 0)),
                   pl.BlockSpec((1, blk, M_WIDTH), lambda b, j: (b, nc - 1 - j, 0)),
                   cn_spec, m_spec],
        out_shape=[jax.ShapeDtypeStruct((bsz, t, M_WIDTH), BF16),
                   jax.ShapeDtypeStruct((bsz, t, M_WIDTH), BF16),
                   jax.ShapeDtypeStruct((bsz, UNITS, DQK, 2 * DV), F32),
                   jax.ShapeDtypeStruct((bsz, UNITS, LANES), F32)],
        compiler_params=_cparams("parallel", "arbitrary"),
        name="mlstm",
    )(*args, *args, cn0, m0)


OUT_PITCH = _pitch(CHUNK)


def _fourier_dense_kernel(xr_ref, xi_ref, cst_ref, bf_ref, o_ref):
    xx = jnp.concatenate([xr_ref[0], xi_ref[0]], axis=0)
    yr = jnp.dot(cst_ref[...], xx, preferred_element_type=F32)
    o_ref[0] = (yr + bf_ref[...]).astype(BF16)


def _fourier_dense(xr, xi, cst, bfno):
    bsz, t, w = xr.shape
    blk = pl.BlockSpec((1, t, w), lambda b: (b, 0, 0))
    return pl.pallas_call(
        _fourier_dense_kernel,
        grid=(bsz,),
        in_specs=[blk, blk,
                  pl.BlockSpec((t, 2 * t), lambda b: (0, 0)),
                  pl.BlockSpec((1, w), lambda b: (0, 0))],
        out_specs=blk,
        out_shape=jax.ShapeDtypeStruct((bsz, t, w), BF16),
        compiler_params=_cparams("parallel"),
        name="fourier_dense",
    )(xr, xi, cst, bfno.reshape(1, w))


def _fourier_fused_kernel(xr_ref, xi_ref, f1_ref, m_ref, bf_ref, o_ref, gr_scr, gi_scr, y_scr, *, n1, n2):
    pin = _pitch(n2)

    def stage1(j, carry):
        t2 = 2 * j

        def ld(ref, s):
            return ref[0, 0, pl.ds(s, n1, stride=pin), :].astype(BF16)

        top = jnp.concatenate([ld(xr_ref, t2), ld(xr_ref, t2 + 1)], axis=1)
        bot = jnp.concatenate([ld(xi_ref, t2), ld(xi_ref, t2 + 1)], axis=1)
        g = jnp.dot(f1_ref[...], jnp.concatenate([top, bot], axis=0), preferred_element_type=F32)
        gr_scr[pl.ds(t2, n1, stride=pin), :] = g[:n1, :GC]
        gr_scr[pl.ds(t2 + 1, n1, stride=pin), :] = g[:n1, GC:]
        gi_scr[pl.ds(t2, n1, stride=pin), :] = g[n1:, :GC]
        gi_scr[pl.ds(t2 + 1, n1, stride=pin), :] = g[n1:, GC:]
        return carry

    jax.lax.fori_loop(0, n2 // 2, stage1, 0, unroll=True)

    def stage2(k1, carry):
        base = pl.multiple_of(k1 * pin, 8)
        gg = jnp.concatenate([gr_scr[pl.ds(base, n2), :], gi_scr[pl.ds(base, n2), :]], axis=0).astype(BF16)
        yr = jnp.dot(m_ref[k1], gg, preferred_element_type=F32) + bf_ref[0]
        y_scr[pl.ds(k1, n2, stride=OUT_PITCH), :] = yr
        return carry

    jax.lax.fori_loop(0, n1, stage2, 0, unroll=True)
    for k2 in range(n2):
        o_ref[0, k2 * n1:(k2 + 1) * n1, :] = y_scr[k2 * OUT_PITCH:k2 * OUT_PITCH + n1, :].astype(BF16)


def _fourier_fused(xr, xi, f1, mtab, bfno, *, n1):
    bsz, _, rows_in, _ = xr.shape
    n2 = mtab.shape[1]
    pin = _pitch(n2)
    assert rows_in == n1 * pin and n1 == CHUNK
    in_blk = pl.BlockSpec((1, 1, rows_in, GC), lambda b, g: (b, g, 0, 0))
    return pl.pallas_call(
        functools.partial(_fourier_fused_kernel, n1=n1, n2=n2),
        grid=(bsz, GROUPS),
        in_specs=[in_blk, in_blk,
                  pl.BlockSpec((2 * n1, 2 * n1), lambda b, g: (0, 0)),
                  pl.BlockSpec((n1, n2, 2 * n2), lambda b, g: (0, 0, 0)),
                  pl.BlockSpec((1, 1, GC), lambda b, g: (g, 0, 0))],
        out_specs=pl.BlockSpec((1, n1 * n2, GC), lambda b, g: (b, 0, g)),
        out_shape=jax.ShapeDtypeStruct((bsz, n1 * n2, GROUPS * GC), BF16),
        scratch_shapes=[pltpu.VMEM((n1 * pin, GC), F32), pltpu.VMEM((n1 * pin, GC), F32),
                        pltpu.VMEM((n2 * OUT_PITCH, GC), F32)],
        compiler_params=_cparams("parallel", "parallel"),
        name="fourier_fused",
    )(xr, xi, f1, mtab, bfno.reshape(GROUPS, 1, GC))


def _dft_consts(t):
    ang = 2.0 * np.pi / GC * np.outer(np.arange(GC), np.arange(GC))
    cs = np.concatenate([np.cos(ang), -np.sin(ang)], axis=0) / np.sqrt(GC)
    out = {"cs": cs.astype(np.float32)}
    if t <= 2 * CHUNK:
        ang = 2.0 * np.pi / t * np.mod(np.outer(np.arange(t), np.arange(t)), t)
        out["dense"] = (np.concatenate([np.cos(ang), np.sin(ang)], axis=1) / np.sqrt(t)).astype(np.float32)
    else:
        n1 = CHUNK
        n2 = t // n1
        ang = 2.0 * np.pi / n1 * np.mod(np.outer(np.arange(n1), np.arange(n1)), n1)
        c, s = np.cos(ang), np.sin(ang)
        out["f1"] = (np.block([[c, s], [-s, c]]) / np.sqrt(n1)).astype(np.float32)
        k = np.arange(n1)[:, None, None] + n1 * np.arange(n2)[None, :, None]
        ang = 2.0 * np.pi / t * np.mod(k * np.arange(n2)[None, None, :], t)
        out["mtab"] = (np.concatenate([np.cos(ang), np.sin(ang)], axis=2) / np.sqrt(n2)).astype(np.float32)
    return out


def _outproj_kernel(hf_ref, hb_ref, ga_ref, vs_ref, gb_ref, ym_ref, szc_ref, x_ref, gt_ref,
                    wout_ref, ghn_ref, gsgu_ref, gpost_ref, wsp_ref, bsp_ref, xo_ref, y_scr):
    tm = x_ref.shape[1]
    kc = S_WIDTH

    def project(k):
        return jnp.dot(y_scr[:, k * kc:(k + 1) * kc], wout_ref[k * kc:(k + 1) * kc, :], preferred_element_type=F32)

    out = None
    for h in range(HEADS):
        sl = slice(h * DV, (h + 1) * DV)
        hh = (hf_ref[0, :, sl] + hb_ref[0, :, sl]).astype(F32)
        hn = hh * jax.lax.rsqrt(jnp.mean(hh * hh, axis=-1, keepdims=True) + EPS) * ghn_ref[:, sl]
        y_scr[:, sl] = hn.astype(BF16) * ga_ref[0, :, sl]
        if (h + 1) * DV % kc == 0:
            part = project((h + 1) * DV // kc - 1)
            out = part if out is None else out + part
    vs = vs_ref[0].astype(F32)
    vn = (vs * jax.lax.rsqrt(jnp.mean(vs * vs, axis=-1, keepdims=True) + EPS) * gsgu_ref[...]).astype(BF16)
    for c in range(tm // CHUNK):
        rows = slice(c * CHUNK, (c + 1) * CHUNK)
        for g in range(GROUPS):
            cols = slice(g * GC, (g + 1) * GC)
            mixed = jnp.dot(wsp_ref[g], vn[rows, cols], preferred_element_type=F32) + bsp_ref[:, cols]
            y_scr[rows, M_WIDTH + g * GC:M_WIDTH + (g + 1) * GC] = gb_ref[0, rows, cols] * mixed.astype(BF16)
    out = out + project(M_WIDTH // kc)
    for c in range(tm // CHUNK):
        rows = slice(c * CHUNK, (c + 1) * CHUNK)
        for g in range(GROUPS):
            cols = slice(g * GC, (g + 1) * GC)
            y_scr[rows, M_WIDTH + S_WIDTH + g * GC:M_WIDTH + S_WIDTH + (g + 1) * GC] = (
                ym_ref[0, rows, cols] * szc_ref[0, rows, cols])
    out = out + project((M_WIDTH + S_WIDTH) // kc)
    on = out * jax.lax.rsqrt(jnp.mean(out * out, axis=-1, keepdims=True) + EPS) * gpost_ref[...]
    xo_ref[0] = x_ref[0] + gt_ref[0] * on


def _outproj(hf, hb, ga, vs, gb, ym, szc, x, gt, w_out, ghn, gsgu, gpost, wsp, bsp, *, tm):
    bsz, t, d = x.shape
    wide = pl.BlockSpec((1, tm, M_WIDTH), lambda b, i: (b, i, 0))
    half = pl.BlockSpec((1, tm, S_WIDTH), lambda b, i: (b, i, 0))
    const2 = lambda shape: pl.BlockSpec(shape, lambda b, i: (0,) * len(shape))
    return pl.pallas_call(
        _outproj_kernel,
        grid=(bsz, t // tm),
        in_specs=[wide, wide, wide, half, half, half, half, wide,
                  pl.BlockSpec((1, 1, d), lambda b, i: (b, 0, 0)),
                  const2((2 * D_MODEL, d)), const2((1, M_WIDTH)), const2((1, S_WIDTH)), const2((1, d)),
                  const2((GROUPS, CHUNK, CHUNK)), const2((CHUNK, S_WIDTH))],
        out_specs=wide,
        out_shape=jax.ShapeDtypeStruct((bsz, t, d), F32),
        scratch_shapes=[pltpu.VMEM((tm, 2 * D_MODEL), BF16)],
        compiler_params=_cparams("parallel", "parallel"),
        name="outproj",
    )(hf, hb, ga, vs, gb, ym, szc, x, gt, w_out, ghn, gsgu, gpost, wsp, bsp)


_W_IN_SPLITS = np.cumsum([0, QK_W, QK_W, M_WIDTH, 4 * HEADS, M_WIDTH, M_WIDTH, S_WIDTH, S_WIDTH, S_WIDTH, F_WIDTH,
                          F_WIDTH])
_N_NAT = sum(_wrows(w) for _, w in _FULL_PIECES)
_GATE_ROWS = 4 * HEADS


def _wprep_plan():
    c = _W_IN_SPLITS
    starts, q_blocks, gate_block, f_block = [], None, None, None
    for i in (1, 0, 2, 3, 4, 5, 6, 7, 8, 9, 10):
        if i == 3:
            gate_block = len(starts)
            starts.append(0)
            continue
        first = len(starts)
        if i == 9:
            f_block = first
            starts += list(range(int(c[i]), int(c[i + 1]), _WT_BLK // 2))
            continue
        starts += list(range(int(c[i]), int(c[i + 1]), _WT_BLK))
        if i == 0:
            q_blocks = (first, len(starts))
    return starts, q_blocks, gate_block, f_block


F_STEP_GROUPS = _WT_BLK // (2 * GC)


def _wprep_kernel(starts_ref, w_ref, g_ref, cst_ref, wf_ref, o_ref, *, q_blocks, gate_block, f_block):
    del starts_ref
    s = pl.program_id(1)
    n_f = GROUPS // F_STEP_GROUPS

    @pl.when((s != gate_block) & ((s < f_block) | (s >= f_block + n_f)))
    def _():
        scale = jnp.where((s >= q_blocks[0]) & (s < q_blocks[1]), DQK ** -0.5, 1.0)
        o_ref[0] = (w_ref[0] * scale).astype(BF16)

    for k in range(n_f):
        @pl.when(s == f_block + k)
        def _(k=k):
            outs = []
            for gg in range(F_STEP_GROUPS):
                a = w_ref[0, gg * GC:(gg + 1) * GC, :].astype(BF16)
                wc = jnp.dot(cst_ref[...], wf_ref[0, k * F_STEP_GROUPS + gg].astype(BF16),
                             preferred_element_type=F32)
                for half in range(2):
                    cw_t = wc[half * GC:(half + 1) * GC].T.astype(BF16)
                    outs.append(jnp.dot(cw_t, a, preferred_element_type=F32))
            o_ref[0] = jnp.concatenate(outs, axis=0).astype(BF16)

    @pl.when(s == gate_block)
    def _():
        g = g_ref[0]
        zero = jnp.zeros((LANES - UNITS, g.shape[1]), F32)
        tail = jnp.zeros((_WT_BLK - 2 * LANES, g.shape[1]), F32)
        o_ref[0] = jnp.concatenate([g[0:8], g[16:24], zero,
                                    g[8:16], g[24:32], zero,
                                    tail], axis=0).astype(BF16)


def _wprep(w_in, cst, w_fno):
    depth, d, p_in = w_in.shape
    w_t = jnp.swapaxes(w_in, 1, 2)
    starts, q_blocks, gate_block, f_block = _wprep_plan()
    assert _W_IN_SPLITS[3] % _GATE_ROWS == 0 and len(starts) * _WT_BLK == QK_W + _N_NAT
    grid_spec = pltpu.PrefetchScalarGridSpec(
        num_scalar_prefetch=1,
        grid=(depth, len(starts)),
        in_specs=[pl.BlockSpec((pl.Element(1), pl.Element(_WT_BLK), pl.Element(d)),
                               lambda l, s, st: (l, st[s] * 8, 0)),
                  pl.BlockSpec((1, _GATE_ROWS, d), lambda l, s, st: (l, int(_W_IN_SPLITS[3]) // _GATE_ROWS, 0)),
                  pl.BlockSpec((2 * GC, GC), lambda l, s, st: (0, 0)),
                  pl.BlockSpec((1, GROUPS, GC, GC), lambda l, s, st: (l, 0, 0, 0))],
        out_specs=pl.BlockSpec((1, _WT_BLK, d), lambda l, s, st: (l, s, 0)),
    )
    return pl.pallas_call(
        functools.partial(_wprep_kernel, q_blocks=q_blocks, gate_block=gate_block, f_block=f_block),
        grid_spec=grid_spec,
        out_shape=jax.ShapeDtypeStruct((depth, QK_W + _N_NAT, d), BF16),
        compiler_params=_cparams("parallel", "arbitrary"),
        name="wprep",
    )(jnp.asarray(starts, jnp.int32) // 8, w_t, w_t, cst, w_fno)


def _gate_bias(b_gate_l):
    b_i = jnp.tile(jnp.concatenate([b_gate_l[0:8], b_gate_l[16:24]]), LANES // UNITS).reshape(1, LANES)
    b_f = jnp.tile(jnp.concatenate([b_gate_l[8:16], b_gate_l[24:32]]), LANES // UNITS).reshape(1, LANES)
    return b_i, b_f


def kernel(x, c, ctx, c_ctx, w_mod, b_mod, g_pre, g_post, w_in, b_gate, g_hnorm, g_sgu, w_sp, b_sp, w_fno, b_fno, w_out):
    bsz, t_lat, d = x.shape
    t_ctx = ctx.shape[1]
    depth = w_mod.shape[0]
    assert d == D_MODEL and t_lat % (2 * CHUNK) == 0 and t_ctx % (2 * CHUNK) == 0 and bsz + 1 <= 8

    cc = jnp.concatenate([c, c_ctx[None, :], jnp.zeros((8 - bsz - 1, d), c.dtype)], axis=0)
    mod = _modulation(cc, w_mod, b_mod)
    consts_lat, consts_ctx = _dft_consts(t_lat), _dft_consts(t_ctx)
    w_t_all = _wprep(w_in, jnp.asarray(consts_lat["cs"]).astype(BF16), w_fno)

    def fourier(xr, xi, consts, bfno):
        if "dense" in consts:
            return _fourier_dense(xr, xi, jnp.asarray(consts["dense"]).astype(BF16), bfno)
        return _fourier_fused(xr, xi, jnp.asarray(consts["f1"]).astype(BF16),
                              jnp.asarray(consts["mtab"]).astype(BF16), bfno, n1=CHUNK)

    cn_zero = jnp.zeros((bsz, UNITS, DQK, 2 * DV), F32)
    m_zero = jnp.zeros((bsz, UNITS, LANES), F32)
    xc = ctx
    for l in range(depth):
        sh_l, sc_l, gt_l = (mod[l, :bsz, i * d:(i + 1) * d].reshape(bsz, 1, d) for i in range(3))
        sh_c, sc_c, gt_c = (jnp.broadcast_to(mod[l, bsz, i * d:(i + 1) * d].reshape(1, 1, d), (bsz, 1, d))
                            for i in range(3))
        b_i, b_f = _gate_bias(b_gate[l])
        gpre = g_pre[l].reshape(1, d)
        wsp = w_sp[l].astype(BF16)
        bsp = jnp.broadcast_to(b_sp[l].T[:, :, None], (CHUNK, GROUPS, GC)).reshape(CHUNK, S_WIDTH)
        wo = w_out[l].astype(BF16)
        tail = (wo, g_hnorm[l].reshape(1, M_WIDTH), g_sgu[l].reshape(1, S_WIDTH), g_post[l].reshape(1, d), wsp, bsp)
        front = functools.partial(_inproj, g_pre=gpre, w_t=w_t_all, b_i=b_i, b_f=b_f, layer=l)

        last = l == depth - 1
        pc = front(xc, sc_c, sh_c, full=not last, tm=256)
        hf_c, hb_c, cn_c, m_c = _mlstm(pc["q"], pc["kT"], pc["v"], pc["cola"], pc["colb"], pc["rowa"], cn_zero, m_zero)
        p = front(x, sc_l, sh_l, full=True, tm=512)
        hf, hb, _, _ = _mlstm(p["q"], p["kT"], p["v"], p["cola"], p["colb"], p["rowa"], cn_c, m_c)
        ym = fourier(p["xr"], p["xi"], consts_lat, b_fno[l])
        x = _outproj(hf, hb, p["ga"], p["vs"], p["gb"], ym, p["szc"], x, gt_l, *tail, tm=1024)
        if not last:
            ymc = fourier(pc["xr"], pc["xi"], consts_ctx, b_fno[l])
            xc = _outproj(hf_c, hb_c, pc["ga"], pc["vs"], pc["gb"], ymc, pc["szc"], xc, gt_c, *tail, tm=256)
    return x
```

```python
import functools

import numpy as np
import jax
import jax.numpy as jnp
from jax.experimental import pallas as pl
from jax.experimental.pallas import tpu as pltpu

EPS = 1e-6
LOG2E = 1.4426950408889634
D_MODEL = 1024
HEADS = 8
DV = 128
DQK = 64
QK_W = HEADS * DQK
M_WIDTH = HEADS * DV
CHUNK = 128
S_WIDTH = 512
F_WIDTH = 512
GROUPS = 4
GC = 128
UNITS = 2 * HEADS
LANES = 128

V7X_VMEM_BYTES = 64 * 1024 * 1024
VMEM_LIMIT = V7X_VMEM_BYTES * 7 // 8

TM_IN_LATENT, TM_OUT_LATENT, TM_CONTEXT = 512, 1024, 256

F32 = jnp.float32
BF16 = jnp.bfloat16


def _cparams(*sem):
    return pltpu.CompilerParams(dimension_semantics=sem, vmem_limit_bytes=VMEM_LIMIT)


def _sigmoid(x):
    return 1.0 / (1.0 + jnp.exp(-x))


def _silu(x):
    return x * _sigmoid(x)


def _mod_kernel(cc_ref, w_ref, b_ref, o_ref):
    s = _silu(cc_ref[...]).astype(BF16)
    w = w_ref[0].astype(BF16)
    o_ref[0] = jnp.dot(s, w, preferred_element_type=F32) + b_ref[0]


def _modulation(cc, w_mod, b_mod):
    depth, d, d3 = w_mod.shape
    return pl.pallas_call(
        _mod_kernel,
        grid=(depth,),
        in_specs=[pl.BlockSpec((8, d), lambda l: (0, 0)),
                  pl.BlockSpec((1, d, d3), lambda l: (l, 0, 0)),
                  pl.BlockSpec((1, 1, d3), lambda l: (l, 0, 0))],
        out_specs=pl.BlockSpec((1, 8, d3), lambda l: (l, 0, 0)),
        out_shape=jax.ShapeDtypeStruct((depth, 8, d3), F32),
        compiler_params=_cparams("arbitrary"),
        name="modulation",
    )(cc, w_mod, b_mod.reshape(depth, 1, d3))


def _log_sigmoid(x):
    return jnp.minimum(x, 0.0) - jnp.log1p(jnp.exp(-jnp.abs(x)))


def _gate_scans(g_t, bi_ref, bf_ref, cola_ref, colb_ref, rowa_ref):
    tm = g_t.shape[1]
    n_c = tm // CHUNK
    assert n_c * UNITS <= LANES
    row = jax.lax.broadcasted_iota(jnp.int32, (CHUNK, LANES), 0)
    lane = jax.lax.broadcasted_iota(jnp.int32, (CHUNK, LANES), 1)
    fwd = lane % UNITS < HEADS
    gi = jnp.zeros((CHUNK, LANES), F32)
    gf = jnp.zeros((CHUNK, LANES), F32)
    pad = jnp.zeros((CHUNK - 2 * UNITS, CHUNK), F32)
    for c in range(n_c):
        t_c = jnp.concatenate([g_t[:, c * CHUNK:(c + 1) * CHUNK], pad], axis=0).T
        sel = lane // UNITS == c
        gi = jnp.where(sel, pltpu.roll(t_c, c * UNITS, axis=1) if c else t_c, gi)
        gf = jnp.where(sel, pltpu.roll(t_c, (c * UNITS - UNITS) % LANES, axis=1), gf)
    gi = gi + bi_ref[...]
    lf = _log_sigmoid(gf + bf_ref[...])

    def scan(x, op):
        pre, suf = x, x
        k = 1
        while k < CHUNK:
            sh = pltpu.roll(pre, k, axis=0)
            pre = jnp.where(row >= k, op(pre, sh), pre)
            sh = pltpu.roll(suf, CHUNK - k, axis=0)
            suf = jnp.where(row < CHUNK - k, op(suf, sh), suf)
            k *= 2
        return jnp.where(fwd, pre, suf)

    b = scan(lf, jnp.add)
    a = (gi - b) * LOG2E
    amax = scan(a, jnp.maximum)
    b = b * LOG2E
    a_t = a.T
    for c in range(n_c):
        rows = slice(c * CHUNK, (c + 1) * CHUNK)
        back = (LANES - c * UNITS) % LANES
        cola_ref[0, rows, :] = pltpu.roll(amax, back, axis=1) if c else amax
        colb_ref[0, rows, :] = pltpu.roll(b, back, axis=1) if c else b
        rowa_ref[0, :, rows] = a_t[c * UNITS:(c + 1) * UNITS, :]


def _pitch(n):
    p = n
    while (p // 8) % 2 == 0:
        p += 8
    return p


def _store_slabs(ref, g, z, n2):
    pin = _pitch(n2)
    for r in range(z.shape[0] // n2):
        ref[0, g, r * pin:r * pin + n2, :] = z[r * n2:(r + 1) * n2]
        if pin > n2:
            ref[0, g, r * pin + n2:(r + 1) * pin, :] = jnp.zeros((pin - n2, GC), F32)


def _inproj_kernel(x_ref, sc_ref, sh_ref, g_ref, wt_ref, bi_ref, bf_ref, *out_refs, pieces, slab_n2):
    nt = (((1,), (1,)), ((), ()))
    x = x_ref[0]
    y = x * jax.lax.rsqrt(jnp.mean(x * x, axis=-1, keepdims=True) + EPS)
    h = y * (g_ref[...] * (1.0 + sc_ref[0])) + sh_ref[0]
    hb = h.astype(BF16)

    def silu(z):
        t = z * 0.5
        return t + t * jnp.tanh(t)

    held = {}
    oi = 0
    for name, off, width in pieces:
        if name == "gates":
            gate_rows = QK_W + off
            w_kg = jnp.concatenate([wt_ref[0:QK_W, :], wt_ref[gate_rows:gate_rows + UNITS, :],
                                    wt_ref[gate_rows + LANES:gate_rows + LANES + UNITS, :]], axis=0)
            r = jax.lax.dot_general(w_kg, hb, nt, preferred_element_type=F32)
            out_refs[-1][0] = r[:QK_W].astype(BF16)
            _gate_scans(r[QK_W:], bi_ref, bf_ref, *out_refs[oi:oi + 3])
            oi += 3
            continue
        if name == "kT":
            continue
        r = jax.lax.dot_general(hb, wt_ref[QK_W + off:QK_W + off + width, :], nt, preferred_element_type=F32)
        if name == "f":
            xr_ref, xi_ref = out_refs[oi], out_refs[oi + 1]
            for g in range(GROUPS):
                zr, zi = r[:, 2 * g * GC:(2 * g + 1) * GC], r[:, (2 * g + 1) * GC:(2 * g + 2) * GC]
                if slab_n2 is None:
                    xr_ref[0, :, g * GC:(g + 1) * GC] = zr.astype(BF16)
                    xi_ref[0, :, g * GC:(g + 1) * GC] = zi.astype(BF16)
                else:
                    _store_slabs(xr_ref, g, zr, slab_n2)
                    _store_slabs(xi_ref, g, zi, slab_n2)
            oi += 2
        elif name in _HELD:
            held[name] = r
        else:
            if name == "za":
                r = (0.5 + 0.5 * jnp.tanh(held["o"] * 0.5)) * silu(r)
            elif name == "zb":
                r = held["u"] * silu(r)
            elif name == "zc":
                r = silu(r)
            out_refs[oi][0] = r.astype(BF16)
            oi += 1


_FULL_PIECES = (("q", QK_W), ("v", M_WIDTH), ("gates", 2 * LANES), ("o", M_WIDTH), ("za", M_WIDTH),
                ("u", S_WIDTH), ("vs", S_WIDTH), ("zb", S_WIDTH), ("f", 2 * F_WIDTH), ("zc", F_WIDTH))
_STATE_PIECES = _FULL_PIECES[:3]
_HELD = ("o", "u")
_GATED_NAME = {"za": "ga", "zb": "gb", "zc": "szc"}
_WT_BLK = 512


def _wrows(width):
    return -(-width // _WT_BLK) * _WT_BLK


def _inproj(x, sc, sh, g_pre, w_t, b_i, b_f, *, layer, full, tm):
    bsz, t, d = x.shape
    names = _FULL_PIECES if full else _STATE_PIECES
    slab_n2 = t // CHUNK if t > 2 * CHUNK else None
    row_spec = lambda width: pl.BlockSpec((1, tm, width), lambda b, i: (b, i, 0))
    pieces, off = [], 0
    out_names, out_shapes, out_specs = [], [], []
    for name, width in names:
        pieces.append((name, off, width))
        off += _wrows(width)
        if name == "gates":
            out_names += ["cola", "colb", "rowa"]
            out_shapes += [jax.ShapeDtypeStruct((bsz, t, LANES), F32)] * 2 + [jax.ShapeDtypeStruct((bsz, UNITS, t), F32)]
            out_specs += [row_spec(LANES), row_spec(LANES), pl.BlockSpec((1, UNITS, tm), lambda b, i: (b, 0, i))]
        elif name == "f":
            out_names += ["xr", "xi"]
            if slab_n2 is None:
                out_shapes += [jax.ShapeDtypeStruct((bsz, t, F_WIDTH), BF16)] * 2
                out_specs += [row_spec(F_WIDTH)] * 2
            else:
                assert tm % slab_n2 == 0
                pin = _pitch(slab_n2)
                out_shapes += [jax.ShapeDtypeStruct((bsz, GROUPS, CHUNK * pin, GC), F32)] * 2
                out_specs += [pl.BlockSpec((1, GROUPS, tm // slab_n2 * pin, GC), lambda b, i: (b, 0, i, 0))] * 2
        elif name not in _HELD:
            out_names.append(_GATED_NAME.get(name, name))
            out_shapes.append(jax.ShapeDtypeStruct((bsz, t, width), BF16))
            out_specs.append(row_spec(width))
    n_nat = off
    pieces.append(("kT", 0, QK_W))
    out_names.append("kT")
    out_shapes.append(jax.ShapeDtypeStruct((bsz, QK_W, t), BF16))
    out_specs.append(pl.BlockSpec((1, QK_W, tm), lambda b, i: (b, 0, i)))
    const = lambda shape, **kw: pl.BlockSpec(shape, lambda b, i: (0,) * len(shape), **kw)
    outs = pl.pallas_call(
        functools.partial(_inproj_kernel, pieces=tuple(pieces), slab_n2=slab_n2),
        grid=(bsz, t // tm),
        in_specs=[pl.BlockSpec((1, tm, d), lambda b, i: (b, i, 0)),
                  pl.BlockSpec((1, 1, d), lambda b, i: (b, 0, 0)),
                  pl.BlockSpec((1, 1, d), lambda b, i: (b, 0, 0)),
                  const((1, d)),
                  pl.BlockSpec((None, QK_W + n_nat, d), lambda b, i: (layer, 0, 0), pipeline_mode=pl.Buffered(1)),
                  const((1, LANES)), const((1, LANES))],
        out_specs=out_specs,
        out_shape=out_shapes,
        compiler_params=_cparams("parallel", "parallel"),
        name="inproj_full" if full else "inproj_state",
    )(x, sc, sh, g_pre, w_t, b_i, b_f)
    return dict(zip(out_names, outs))


def _mlstm_unit(h, d, sub, q_ref, kt_ref, v_ref, rowa_ref, m_diag, g_cols, c_cols, g_ends, hout_ref, cn_ref, mask):
    i = d * HEADS + h
    pair = h // 2
    rows = slice(sub * CHUNK, (sub + 1) * CHUNK)
    q_pair = q_ref[0, rows, pair * LANES:(pair + 1) * LANES]
    kt_h = kt_ref[0, h * DQK:(h + 1) * DQK, rows]
    zk = jnp.zeros((DQK, CHUNK), BF16)
    kt_ext = jnp.concatenate([kt_h, zk] if h % 2 == 0 else [zk, kt_h], axis=0)
    v_h = v_ref[0, rows, h * DV:(h + 1) * DV]
    vaug = jnp.concatenate([v_h, jnp.ones((CHUNK, DV), BF16)], axis=1)
    g = jnp.broadcast_to(g_cols[:, i:i + 1], (CHUNK, CHUNK))
    c = jnp.broadcast_to(c_cols[:, i:i + 1], (CHUNK, CHUNK))
    g_end = jnp.broadcast_to(g_ends[:, i:i + 1], (1, LANES))
    m_row = jnp.broadcast_to(m_diag[:, i:i + 1], (1, LANES))
    a_row = rowa_ref[0, h:h + 1, rows]
    cn = cn_ref[0, i]

    dmat = jnp.where(mask, jnp.exp2(a_row - g), 0.0)
    s = jnp.dot(q_pair, kt_ext, preferred_element_type=F32)
    p = (s * dmat).astype(BF16)
    qs = q_pair * jnp.exp2(m_row - g).astype(BF16)
    lhs = jnp.concatenate([p, qs], axis=1)
    zc = jnp.zeros((DQK, 2 * DV), BF16)
    cnb = cn.astype(BF16)
    cn_ext = jnp.concatenate([cnb, zc] if h % 2 == 0 else [zc, cnb], axis=0)
    rhs = jnp.concatenate([vaug, cn_ext], axis=0)
    out = jnp.dot(lhs, rhs, preferred_element_type=F32)
    num, den = out[:, :DV], out[:, DV:]
    hval = num / jnp.maximum(jnp.abs(den), jnp.exp2(c))
    hout_ref[0, rows, h * DV:(h + 1) * DV] = hval.astype(hout_ref.dtype)

    kts = kt_h * jnp.exp2(a_row - g_end).astype(BF16)
    upd = jnp.dot(kts, vaug, preferred_element_type=F32)
    decay = jnp.exp2(m_row - g_end)
    cn_ref[0, i] = jnp.concatenate([decay, decay], axis=1) * cn + upd


MLSTM_SUB = 8


def _mlstm_kernel(qf_ref, ktf_ref, vf_ref, caf_ref, cbf_ref, raf_ref,
                  qb_ref, ktb_ref, vb_ref, cab_ref, cbb_ref, rab_ref,
                  cn0_ref, m0_ref, hf_ref, hb_ref, cn_ref, m_ref, *, n_sub):
    @pl.when(pl.program_id(1) == 0)
    def _():
        cn_ref[...] = cn0_ref[...]
        m_ref[...] = m0_ref[...]

    t_idx = jax.lax.broadcasted_iota(jnp.int32, (CHUNK, CHUNK), 0)
    s_idx = jax.lax.broadcasted_iota(jnp.int32, (CHUNK, CHUNK), 1)
    unit_row = jax.lax.broadcasted_iota(jnp.int32, (UNITS, LANES), 0)
    unit_lane = jax.lax.broadcasted_iota(jnp.int32, (UNITS, LANES), 1)
    m_diag = jnp.sum(jnp.where(unit_row == unit_lane, m_ref[0], 0.0), axis=0, keepdims=True)
    fwd_lane = jax.lax.broadcasted_iota(jnp.int32, (1, LANES), 1) < HEADS

    def columns(ca_ref, cb_ref, sub, end):
        rows = slice(sub * CHUNK, (sub + 1) * CHUNK)
        g_cols = jnp.maximum(m_diag, ca_ref[0, rows, :])
        g_ends = g_cols[end:end + 1, :]
        m_new = cb_ref[0, sub * CHUNK + end:sub * CHUNK + end + 1, :] + g_ends
        return (g_cols, -(cb_ref[0, rows, :] + g_cols), g_ends), m_new

    for step in range(n_sub):
        sub_f, sub_b = step, n_sub - 1 - step
        cols_f, m_new_f = columns(caf_ref, cbf_ref, sub_f, CHUNK - 1)
        cols_b, m_new_b = columns(cab_ref, cbb_ref, sub_b, 0)
        for h in range(HEADS):
            _mlstm_unit(h, 0, sub_f, qf_ref, ktf_ref, vf_ref, raf_ref, m_diag, *cols_f, hf_ref, cn_ref, s_idx <= t_idx)
            _mlstm_unit(h, 1, sub_b, qb_ref, ktb_ref, vb_ref, rab_ref, m_diag, *cols_b, hb_ref, cn_ref, s_idx >= t_idx)
        m_diag = jnp.where(fwd_lane, m_new_f, m_new_b)
    m_ref[0] = jnp.where(unit_row == unit_lane, jnp.broadcast_to(m_diag, (UNITS, LANES)), 0.0)


def _mlstm(q, kt, v, cola, colb, rowa, cn0, m0):
    bsz, t, _ = q.shape
    n_sub = min(MLSTM_SUB, t // CHUNK)
    blk = n_sub * CHUNK
    nc = t // blk

    def specs(rev):
        cj = (lambda j: nc - 1 - j) if rev else (lambda j: j)
        d = 1 if rev else 0
        return [pl.BlockSpec((1, blk, QK_W), lambda b, j: (b, cj(j), 0)),
                pl.BlockSpec((1, QK_W, blk), lambda b, j: (b, 0, cj(j))),
                pl.BlockSpec((1, blk, M_WIDTH), lambda b, j: (b, cj(j), 0)),
                pl.BlockSpec((1, blk, LANES), lambda b, j: (b, cj(j), 0)),
                pl.BlockSpec((1, blk, LANES), lambda b, j: (b, cj(j), 0)),
                pl.BlockSpec((1, HEADS, blk), lambda b, j: (b, d, cj(j)))]

    cn_spec = pl.BlockSpec((1, UNITS, DQK, 2 * DV), lambda b, j: (b, 0, 0, 0))
    m_spec = pl.BlockSpec((1, UNITS, LANES), lambda b, j: (b, 0, 0))
    args = (q, kt, v, cola, colb, rowa)
    return pl.pallas_call(
        functools.partial(_mlstm_kernel, n_sub=n_sub),
        grid=(bsz, nc),
        in_specs=specs(False) + specs(True) + [cn_spec, m_spec],
        out_specs=[pl.BlockSpec((1, blk, M_WIDTH), lambda b, j: (b, j, 0)),
                   pl.BlockSpec((1, blk, M_WIDTH), lambda b, j: (b, nc - 1 - j, 0)),
                   cn_spec, m_spec],
        out_shape=[jax.ShapeDtypeStruct((bsz, t, M_WIDTH), BF16),
                   jax.ShapeDtypeStruct((bsz, t, M_WIDTH), BF16),
                   jax.ShapeDtypeStruct((bsz, UNITS, DQK, 2 * DV), F32),
                   jax.ShapeDtypeStruct((bsz, UNITS, LANES), F32)],
        compiler_params=_cparams("parallel", "arbitrary"),
        name="mlstm",
    )(*args, *args, cn0, m0)


OUT_PITCH = _pitch(CHUNK)


def _fourier_dense_kernel(xr_ref, xi_ref, cst_ref, bf_ref, o_ref):
    xx = jnp.concatenate([xr_ref[0], xi_ref[0]], axis=0)
    yr = jnp.dot(cst_ref[...], xx, preferred_element_type=F32)
    o_ref[0] = (yr + bf_ref[...]).astype(BF16)


def _fourier_dense(xr, xi, cst, bfno):
    bsz, t, w = xr.shape
    blk = pl.BlockSpec((1, t, w), lambda b: (b, 0, 0))
    return pl.pallas_call(
        _fourier_dense_kernel,
        grid=(bsz,),
        in_specs=[blk, blk,
                  pl.BlockSpec((t, 2 * t), lambda b: (0, 0)),
                  pl.BlockSpec((1, w), lambda b: (0, 0))],
        out_specs=blk,
        out_shape=jax.ShapeDtypeStruct((bsz, t, w), BF16),
        compiler_params=_cparams("parallel"),
        name="fourier_dense",
    )(xr, xi, cst, bfno.reshape(1, w))


def _fourier_fused_kernel(xr_ref, xi_ref, f1_ref, m_ref, bf_ref, o_ref, gr_scr, gi_scr, y_scr, *, n1, n2):
    pin = _pitch(n2)

    def stage1(j, carry):
        t2 = 2 * j

        def ld(ref, s):
            return ref[0, 0, pl.ds(s, n1, stride=pin), :].astype(BF16)

        top = jnp.concatenate([ld(xr_ref, t2), ld(xr_ref, t2 + 1)], axis=1)
        bot = jnp.concatenate([ld(xi_ref, t2), ld(xi_ref, t2 + 1)], axis=1)
        g = jnp.dot(f1_ref[...], jnp.concatenate([top, bot], axis=0), preferred_element_type=F32)
        gr_scr[pl.ds(t2, n1, stride=pin), :] = g[:n1, :GC]
        gr_scr[pl.ds(t2 + 1, n1, stride=pin), :] = g[:n1, GC:]
        gi_scr[pl.ds(t2, n1, stride=pin), :] = g[n1:, :GC]
        gi_scr[pl.ds(t2 + 1, n1, stride=pin), :] = g[n1:, GC:]
        return carry

    jax.lax.fori_loop(0, n2 // 2, stage1, 0, unroll=True)

    def stage2(k1, carry):
        base = pl.multiple_of(k1 * pin, 8)
        gg = jnp.concatenate([gr_scr[pl.ds(base, n2), :], gi_scr[pl.ds(base, n2), :]], axis=0).astype(BF16)
        yr = jnp.dot(m_ref[k1], gg, preferred_element_type=F32) + bf_ref[0]
        y_scr[pl.ds(k1, n2, stride=OUT_PITCH), :] = yr
        return carry

    jax.lax.fori_loop(0, n1, stage2, 0, unroll=True)
    for k2 in range(n2):
        o_ref[0, k2 * n1:(k2 + 1) * n1, :] = y_scr[k2 * OUT_PITCH:k2 * OUT_PITCH + n1, :].astype(BF16)


def _fourier_fused(xr, xi, f1, mtab, bfno, *, n1):
    bsz, _, rows_in, _ = xr.shape
    n2 = mtab.shape[1]
    pin = _pitch(n2)
    assert rows_in == n1 * pin and n1 == CHUNK
    in_blk = pl.BlockSpec((1, 1, rows_in, GC), lambda b, g: (b, g, 0, 0))
    return pl.pallas_call(
        functools.partial(_fourier_fused_kernel, n1=n1, n2=n2),
        grid=(bsz, GROUPS),
        in_specs=[in_blk, in_blk,
                  pl.BlockSpec((2 * n1, 2 * n1), lambda b, g: (0, 0)),
                  pl.BlockSpec((n1, n2, 2 * n2), lambda b, g: (0, 0, 0)),
                  pl.BlockSpec((1, 1, GC), lambda b, g: (g, 0, 0))],
        out_specs=pl.BlockSpec((1, n1 * n2, GC), lambda b, g: (b, 0, g)),
        out_shape=jax.ShapeDtypeStruct((bsz, n1 * n2, GROUPS * GC), BF16),
        scratch_shapes=[pltpu.VMEM((n1 * pin, GC), F32), pltpu.VMEM((n1 * pin, GC), F32),
                        pltpu.VMEM((n2 * OUT_PITCH, GC), F32)],
        compiler_params=_cparams("parallel", "parallel"),
        name="fourier_fused",
    )(xr, xi, f1, mtab, bfno.reshape(GROUPS, 1, GC))


def _dft_consts(t):
    ang = 2.0 * np.pi / GC * np.outer(np.arange(GC), np.arange(GC))
    cs = np.concatenate([np.cos(ang), -np.sin(ang)], axis=0) / np.sqrt(GC)
    out = {"cs": cs.astype(np.float32)}
    if t <= 2 * CHUNK:
        ang = 2.0 * np.pi / t * np.mod(np.outer(np.arange(t), np.arange(t)), t)
        out["dense"] = (np.concatenate([np.cos(ang), np.sin(ang)], axis=1) / np.sqrt(t)).astype(np.float32)
    else:
        n1 = CHUNK
        n2 = t // n1
        ang = 2.0 * np.pi / n1 * np.mod(np.outer(np.arange(n1), np.arange(n1)), n1)
        c, s = np.cos(ang), np.sin(ang)
        out["f1"] = (np.block([[c, s], [-s, c]]) / np.sqrt(n1)).astype(np.float32)
        k = np.arange(n1)[:, None, None] + n1 * np.arange(n2)[None, :, None]
        ang = 2.0 * np.pi / t * np.mod(k * np.arange(n2)[None, None, :], t)
        out["mtab"] = (np.concatenate([np.cos(ang), np.sin(ang)], axis=2) / np.sqrt(n2)).astype(np.float32)
    return out


def _outproj_kernel(hf_ref, hb_ref, ga_ref, vs_ref, gb_ref, ym_ref, szc_ref, x_ref, gt_ref,
                    wout_ref, ghn_ref, gsgu_ref, gpost_ref, wsp_ref, bsp_ref, xo_ref, y_scr):
    tm = x_ref.shape[1]
    kc = S_WIDTH

    def project(k):
        return jnp.dot(y_scr[:, k * kc:(k + 1) * kc], wout_ref[k * kc:(k + 1) * kc, :], preferred_element_type=F32)

    out = None
    for h in range(HEADS):
        sl = slice(h * DV, (h + 1) * DV)
        hh = (hf_ref[0, :, sl] + hb_ref[0, :, sl]).astype(F32)
        hn = hh * jax.lax.rsqrt(jnp.mean(hh * hh, axis=-1, keepdims=True) + EPS) * ghn_ref[:, sl]
        y_scr[:, sl] = hn.astype(BF16) * ga_ref[0, :, sl]
        if (h + 1) * DV % kc == 0:
            part = project((h + 1) * DV // kc - 1)
            out = part if out is None else out + part
    vs = vs_ref[0].astype(F32)
    vn = (vs * jax.lax.rsqrt(jnp.mean(vs * vs, axis=-1, keepdims=True) + EPS) * gsgu_ref[...]).astype(BF16)
    for c in range(tm // CHUNK):
        rows = slice(c * CHUNK, (c + 1) * CHUNK)
        for g in range(GROUPS):
            cols = slice(g * GC, (g + 1) * GC)
            mixed = jnp.dot(wsp_ref[g], vn[rows, cols], preferred_element_type=F32) + bsp_ref[:, cols]
            y_scr[rows, M_WIDTH + g * GC:M_WIDTH + (g + 1) * GC] = gb_ref[0, rows, cols] * mixed.astype(BF16)
    out = out + project(M_WIDTH // kc)
    for c in range(tm // CHUNK):
        rows = slice(c * CHUNK, (c + 1) * CHUNK)
        for g in range(GROUPS):
            cols = slice(g * GC, (g + 1) * GC)
            y_scr[rows, M_WIDTH + S_WIDTH + g * GC:M_WIDTH + S_WIDTH + (g + 1) * GC] = (
                ym_ref[0, rows, cols] * szc_ref[0, rows, cols])
    out = out + project((M_WIDTH + S_WIDTH) // kc)
    on = out * jax.lax.rsqrt(jnp.mean(out * out, axis=-1, keepdims=True) + EPS) * gpost_ref[...]
    xo_ref[0] = x_ref[0] + gt_ref[0] * on


def _outproj(hf, hb, ga, vs, gb, ym, szc, x, gt, w_out, ghn, gsgu, gpost, wsp, bsp, *, tm):
    bsz, t, d = x.shape
    wide = pl.BlockSpec((1, tm, M_WIDTH), lambda b, i: (b, i, 0))
    half = pl.BlockSpec((1, tm, S_WIDTH), lambda b, i: (b, i, 0))
    const2 = lambda shape: pl.BlockSpec(shape, lambda b, i: (0,) * len(shape))
    return pl.pallas_call(
        _outproj_kernel,
        grid=(bsz, t // tm),
        in_specs=[wide, wide, wide, half, half, half, half, wide,
                  pl.BlockSpec((1, 1, d), lambda b, i: (b, 0, 0)),
                  const2((2 * D_MODEL, d)), const2((1, M_WIDTH)), const2((1, S_WIDTH)), const2((1, d)),
                  const2((GROUPS, CHUNK, CHUNK)), const2((CHUNK, S_WIDTH))],
        out_specs=wide,
        out_shape=jax.ShapeDtypeStruct((bsz, t, d), F32),
        scratch_shapes=[pltpu.VMEM((tm, 2 * D_MODEL), BF16)],
        compiler_params=_cparams("parallel", "parallel"),
        name="outproj",
    )(hf, hb, ga, vs, gb, ym, szc, x, gt, w_out, ghn, gsgu, gpost, wsp, bsp)


_W_IN_SPLITS = np.cumsum([0, QK_W, QK_W, M_WIDTH, 4 * HEADS, M_WIDTH, M_WIDTH, S_WIDTH, S_WIDTH, S_WIDTH, F_WIDTH,
                          F_WIDTH])
_N_NAT = sum(_wrows(w) for _, w in _FULL_PIECES)
_GATE_ROWS = 4 * HEADS


def _wprep_plan():
    c = _W_IN_SPLITS
    starts, q_blocks, gate_block, f_block = [], None, None, None
    for i in (1, 0, 2, 3, 4, 5, 6, 7, 8, 9, 10):
        if i == 3:
            gate_block = len(starts)
            starts.append(0)
            continue
        first = len(starts)
        if i == 9:
            f_block = first
            starts += list(range(int(c[i]), int(c[i + 1]), _WT_BLK // 2))
            continue
        starts += list(range(int(c[i]), int(c[i + 1]), _WT_BLK))
        if i == 0:
            q_blocks = (first, len(starts))
    return starts, q_blocks, gate_block, f_block


F_STEP_GROUPS = _WT_BLK // (2 * GC)


def _wprep_kernel(starts_ref, w_ref, g_ref, cst_ref, wf_ref, o_ref, *, q_blocks, gate_block, f_block):
    del starts_ref
    s = pl.program_id(1)
    n_f = GROUPS // F_STEP_GROUPS

    @pl.when((s != gate_block) & ((s < f_block) | (s >= f_block + n_f)))
    def _():
        scale = jnp.where((s >= q_blocks[0]) & (s < q_blocks[1]), DQK ** -0.5, 1.0)
        o_ref[0] = (w_ref[0] * scale).astype(BF16)

    for k in range(n_f):
        @pl.when(s == f_block + k)
        def _(k=k):
            outs = []
            for gg in range(F_STEP_GROUPS):
                a = w_ref[0, gg * GC:(gg + 1) * GC, :].astype(BF16)
                wc = jnp.dot(cst_ref[...], wf_ref[0, k * F_STEP_GROUPS + gg].astype(BF16),
                             preferred_element_type=F32)
                for half in range(2):
                    cw_t = wc[half * GC:(half + 1) * GC].T.astype(BF16)
                    outs.append(jnp.dot(cw_t, a, preferred_element_type=F32))
            o_ref[0] = jnp.concatenate(outs, axis=0).astype(BF16)

    @pl.when(s == gate_block)
    def _():
        g = g_ref[0]
        zero = jnp.zeros((LANES - UNITS, g.shape[1]), F32)
        tail = jnp.zeros((_WT_BLK - 2 * LANES, g.shape[1]), F32)
        o_ref[0] = jnp.concatenate([g[0:8], g[16:24], zero,
                                    g[8:16], g[24:32], zero,
                                    tail], axis=0).astype(BF16)


def _wprep(w_in, cst, w_fno):
    depth, d, p_in = w_in.shape
    w_t = jnp.swapaxes(w_in, 1, 2)
    starts, q_blocks, gate_block, f_block = _wprep_plan()
    assert _W_IN_SPLITS[3] % _GATE_ROWS == 0 and len(starts) * _WT_BLK == QK_W + _N_NAT
    grid_spec = pltpu.PrefetchScalarGridSpec(
        num_scalar_prefetch=1,
        grid=(depth, len(starts)),
        in_specs=[pl.BlockSpec((pl.Element(1), pl.Element(_WT_BLK), pl.Element(d)),
                               lambda l, s, st: (l, st[s] * 8, 0)),
                  pl.BlockSpec((1, _GATE_ROWS, d), lambda l, s, st: (l, int(_W_IN_SPLITS[3]) // _GATE_ROWS, 0)),
                  pl.BlockSpec((2 * GC, GC), lambda l, s, st: (0, 0)),
                  pl.BlockSpec((1, GROUPS, GC, GC), lambda l, s, st: (l, 0, 0, 0))],
        out_specs=pl.BlockSpec((1, _WT_BLK, d), lambda l, s, st: (l, s, 0)),
    )
    return pl.pallas_call(
        functools.partial(_wprep_kernel, q_blocks=q_blocks, gate_block=gate_block, f_block=f_block),
        grid_spec=grid_spec,
        out_shape=jax.ShapeDtypeStruct((depth, QK_W + _N_NAT, d), BF16),
        compiler_params=_cparams("parallel", "arbitrary"),
        name="wprep",
    )(jnp.asarray(starts, jnp.int32) // 8, w_t, w_t, cst, w_fno)


def _gate_bias(b_gate_l):
    b_i = jnp.tile(jnp.concatenate([b_gate_l[0:8], b_gate_l[16:24]]), LANES // UNITS).reshape(1, LANES)
    b_f = jnp.tile(jnp.concatenate([b_gate_l[8:16], b_gate_l[24:32]]), LANES // UNITS).reshape(1, LANES)
    return b_i, b_f


def kernel(x, c, ctx, c_ctx, w_mod, b_mod, g_pre, g_post, w_in, b_gate, g_hnorm, g_sgu, w_sp, b_sp, w_fno, b_fno, w_out):
    bsz, t_lat, d = x.shape
    t_ctx = ctx.shape[1]
    depth = w_mod.shape[0]
    tm_in, tm_out = min(TM_IN_LATENT, t_lat), min(TM_OUT_LATENT, t_lat)
    assert d == D_MODEL and t_lat % tm_in == 0 and t_lat % tm_out == 0 and t_ctx % TM_CONTEXT == 0 and bsz + 1 <= 8

    cc = jnp.concatenate([c, c_ctx[None, :], jnp.zeros((8 - bsz - 1, d), c.dtype)], axis=0)
    mod = _modulation(cc, w_mod, b_mod)
    consts_lat, consts_ctx = _dft_consts(t_lat), _dft_consts(t_ctx)
    w_t_all = _wprep(w_in, jnp.asarray(consts_lat["cs"]).astype(BF16), w_fno)

    def fourier(xr, xi, consts, bfno):
        if "dense" in consts:
            return _fourier_dense(xr, xi, jnp.asarray(consts["dense"]).astype(BF16), bfno)
        return _fourier_fused(xr, xi, jnp.asarray(consts["f1"]).astype(BF16),
                              jnp.asarray(consts["mtab"]).astype(BF16), bfno, n1=CHUNK)

    cn_zero = jnp.zeros((bsz, UNITS, DQK, 2 * DV), F32)
    m_zero = jnp.zeros((bsz, UNITS, LANES), F32)
    xc = ctx
    for l in range(depth):
        sh_l, sc_l, gt_l = (mod[l, :bsz, i * d:(i + 1) * d].reshape(bsz, 1, d) for i in range(3))
        sh_c, sc_c, gt_c = (jnp.broadcast_to(mod[l, bsz, i * d:(i + 1) * d].reshape(1, 1, d), (bsz, 1, d))
                            for i in range(3))
        b_i, b_f = _gate_bias(b_gate[l])
        gpre = g_pre[l].reshape(1, d)
        wsp = w_sp[l].astype(BF16)
        bsp = jnp.broadcast_to(b_sp[l].T[:, :, None], (CHUNK, GROUPS, GC)).reshape(CHUNK, S_WIDTH)
        wo = w_out[l].astype(BF16)
        tail = (wo, g_hnorm[l].reshape(1, M_WIDTH), g_sgu[l].reshape(1, S_WIDTH), g_post[l].reshape(1, d), wsp, bsp)
        front = functools.partial(_inproj, g_pre=gpre, w_t=w_t_all, b_i=b_i, b_f=b_f, layer=l)

        last = l == depth - 1
        pc = front(xc, sc_c, sh_c, full=not last, tm=TM_CONTEXT)
        hf_c, hb_c, cn_c, m_c = _mlstm(pc["q"], pc["kT"], pc["v"], pc["cola"], pc["colb"], pc["rowa"], cn_zero, m_zero)
        p = front(x, sc_l, sh_l, full=True, tm=tm_in)
        hf, hb, _, _ = _mlstm(p["q"], p["kT"], p["v"], p["cola"], p["colb"], p["rowa"], cn_c, m_c)
        ym = fourier(p["xr"], p["xi"], consts_lat, b_fno[l])
        x = _outproj(hf, hb, p["ga"], p["vs"], p["gb"], ym, p["szc"], x, gt_l, *tail, tm=tm_out)
        if not last:
            ymc = fourier(pc["xr"], pc["xi"], consts_ctx, b_fno[l])
            xc = _outproj(hf_c, hb_c, pc["ga"], pc["vs"], pc["gb"], ymc, pc["szc"], xc, gt_c, *tail, tm=TM_CONTEXT)
    return x
```

```python
import functools

import numpy as np
import jax
import jax.numpy as jnp
from jax.experimental import pallas as pl
from jax.experimental.pallas import tpu as pltpu

EPS = 1e-6
LOG2E = 1.4426950408889634
D_MODEL = 1024
HEADS = 8
DV = 128
DQK = 64
QK_W = HEADS * DQK
M_WIDTH = HEADS * DV
CHUNK = 128
S_WIDTH = 512
F_WIDTH = 512
GROUPS = 4
GC = 128
UNITS = 2 * HEADS
LANES = 128

V7X_VMEM_BYTES = 64 * 1024 * 1024
VMEM_LIMIT = V7X_VMEM_BYTES * 7 // 8

TM_IN_LATENT, TM_OUT_LATENT, TM_CONTEXT = 512, 1024, 256

F32 = jnp.float32
BF16 = jnp.bfloat16


def _cparams(*sem):
    return pltpu.CompilerParams(dimension_semantics=sem, vmem_limit_bytes=VMEM_LIMIT)


def _sigmoid(x):
    return 1.0 / (1.0 + jnp.exp(-x))


def _silu(x):
    return x * _sigmoid(x)


def _mod_kernel(cc_ref, w_ref, b_ref, o_ref):
    s = _silu(cc_ref[...]).astype(BF16)
    w = w_ref[0].astype(BF16)
    o_ref[0] = jnp.dot(s, w, preferred_element_type=F32) + b_ref[0]


def _modulation(cc, w_mod, b_mod):
    depth, d, d3 = w_mod.shape
    return pl.pallas_call(
        _mod_kernel,
        grid=(depth,),
        in_specs=[pl.BlockSpec((8, d), lambda l: (0, 0)),
                  pl.BlockSpec((1, d, d3), lambda l: (l, 0, 0)),
                  pl.BlockSpec((1, 1, d3), lambda l: (l, 0, 0))],
        out_specs=pl.BlockSpec((1, 8, d3), lambda l: (l, 0, 0)),
        out_shape=jax.ShapeDtypeStruct((depth, 8, d3), F32),
        compiler_params=_cparams("arbitrary"),
        name="modulation",
    )(cc, w_mod, b_mod.reshape(depth, 1, d3))


def _log_sigmoid(x):
    return jnp.minimum(x, 0.0) - jnp.log1p(jnp.exp(-jnp.abs(x)))


def _gate_scans(g_t, bi_ref, bf_ref, cola_ref, colb_ref, rowa_ref):
    tm = g_t.shape[1]
    n_c = tm // CHUNK
    assert n_c * UNITS <= LANES
    row = jax.lax.broadcasted_iota(jnp.int32, (CHUNK, LANES), 0)
    lane = jax.lax.broadcasted_iota(jnp.int32, (CHUNK, LANES), 1)
    fwd = lane % UNITS < HEADS
    gi = jnp.zeros((CHUNK, LANES), F32)
    gf = jnp.zeros((CHUNK, LANES), F32)
    pad = jnp.zeros((CHUNK - 2 * UNITS, CHUNK), F32)
    for c in range(n_c):
        t_c = jnp.concatenate([g_t[:, c * CHUNK:(c + 1) * CHUNK], pad], axis=0).T
        sel = lane // UNITS == c
        gi = jnp.where(sel, pltpu.roll(t_c, c * UNITS, axis=1) if c else t_c, gi)
        gf = jnp.where(sel, pltpu.roll(t_c, (c * UNITS - UNITS) % LANES, axis=1), gf)
    gi = gi + bi_ref[...]
    lf = _log_sigmoid(gf + bf_ref[...])

    def scan(x, op):
        pre, suf = x, x
        k = 1
        while k < CHUNK:
            sh = pltpu.roll(pre, k, axis=0)
            pre = jnp.where(row >= k, op(pre, sh), pre)
            sh = pltpu.roll(suf, CHUNK - k, axis=0)
            suf = jnp.where(row < CHUNK - k, op(suf, sh), suf)
            k *= 2
        return jnp.where(fwd, pre, suf)

    b = scan(lf, jnp.add)
    a = (gi - b) * LOG2E
    amax = scan(a, jnp.maximum)
    b = b * LOG2E
    a_t = a.T
    for c in range(n_c):
        rows = slice(c * CHUNK, (c + 1) * CHUNK)
        back = (LANES - c * UNITS) % LANES
        cola_ref[0, rows, :] = pltpu.roll(amax, back, axis=1) if c else amax
        colb_ref[0, rows, :] = pltpu.roll(b, back, axis=1) if c else b
        rowa_ref[0, :, rows] = a_t[c * UNITS:(c + 1) * UNITS, :]


def _pitch(n):
    p = n
    while (p // 8) % 2 == 0:
        p += 8
    return p


def _store_slabs(ref, g, z, n2):
    pin = _pitch(n2)
    for r in range(z.shape[0] // n2):
        ref[0, g, r * pin:r * pin + n2, :] = z[r * n2:(r + 1) * n2]
        if pin > n2:
            ref[0, g, r * pin + n2:(r + 1) * pin, :] = jnp.zeros((pin - n2, GC), F32)


def _inproj_kernel(x_ref, sc_ref, sh_ref, g_ref, wt_ref, bi_ref, bf_ref, *out_refs, pieces, slab_n2):
    nt = (((1,), (1,)), ((), ()))
    x = x_ref[0]
    y = x * jax.lax.rsqrt(jnp.mean(x * x, axis=-1, keepdims=True) + EPS)
    h = y * (g_ref[...] * (1.0 + sc_ref[0])) + sh_ref[0]
    hb = h.astype(BF16)

    def silu(z):
        t = z * 0.5
        return t + t * jnp.tanh(t)

    held = {}
    oi = 0
    for name, off, width in pieces:
        if name == "gates":
            gate_rows = QK_W + off
            w_kg = jnp.concatenate([wt_ref[0:QK_W, :], wt_ref[gate_rows:gate_rows + UNITS, :],
                                    wt_ref[gate_rows + LANES:gate_rows + LANES + UNITS, :]], axis=0)
            r = jax.lax.dot_general(w_kg, hb, nt, preferred_element_type=F32)
            out_refs[-1][0] = r[:QK_W].astype(BF16)
            _gate_scans(r[QK_W:], bi_ref, bf_ref, *out_refs[oi:oi + 3])
            oi += 3
            continue
        if name == "kT":
            continue
        r = jax.lax.dot_general(hb, wt_ref[QK_W + off:QK_W + off + width, :], nt, preferred_element_type=F32)
        if name == "f":
            xr_ref, xi_ref = out_refs[oi], out_refs[oi + 1]
            for g in range(GROUPS):
                zr, zi = r[:, 2 * g * GC:(2 * g + 1) * GC], r[:, (2 * g + 1) * GC:(2 * g + 2) * GC]
                if slab_n2 is None:
                    xr_ref[0, :, g * GC:(g + 1) * GC] = zr.astype(BF16)
                    xi_ref[0, :, g * GC:(g + 1) * GC] = zi.astype(BF16)
                else:
                    _store_slabs(xr_ref, g, zr, slab_n2)
                    _store_slabs(xi_ref, g, zi, slab_n2)
            oi += 2
        elif name in _HELD:
            held[name] = r
        else:
            if name == "za":
                r = (0.5 + 0.5 * jnp.tanh(held["o"] * 0.5)) * silu(r)
            elif name == "zb":
                r = held["u"] * silu(r)
            elif name == "zc":
                r = silu(r)
            out_refs[oi][0] = r.astype(BF16)
            oi += 1


_FULL_PIECES = (("q", QK_W), ("v", M_WIDTH), ("gates", 2 * LANES), ("o", M_WIDTH), ("za", M_WIDTH),
                ("u", S_WIDTH), ("vs", S_WIDTH), ("zb", S_WIDTH), ("f", 2 * F_WIDTH), ("zc", F_WIDTH))
_STATE_PIECES = _FULL_PIECES[:3]
_HELD = ("o", "u")
_GATED_NAME = {"za": "ga", "zb": "gb", "zc": "szc"}
_WT_BLK = 512


def _wrows(width):
    return -(-width // _WT_BLK) * _WT_BLK


def _inproj(x, sc, sh, g_pre, w_t, b_i, b_f, *, layer, full, tm):
    bsz, t, d = x.shape
    names = _FULL_PIECES if full else _STATE_PIECES
    slab_n2 = t // CHUNK if t > 2 * CHUNK else None
    row_spec = lambda width: pl.BlockSpec((1, tm, width), lambda b, i: (b, i, 0))
    pieces, off = [], 0
    out_names, out_shapes, out_specs = [], [], []
    for name, width in names:
        pieces.append((name, off, width))
        off += _wrows(width)
        if name == "gates":
            out_names += ["cola", "colb", "rowa"]
            out_shapes += [jax.ShapeDtypeStruct((bsz, t, LANES), F32)] * 2 + [jax.ShapeDtypeStruct((bsz, UNITS, t), F32)]
            out_specs += [row_spec(LANES), row_spec(LANES), pl.BlockSpec((1, UNITS, tm), lambda b, i: (b, 0, i))]
        elif name == "f":
            out_names += ["xr", "xi"]
            if slab_n2 is None:
                out_shapes += [jax.ShapeDtypeStruct((bsz, t, F_WIDTH), BF16)] * 2
                out_specs += [row_spec(F_WIDTH)] * 2
            else:
                assert tm % slab_n2 == 0
                pin = _pitch(slab_n2)
                out_shapes += [jax.ShapeDtypeStruct((bsz, GROUPS, CHUNK * pin, GC), F32)] * 2
                out_specs += [pl.BlockSpec((1, GROUPS, tm // slab_n2 * pin, GC), lambda b, i: (b, 0, i, 0))] * 2
        elif name not in _HELD:
            out_names.append(_GATED_NAME.get(name, name))
            out_shapes.append(jax.ShapeDtypeStruct((bsz, t, width), BF16))
            out_specs.append(row_spec(width))
    n_nat = off
    pieces.append(("kT", 0, QK_W))
    out_names.append("kT")
    out_shapes.append(jax.ShapeDtypeStruct((bsz, QK_W, t), BF16))
    out_specs.append(pl.BlockSpec((1, QK_W, tm), lambda b, i: (b, 0, i)))
    const = lambda shape, **kw: pl.BlockSpec(shape, lambda b, i: (0,) * len(shape), **kw)
    outs = pl.pallas_call(
        functools.partial(_inproj_kernel, pieces=tuple(pieces), slab_n2=slab_n2),
        grid=(bsz, t // tm),
        in_specs=[pl.BlockSpec((1, tm, d), lambda b, i: (b, i, 0)),
                  pl.BlockSpec((1, 1, d), lambda b, i: (b, 0, 0)),
                  pl.BlockSpec((1, 1, d), lambda b, i: (b, 0, 0)),
                  const((1, d)),
                  pl.BlockSpec((None, QK_W + n_nat, d), lambda b, i: (layer, 0, 0), pipeline_mode=pl.Buffered(1)),
                  const((1, LANES)), const((1, LANES))],
        out_specs=out_specs,
        out_shape=out_shapes,
        compiler_params=_cparams("parallel", "parallel"),
        name="inproj_full" if full else "inproj_state",
    )(x, sc, sh, g_pre, w_t, b_i, b_f)
    return dict(zip(out_names, outs))


def _mlstm_unit(h, d, sub, q_ref, kt_ref, v_ref, rowa_ref, m_diag, g_cols, c_cols, g_ends, hout_ref, cn_ref, mask):
    i = d * HEADS + h
    pair = h // 2
    rows = slice(sub * CHUNK, (sub + 1) * CHUNK)
    q_pair = q_ref[0, rows, pair * LANES:(pair + 1) * LANES]
    kt_h = kt_ref[0, h * DQK:(h + 1) * DQK, rows]
    zk = jnp.zeros((DQK, CHUNK), BF16)
    kt_ext = jnp.concatenate([kt_h, zk] if h % 2 == 0 else [zk, kt_h], axis=0)
    v_h = v_ref[0, rows, h * DV:(h + 1) * DV]
    vaug = jnp.concatenate([v_h, jnp.ones((CHUNK, DV), BF16)], axis=1)
    g = jnp.broadcast_to(g_cols[:, i:i + 1], (CHUNK, CHUNK))
    c = jnp.broadcast_to(c_cols[:, i:i + 1], (CHUNK, CHUNK))
    g_end = jnp.broadcast_to(g_ends[:, i:i + 1], (1, LANES))
    m_row = jnp.broadcast_to(m_diag[:, i:i + 1], (1, LANES))
    a_row = rowa_ref[0, h:h + 1, rows]
    cn = cn_ref[0, i]

    dmat = jnp.where(mask, jnp.exp2(a_row - g), 0.0)
    s = jnp.dot(q_pair, kt_ext, preferred_element_type=F32)
    p = (s * dmat).astype(BF16)
    qs = q_pair * jnp.exp2(m_row - g).astype(BF16)
    lhs = jnp.concatenate([p, qs], axis=1)
    zc = jnp.zeros((DQK, 2 * DV), BF16)
    cnb = cn.astype(BF16)
    cn_ext = jnp.concatenate([cnb, zc] if h % 2 == 0 else [zc, cnb], axis=0)
    rhs = jnp.concatenate([vaug, cn_ext], axis=0)
    out = jnp.dot(lhs, rhs, preferred_element_type=F32)
    num, den = out[:, :DV], out[:, DV:]
    hval = num / jnp.maximum(jnp.abs(den), jnp.exp2(c))
    hout_ref[0, rows, h * DV:(h + 1) * DV] = hval.astype(hout_ref.dtype)

    kts = kt_h * jnp.exp2(a_row - g_end).astype(BF16)
    upd = jnp.dot(kts, vaug, preferred_element_type=F32)
    decay = jnp.exp2(m_row - g_end)
    cn_ref[0, i] = jnp.concatenate([decay, decay], axis=1) * cn + upd


MLSTM_SUB = 8


def _mlstm_kernel(qf_ref, ktf_ref, vf_ref, caf_ref, cbf_ref, raf_ref,
                  qb_ref, ktb_ref, vb_ref, cab_ref, cbb_ref, rab_ref,
                  cn0_ref, m0_ref, hf_ref, hb_ref, cn_ref, m_ref, *, n_sub):
    @pl.when(pl.program_id(1) == 0)
    def _():
        cn_ref[...] = cn0_ref[...]
        m_ref[...] = m0_ref[...]

    t_idx = jax.lax.broadcasted_iota(jnp.int32, (CHUNK, CHUNK), 0)
    s_idx = jax.lax.broadcasted_iota(jnp.int32, (CHUNK, CHUNK), 1)
    unit_row = jax.lax.broadcasted_iota(jnp.int32, (UNITS, LANES), 0)
    unit_lane = jax.lax.broadcasted_iota(jnp.int32, (UNITS, LANES), 1)
    m_diag = jnp.sum(jnp.where(unit_row == unit_lane, m_ref[0], 0.0), axis=0, keepdims=True)
    fwd_lane = jax.lax.broadcasted_iota(jnp.int32, (1, LANES), 1) < HEADS

    def columns(ca_ref, cb_ref, sub, end):
        rows = slice(sub * CHUNK, (sub + 1) * CHUNK)
        g_cols = jnp.maximum(m_diag, ca_ref[0, rows, :])
        g_ends = g_cols[end:end + 1, :]
        m_new = cb_ref[0, sub * CHUNK + end:sub * CHUNK + end + 1, :] + g_ends
        return (g_cols, -(cb_ref[0, rows, :] + g_cols), g_ends), m_new

    for step in range(n_sub):
        sub_f, sub_b = step, n_sub - 1 - step
        cols_f, m_new_f = columns(caf_ref, cbf_ref, sub_f, CHUNK - 1)
        cols_b, m_new_b = columns(cab_ref, cbb_ref, sub_b, 0)
        for h in range(HEADS):
            _mlstm_unit(h, 0, sub_f, qf_ref, ktf_ref, vf_ref, raf_ref, m_diag, *cols_f, hf_ref, cn_ref, s_idx <= t_idx)
            _mlstm_unit(h, 1, sub_b, qb_ref, ktb_ref, vb_ref, rab_ref, m_diag, *cols_b, hb_ref, cn_ref, s_idx >= t_idx)
        m_diag = jnp.where(fwd_lane, m_new_f, m_new_b)
    m_ref[0] = jnp.where(unit_row == unit_lane, jnp.broadcast_to(m_diag, (UNITS, LANES)), 0.0)


def _mlstm(q, kt, v, cola, colb, rowa, cn0, m0):
    bsz, t, _ = q.shape
    n_sub = min(MLSTM_SUB, t // CHUNK)
    blk = n_sub * CHUNK
    nc = t // blk

    def specs(rev):
        cj = (lambda j: nc - 1 - j) if rev else (lambda j: j)
        d = 1 if rev else 0
        return [pl.BlockSpec((1, blk, QK_W), lambda b, j: (b, cj(j), 0)),
                pl.BlockSpec((1, QK_W, blk), lambda b, j: (b, 0, cj(j))),
                pl.BlockSpec((1, blk, M_WIDTH), lambda b, j: (b, cj(j), 0)),
                pl.BlockSpec((1, blk, LANES), lambda b, j: (b, cj(j), 0)),
                pl.BlockSpec((1, blk, LANES), lambda b, j: (b, cj(j), 0)),
                pl.BlockSpec((1, HEADS, blk), lambda b, j: (b, d, cj(j)))]

    cn_spec = pl.BlockSpec((1, UNITS, DQK, 2 * DV), lambda b, j: (b, 0, 0, 0))
    m_spec = pl.BlockSpec((1, UNITS, LANES), lambda b, j: (b, 0, 0))
    args = (q, kt, v, cola, colb, rowa)
    return pl.pallas_call(
        functools.partial(_mlstm_kernel, n_sub=n_sub),
        grid=(bsz, nc),
        in_specs=specs(False) + specs(True) + [cn_spec, m_spec],
        out_specs=[pl.BlockSpec((1, blk, M_WIDTH), lambda b, j: (b, j, 0)),
                   pl.BlockSpec((1, blk, M_WIDTH), lambda b, j: (b, nc - 1 - j, 0)),
                   cn_spec, m_spec],
        out_shape=[jax.ShapeDtypeStruct((bsz, t, M_WIDTH), BF16),
                   jax.ShapeDtypeStruct((bsz, t, M_WIDTH), BF16),
                   jax.ShapeDtypeStruct((bsz, UNITS, DQK, 2 * DV), F32),
                   jax.ShapeDtypeStruct((bsz, UNITS, LANES), F32)],
        compiler_params=_cparams("parallel", "arbitrary"),
        name="mlstm",
    )(*args, *args, cn0, m0)


OUT_PITCH = _pitch(CHUNK)


def _fourier_dense_kernel(xr_ref, xi_ref, cst_ref, bf_ref, o_ref):
    xx = jnp.concatenate([xr_ref[0], xi_ref[0]], axis=0)
    yr = jnp.dot(cst_ref[...], xx, preferred_element_type=F32)
    o_ref[0] = (yr + bf_ref[...]).astype(BF16)


def _fourier_dense(xr, xi, cst, bfno):
    bsz, t, w = xr.shape
    blk = pl.BlockSpec((1, t, w), lambda b: (b, 0, 0))
    return pl.pallas_call(
        _fourier_dense_kernel,
        grid=(bsz,),
        in_specs=[blk, blk,
                  pl.BlockSpec((t, 2 * t), lambda b: (0, 0)),
                  pl.BlockSpec((1, w), lambda b: (0, 0))],
        out_specs=blk,
        out_shape=jax.ShapeDtypeStruct((bsz, t, w), BF16),
        compiler_params=_cparams("parallel"),
        name="fourier_dense",
    )(xr, xi, cst, bfno.reshape(1, w))


def _fourier_fused_kernel(xr_ref, xi_ref, f1_ref, m_ref, bf_ref, o_ref, gr_scr, gi_scr, y_scr, *, n1, n2):
    pin = _pitch(n2)

    def stage1(j, carry):
        t2 = 2 * j

        def ld(ref, s):
            return ref[0, 0, pl.ds(s, n1, stride=pin), :].astype(BF16)

        top = jnp.concatenate([ld(xr_ref, t2), ld(xr_ref, t2 + 1)], axis=1)
        bot = jnp.concatenate([ld(xi_ref, t2), ld(xi_ref, t2 + 1)], axis=1)
        g = jnp.dot(f1_ref[...], jnp.concatenate([top, bot], axis=0), preferred_element_type=F32)
        gr_scr[pl.ds(t2, n1, stride=pin), :] = g[:n1, :GC]
        gr_scr[pl.ds(t2 + 1, n1, stride=pin), :] = g[:n1, GC:]
        gi_scr[pl.ds(t2, n1, stride=pin), :] = g[n1:, :GC]
        gi_scr[pl.ds(t2 + 1, n1, stride=pin), :] = g[n1:, GC:]
        return carry

    jax.lax.fori_loop(0, n2 // 2, stage1, 0, unroll=True)

    def stage2(k1, carry):
        base = pl.multiple_of(k1 * pin, 8)
        gg = jnp.concatenate([gr_scr[pl.ds(base, n2), :], gi_scr[pl.ds(base, n2), :]], axis=0).astype(BF16)
        yr = jnp.dot(m_ref[k1], gg, preferred_element_type=F32) + bf_ref[0]
        y_scr[pl.ds(k1, n2, stride=OUT_PITCH), :] = yr
        return carry

    jax.lax.fori_loop(0, n1, stage2, 0, unroll=True)
    for k2 in range(n2):
        o_ref[0, k2 * n1:(k2 + 1) * n1, :] = y_scr[k2 * OUT_PITCH:k2 * OUT_PITCH + n1, :].astype(BF16)


def _fourier_fused(xr, xi, f1, mtab, bfno, *, n1):
    bsz, _, rows_in, _ = xr.shape
    n2 = mtab.shape[1]
    pin = _pitch(n2)
    assert rows_in == n1 * pin and n1 == CHUNK
    in_blk = pl.BlockSpec((1, 1, rows_in, GC), lambda b, g: (b, g, 0, 0))
    return pl.pallas_call(
        functools.partial(_fourier_fused_kernel, n1=n1, n2=n2),
        grid=(bsz, GROUPS),
        in_specs=[in_blk, in_blk,
                  pl.BlockSpec((2 * n1, 2 * n1), lambda b, g: (0, 0)),
                  pl.BlockSpec((n1, n2, 2 * n2), lambda b, g: (0, 0, 0)),
                  pl.BlockSpec((1, 1, GC), lambda b, g: (g, 0, 0))],
        out_specs=pl.BlockSpec((1, n1 * n2, GC), lambda b, g: (b, 0, g)),
        out_shape=jax.ShapeDtypeStruct((bsz, n1 * n2, GROUPS * GC), BF16),
        scratch_shapes=[pltpu.VMEM((n1 * pin, GC), F32), pltpu.VMEM((n1 * pin, GC), F32),
                        pltpu.VMEM((n2 * OUT_PITCH, GC), F32)],
        compiler_params=_cparams("parallel", "parallel"),
        name="fourier_fused",
    )(xr, xi, f1, mtab, bfno.reshape(GROUPS, 1, GC))


def _dft_consts(t):
    ang = 2.0 * np.pi / GC * np.outer(np.arange(GC), np.arange(GC))
    cs = np.concatenate([np.cos(ang), -np.sin(ang)], axis=0) / np.sqrt(GC)
    out = {"cs": cs.astype(np.float32)}
    if t <= 2 * CHUNK:
        ang = 2.0 * np.pi / t * np.mod(np.outer(np.arange(t), np.arange(t)), t)
        out["dense"] = (np.concatenate([np.cos(ang), np.sin(ang)], axis=1) / np.sqrt(t)).astype(np.float32)
    else:
        n1 = CHUNK
        n2 = t // n1
        ang = 2.0 * np.pi / n1 * np.mod(np.outer(np.arange(n1), np.arange(n1)), n1)
        c, s = np.cos(ang), np.sin(ang)
        out["f1"] = (np.block([[c, s], [-s, c]]) / np.sqrt(n1)).astype(np.float32)
        k = np.arange(n1)[:, None, None] + n1 * np.arange(n2)[None, :, None]
        ang = 2.0 * np.pi / t * np.mod(k * np.arange(n2)[None, None, :], t)
        out["mtab"] = (np.concatenate([np.cos(ang), np.sin(ang)], axis=2) / np.sqrt(n2)).astype(np.float32)
    return out


def _outproj_kernel(hf_ref, hb_ref, ga_ref, vs_ref, gb_ref, ym_ref, szc_ref, x_ref, gt_ref,
                    wout_ref, ghn_ref, gsgu_ref, gpost_ref, wsp_ref, bsp_ref, xo_ref, y_scr):
    tm = x_ref.shape[1]
    def project(lo, hi):
        return jnp.dot(y_scr[:, lo:hi], wout_ref[lo:hi, :], preferred_element_type=F32)

    for h in range(HEADS):
        sl = slice(h * DV, (h + 1) * DV)
        hh = (hf_ref[0, :, sl] + hb_ref[0, :, sl]).astype(F32)
        hn = hh * jax.lax.rsqrt(jnp.mean(hh * hh, axis=-1, keepdims=True) + EPS) * ghn_ref[:, sl]
        y_scr[:, sl] = hn.astype(BF16) * ga_ref[0, :, sl]
    out = project(0, M_WIDTH)
    vs = vs_ref[0].astype(F32)
    vn = (vs * jax.lax.rsqrt(jnp.mean(vs * vs, axis=-1, keepdims=True) + EPS) * gsgu_ref[...]).astype(BF16)
    for c in range(tm // CHUNK):
        rows = slice(c * CHUNK, (c + 1) * CHUNK)
        for g in range(GROUPS):
            cols = slice(g * GC, (g + 1) * GC)
            mixed = jnp.dot(wsp_ref[g], vn[rows, cols], preferred_element_type=F32) + bsp_ref[:, cols]
            y_scr[rows, M_WIDTH + g * GC:M_WIDTH + (g + 1) * GC] = gb_ref[0, rows, cols] * mixed.astype(BF16)
    for c in range(tm // CHUNK):
        rows = slice(c * CHUNK, (c + 1) * CHUNK)
        for g in range(GROUPS):
            cols = slice(g * GC, (g + 1) * GC)
            y_scr[rows, M_WIDTH + S_WIDTH + g * GC:M_WIDTH + S_WIDTH + (g + 1) * GC] = (
                ym_ref[0, rows, cols] * szc_ref[0, rows, cols])
    out = out + project(M_WIDTH, 2 * D_MODEL)
    on = out * jax.lax.rsqrt(jnp.mean(out * out, axis=-1, keepdims=True) + EPS) * gpost_ref[...]
    xo_ref[0] = x_ref[0] + gt_ref[0] * on


def _outproj(hf, hb, ga, vs, gb, ym, szc, x, gt, w_out, ghn, gsgu, gpost, wsp, bsp, *, tm):
    bsz, t, d = x.shape
    wide = pl.BlockSpec((1, tm, M_WIDTH), lambda b, i: (b, i, 0))
    half = pl.BlockSpec((1, tm, S_WIDTH), lambda b, i: (b, i, 0))
    const2 = lambda shape: pl.BlockSpec(shape, lambda b, i: (0,) * len(shape))
    return pl.pallas_call(
        _outproj_kernel,
        grid=(bsz, t // tm),
        in_specs=[wide, wide, wide, half, half, half, half, wide,
                  pl.BlockSpec((1, 1, d), lambda b, i: (b, 0, 0)),
                  const2((2 * D_MODEL, d)), const2((1, M_WIDTH)), const2((1, S_WIDTH)), const2((1, d)),
                  const2((GROUPS, CHUNK, CHUNK)), const2((CHUNK, S_WIDTH))],
        out_specs=wide,
        out_shape=jax.ShapeDtypeStruct((bsz, t, d), F32),
        scratch_shapes=[pltpu.VMEM((tm, 2 * D_MODEL), BF16)],
        compiler_params=_cparams("parallel", "parallel"),
        name="outproj",
    )(hf, hb, ga, vs, gb, ym, szc, x, gt, w_out, ghn, gsgu, gpost, wsp, bsp)


_W_IN_SPLITS = np.cumsum([0, QK_W, QK_W, M_WIDTH, 4 * HEADS, M_WIDTH, M_WIDTH, S_WIDTH, S_WIDTH, S_WIDTH, F_WIDTH,
                          F_WIDTH])
_N_NAT = sum(_wrows(w) for _, w in _FULL_PIECES)
_GATE_ROWS = 4 * HEADS


def _wprep_plan():
    c = _W_IN_SPLITS
    starts, q_blocks, gate_block, f_block = [], None, None, None
    for i in (1, 0, 2, 3, 4, 5, 6, 7, 8, 9, 10):
        if i == 3:
            gate_block = len(starts)
            starts.append(0)
            continue
        first = len(starts)
        if i == 9:
            f_block = first
            starts += list(range(int(c[i]), int(c[i + 1]), _WT_BLK // 2))
            continue
        starts += list(range(int(c[i]), int(c[i + 1]), _WT_BLK))
        if i == 0:
            q_blocks = (first, len(starts))
    return starts, q_blocks, gate_block, f_block


F_STEP_GROUPS = _WT_BLK // (2 * GC)


def _wprep_kernel(starts_ref, w_ref, g_ref, cst_ref, wf_ref, o_ref, *, q_blocks, gate_block, f_block):
    del starts_ref
    s = pl.program_id(1)
    n_f = GROUPS // F_STEP_GROUPS

    @pl.when((s != gate_block) & ((s < f_block) | (s >= f_block + n_f)))
    def _():
        scale = jnp.where((s >= q_blocks[0]) & (s < q_blocks[1]), DQK ** -0.5, 1.0)
        o_ref[0] = (w_ref[0] * scale).astype(BF16)

    for k in range(n_f):
        @pl.when(s == f_block + k)
        def _(k=k):
            outs = []
            for gg in range(F_STEP_GROUPS):
                a = w_ref[0, gg * GC:(gg + 1) * GC, :].astype(BF16)
                wc = jnp.dot(cst_ref[...], wf_ref[0, k * F_STEP_GROUPS + gg].astype(BF16),
                             preferred_element_type=F32)
                for half in range(2):
                    cw_t = wc[half * GC:(half + 1) * GC].T.astype(BF16)
                    outs.append(jnp.dot(cw_t, a, preferred_element_type=F32))
            o_ref[0] = jnp.concatenate(outs, axis=0).astype(BF16)

    @pl.when(s == gate_block)
    def _():
        g = g_ref[0]
        zero = jnp.zeros((LANES - UNITS, g.shape[1]), F32)
        tail = jnp.zeros((_WT_BLK - 2 * LANES, g.shape[1]), F32)
        o_ref[0] = jnp.concatenate([g[0:8], g[16:24], zero,
                                    g[8:16], g[24:32], zero,
                                    tail], axis=0).astype(BF16)


def _wprep(w_in, cst, w_fno):
    depth, d, p_in = w_in.shape
    w_t = jnp.swapaxes(w_in, 1, 2)
    starts, q_blocks, gate_block, f_block = _wprep_plan()
    assert _W_IN_SPLITS[3] % _GATE_ROWS == 0 and len(starts) * _WT_BLK == QK_W + _N_NAT
    grid_spec = pltpu.PrefetchScalarGridSpec(
        num_scalar_prefetch=1,
        grid=(depth, len(starts)),
        in_specs=[pl.BlockSpec((pl.Element(1), pl.Element(_WT_BLK), pl.Element(d)),
                               lambda l, s, st: (l, st[s] * 8, 0)),
                  pl.BlockSpec((1, _GATE_ROWS, d), lambda l, s, st: (l, int(_W_IN_SPLITS[3]) // _GATE_ROWS, 0)),
                  pl.BlockSpec((2 * GC, GC), lambda l, s, st: (0, 0)),
                  pl.BlockSpec((1, GROUPS, GC, GC), lambda l, s, st: (l, 0, 0, 0))],
        out_specs=pl.BlockSpec((1, _WT_BLK, d), lambda l, s, st: (l, s, 0)),
    )
    return pl.pallas_call(
        functools.partial(_wprep_kernel, q_blocks=q_blocks, gate_block=gate_block, f_block=f_block),
        grid_spec=grid_spec,
        out_shape=jax.ShapeDtypeStruct((depth, QK_W + _N_NAT, d), BF16),
        compiler_params=_cparams("parallel", "arbitrary"),
        name="wprep",
    )(jnp.asarray(starts, jnp.int32) // 8, w_t, w_t, cst, w_fno)


def _gate_bias(b_gate_l):
    b_i = jnp.tile(jnp.concatenate([b_gate_l[0:8], b_gate_l[16:24]]), LANES // UNITS).reshape(1, LANES)
    b_f = jnp.tile(jnp.concatenate([b_gate_l[8:16], b_gate_l[24:32]]), LANES // UNITS).reshape(1, LANES)
    return b_i, b_f


def kernel(x, c, ctx, c_ctx, w_mod, b_mod, g_pre, g_post, w_in, b_gate, g_hnorm, g_sgu, w_sp, b_sp, w_fno, b_fno, w_out):
    bsz, t_lat, d = x.shape
    t_ctx = ctx.shape[1]
    depth = w_mod.shape[0]
    tm_in, tm_out = min(TM_IN_LATENT, t_lat), min(TM_OUT_LATENT, t_lat)
    assert d == D_MODEL and t_lat % tm_in == 0 and t_lat % tm_out == 0 and t_ctx % TM_CONTEXT == 0 and bsz + 1 <= 8

    cc = jnp.concatenate([c, c_ctx[None, :], jnp.zeros((8 - bsz - 1, d), c.dtype)], axis=0)
    mod = _modulation(cc, w_mod, b_mod)
    consts_lat, consts_ctx = _dft_consts(t_lat), _dft_consts(t_ctx)
    w_t_all = _wprep(w_in, jnp.asarray(consts_lat["cs"]).astype(BF16), w_fno)

    def fourier(xr, xi, consts, bfno):
        if "dense" in consts:
            return _fourier_dense(xr, xi, jnp.asarray(consts["dense"]).astype(BF16), bfno)
        return _fourier_fused(xr, xi, jnp.asarray(consts["f1"]).astype(BF16),
                              jnp.asarray(consts["mtab"]).astype(BF16), bfno, n1=CHUNK)

    cn_zero = jnp.zeros((bsz, UNITS, DQK, 2 * DV), F32)
    m_zero = jnp.zeros((bsz, UNITS, LANES), F32)
    xc = ctx
    for l in range(depth):
        sh_l, sc_l, gt_l = (mod[l, :bsz, i * d:(i + 1) * d].reshape(bsz, 1, d) for i in range(3))
        sh_c, sc_c, gt_c = (jnp.broadcast_to(mod[l, bsz, i * d:(i + 1) * d].reshape(1, 1, d), (bsz, 1, d))
                            for i in range(3))
        b_i, b_f = _gate_bias(b_gate[l])
        gpre = g_pre[l].reshape(1, d)
        wsp = w_sp[l].astype(BF16)
        bsp = jnp.broadcast_to(b_sp[l].T[:, :, None], (CHUNK, GROUPS, GC)).reshape(CHUNK, S_WIDTH)
        wo = w_out[l].astype(BF16)
        tail = (wo, g_hnorm[l].reshape(1, M_WIDTH), g_sgu[l].reshape(1, S_WIDTH), g_post[l].reshape(1, d), wsp, bsp)
        front = functools.partial(_inproj, g_pre=gpre, w_t=w_t_all, b_i=b_i, b_f=b_f, layer=l)

        last = l == depth - 1
        pc = front(xc, sc_c, sh_c, full=not last, tm=TM_CONTEXT)
        hf_c, hb_c, cn_c, m_c = _mlstm(pc["q"], pc["kT"], pc["v"], pc["cola"], pc["colb"], pc["rowa"], cn_zero, m_zero)
        p = front(x, sc_l, sh_l, full=True, tm=tm_in)
        hf, hb, _, _ = _mlstm(p["q"], p["kT"], p["v"], p["cola"], p["colb"], p["rowa"], cn_c, m_c)
        ym = fourier(p["xr"], p["xi"], consts_lat, b_fno[l])
        x = _outproj(hf, hb, p["ga"], p["vs"], p["gb"], ym, p["szc"], x, gt_l, *tail, tm=tm_out)
        if not last:
            ymc = fourier(pc["xr"], pc["xi"], consts_ctx, b_fno[l])
            xc = _outproj(hf_c, hb_c, pc["ga"], pc["vs"], pc["gb"], ymc, pc["szc"], xc, gt_c, *tail, tm=TM_CONTEXT)
    return x
```

```python
import functools

import numpy as np
import jax
import jax.numpy as jnp
from jax.experimental import pallas as pl
from jax.experimental.pallas import tpu as pltpu

EPS = 1e-6
LOG2E = 1.4426950408889634
D_MODEL = 1024
HEADS = 8
DV = 128
DQK = 64
QK_W = HEADS * DQK
M_WIDTH = HEADS * DV
CHUNK = 128
S_WIDTH = 512
F_WIDTH = 512
GROUPS = 4
GC = 128
UNITS = 2 * HEADS
LANES = 128

V7X_VMEM_BYTES = 64 * 1024 * 1024
VMEM_LIMIT = V7X_VMEM_BYTES * 7 // 8

TM_IN_LATENT, TM_OUT_LATENT, TM_CONTEXT = 512, 1024, 256

F32 = jnp.float32
BF16 = jnp.bfloat16


def _cparams(*sem):
    return pltpu.CompilerParams(dimension_semantics=sem, vmem_limit_bytes=VMEM_LIMIT)


def _sigmoid(x):
    return 1.0 / (1.0 + jnp.exp(-x))


def _silu(x):
    return x * _sigmoid(x)


def _mod_kernel(cc_ref, w_ref, b_ref, o_ref):
    s = _silu(cc_ref[...]).astype(BF16)
    w = w_ref[0].astype(BF16)
    o_ref[0] = jnp.dot(s, w, preferred_element_type=F32) + b_ref[0]


def _modulation(cc, w_mod, b_mod):
    depth, d, d3 = w_mod.shape
    return pl.pallas_call(
        _mod_kernel,
        grid=(depth,),
        in_specs=[pl.BlockSpec((8, d), lambda l: (0, 0)),
                  pl.BlockSpec((1, d, d3), lambda l: (l, 0, 0)),
                  pl.BlockSpec((1, 1, d3), lambda l: (l, 0, 0))],
        out_specs=pl.BlockSpec((1, 8, d3), lambda l: (l, 0, 0)),
        out_shape=jax.ShapeDtypeStruct((depth, 8, d3), F32),
        compiler_params=_cparams("arbitrary"),
        name="modulation",
    )(cc, w_mod, b_mod.reshape(depth, 1, d3))


def _log_sigmoid(x):
    return jnp.minimum(x, 0.0) - jnp.log1p(jnp.exp(-jnp.abs(x)))


def _gate_scans(g_t, bi_ref, bf_ref, cola_ref, colb_ref, rowa_ref):
    tm = g_t.shape[1]
    n_c = tm // CHUNK
    assert n_c * UNITS <= LANES
    row = jax.lax.broadcasted_iota(jnp.int32, (CHUNK, LANES), 0)
    lane = jax.lax.broadcasted_iota(jnp.int32, (CHUNK, LANES), 1)
    fwd = lane % UNITS < HEADS
    gi = jnp.zeros((CHUNK, LANES), F32)
    gf = jnp.zeros((CHUNK, LANES), F32)
    pad = jnp.zeros((CHUNK - 2 * UNITS, CHUNK), F32)
    for c in range(n_c):
        t_c = jnp.concatenate([g_t[:, c * CHUNK:(c + 1) * CHUNK], pad], axis=0).T
        sel = lane // UNITS == c
        gi = jnp.where(sel, pltpu.roll(t_c, c * UNITS, axis=1) if c else t_c, gi)
        gf = jnp.where(sel, pltpu.roll(t_c, (c * UNITS - UNITS) % LANES, axis=1), gf)
    gi = gi + bi_ref[...]
    lf = _log_sigmoid(gf + bf_ref[...])

    def scan(x, op):
        pre, suf = x, x
        k = 1
        while k < CHUNK:
            sh = pltpu.roll(pre, k, axis=0)
            pre = jnp.where(row >= k, op(pre, sh), pre)
            sh = pltpu.roll(suf, CHUNK - k, axis=0)
            suf = jnp.where(row < CHUNK - k, op(suf, sh), suf)
            k *= 2
        return jnp.where(fwd, pre, suf)

    b = scan(lf, jnp.add)
    a = (gi - b) * LOG2E
    amax = scan(a, jnp.maximum)
    b = b * LOG2E
    a_t = a.T
    for c in range(n_c):
        rows = slice(c * CHUNK, (c + 1) * CHUNK)
        back = (LANES - c * UNITS) % LANES
        cola_ref[0, rows, :] = pltpu.roll(amax, back, axis=1) if c else amax
        colb_ref[0, rows, :] = pltpu.roll(b, back, axis=1) if c else b
        rowa_ref[0, :, rows] = a_t[c * UNITS:(c + 1) * UNITS, :]


def _pitch(n):
    p = n
    while (p // 8) % 2 == 0:
        p += 8
    return p


def _store_slabs(ref, g, z, n2):
    pin = _pitch(n2)
    for r in range(z.shape[0] // n2):
        ref[0, g, r * pin:r * pin + n2, :] = z[r * n2:(r + 1) * n2]
        if pin > n2:
            ref[0, g, r * pin + n2:(r + 1) * pin, :] = jnp.zeros((pin - n2, GC), F32)


def _inproj_kernel(x_ref, sc_ref, sh_ref, g_ref, wt_ref, bi_ref, bf_ref, *out_refs, pieces, slab_n2):
    nt = (((1,), (1,)), ((), ()))
    x = x_ref[0]
    y = x * jax.lax.rsqrt(jnp.mean(x * x, axis=-1, keepdims=True) + EPS)
    h = y * (g_ref[...] * (1.0 + sc_ref[0])) + sh_ref[0]
    hb = h.astype(BF16)

    def silu(z):
        t = z * 0.5
        return t + t * jnp.tanh(t)

    held = {}
    oi = 0
    for name, off, width in pieces:
        if name == "gates":
            gate_rows = QK_W + off
            w_kg = jnp.concatenate([wt_ref[0:QK_W, :], wt_ref[gate_rows:gate_rows + UNITS, :],
                                    wt_ref[gate_rows + LANES:gate_rows + LANES + UNITS, :]], axis=0)
            r = jax.lax.dot_general(w_kg, hb, nt, preferred_element_type=F32)
            out_refs[-1][0] = r[:QK_W].astype(BF16)
            _gate_scans(r[QK_W:], bi_ref, bf_ref, *out_refs[oi:oi + 3])
            oi += 3
            continue
        if name == "kT":
            continue
        r = jax.lax.dot_general(hb, wt_ref[QK_W + off:QK_W + off + width, :], nt, preferred_element_type=F32)
        if name == "f":
            xr_ref, xi_ref = out_refs[oi], out_refs[oi + 1]
            for g in range(GROUPS):
                zr, zi = r[:, 2 * g * GC:(2 * g + 1) * GC], r[:, (2 * g + 1) * GC:(2 * g + 2) * GC]
                if slab_n2 is None:
                    xr_ref[0, :, g * GC:(g + 1) * GC] = zr.astype(BF16)
                    xi_ref[0, :, g * GC:(g + 1) * GC] = zi.astype(BF16)
                else:
                    _store_slabs(xr_ref, g, zr, slab_n2)
                    _store_slabs(xi_ref, g, zi, slab_n2)
            oi += 2
        elif name in _HELD:
            held[name] = r
        else:
            if name == "za":
                r = (0.5 + 0.5 * jnp.tanh(held["o"] * 0.5)) * silu(r)
            elif name == "zb":
                r = held["u"] * silu(r)
            elif name == "zc":
                r = silu(r)
            out_refs[oi][0] = r.astype(BF16)
            oi += 1


_FULL_PIECES = (("q", QK_W), ("v", M_WIDTH), ("gates", 2 * LANES), ("o", M_WIDTH), ("za", M_WIDTH),
                ("u", S_WIDTH), ("vs", S_WIDTH), ("zb", S_WIDTH), ("f", 2 * F_WIDTH), ("zc", F_WIDTH))
_STATE_PIECES = _FULL_PIECES[:3]
_HELD = ("o", "u")
_GATED_NAME = {"za": "ga", "zb": "gb", "zc": "szc"}
_WT_BLK = 512


def _wrows(width):
    return -(-width // _WT_BLK) * _WT_BLK


def _inproj(x, sc, sh, g_pre, w_t, b_i, b_f, *, layer, full, tm):
    bsz, t, d = x.shape
    names = _FULL_PIECES if full else _STATE_PIECES
    slab_n2 = t // CHUNK if t > 2 * CHUNK else None
    row_spec = lambda width: pl.BlockSpec((1, tm, width), lambda b, i: (b, i, 0))
    pieces, off = [], 0
    out_names, out_shapes, out_specs = [], [], []
    for name, width in names:
        pieces.append((name, off, width))
        off += _wrows(width)
        if name == "gates":
            out_names += ["cola", "colb", "rowa"]
            out_shapes += [jax.ShapeDtypeStruct((bsz, t, LANES), F32)] * 2 + [jax.ShapeDtypeStruct((bsz, UNITS, t), F32)]
            out_specs += [row_spec(LANES), row_spec(LANES), pl.BlockSpec((1, UNITS, tm), lambda b, i: (b, 0, i))]
        elif name == "f":
            out_names += ["xr", "xi"]
            if slab_n2 is None:
                out_shapes += [jax.ShapeDtypeStruct((bsz, t, F_WIDTH), BF16)] * 2
                out_specs += [row_spec(F_WIDTH)] * 2
            else:
                assert tm % slab_n2 == 0
                pin = _pitch(slab_n2)
                out_shapes += [jax.ShapeDtypeStruct((bsz, GROUPS, CHUNK * pin, GC), F32)] * 2
                out_specs += [pl.BlockSpec((1, GROUPS, tm // slab_n2 * pin, GC), lambda b, i: (b, 0, i, 0))] * 2
        elif name not in _HELD:
            out_names.append(_GATED_NAME.get(name, name))
            out_shapes.append(jax.ShapeDtypeStruct((bsz, t, width), BF16))
            out_specs.append(row_spec(width))
    n_nat = off
    pieces.append(("kT", 0, QK_W))
    out_names.append("kT")
    out_shapes.append(jax.ShapeDtypeStruct((bsz, QK_W, t), BF16))
    out_specs.append(pl.BlockSpec((1, QK_W, tm), lambda b, i: (b, 0, i)))
    const = lambda shape, **kw: pl.BlockSpec(shape, lambda b, i: (0,) * len(shape), **kw)
    outs = pl.pallas_call(
        functools.partial(_inproj_kernel, pieces=tuple(pieces), slab_n2=slab_n2),
        grid=(bsz, t // tm),
        in_specs=[pl.BlockSpec((1, tm, d), lambda b, i: (b, i, 0)),
                  pl.BlockSpec((1, 1, d), lambda b, i: (b, 0, 0)),
                  pl.BlockSpec((1, 1, d), lambda b, i: (b, 0, 0)),
                  const((1, d)),
                  pl.BlockSpec((None, QK_W + n_nat, d), lambda b, i: (layer, 0, 0), pipeline_mode=pl.Buffered(1)),
                  const((1, LANES)), const((1, LANES))],
        out_specs=out_specs,
        out_shape=out_shapes,
        compiler_params=_cparams("parallel", "parallel"),
        name="inproj_full" if full else "inproj_state",
    )(x, sc, sh, g_pre, w_t, b_i, b_f)
    return dict(zip(out_names, outs))


def _mlstm_unit(h, d, sub, q_ref, kt_ref, v_ref, rowa_ref, m_diag, g_cols, c_cols, g_ends, hout_ref, cn_ref, mask):
    i = d * HEADS + h
    pair = h // 2
    rows = slice(sub * CHUNK, (sub + 1) * CHUNK)
    q_pair = q_ref[0, rows, pair * LANES:(pair + 1) * LANES]
    kt_h = kt_ref[0, h * DQK:(h + 1) * DQK, rows]
    zk = jnp.zeros((DQK, CHUNK), BF16)
    kt_ext = jnp.concatenate([kt_h, zk] if h % 2 == 0 else [zk, kt_h], axis=0)
    v_h = v_ref[0, rows, h * DV:(h + 1) * DV]
    vaug = jnp.concatenate([v_h, jnp.ones((CHUNK, DV), BF16)], axis=1)
    g = jnp.broadcast_to(g_cols[:, i:i + 1], (CHUNK, CHUNK))
    c = jnp.broadcast_to(c_cols[:, i:i + 1], (CHUNK, CHUNK))
    g_end = jnp.broadcast_to(g_ends[:, i:i + 1], (1, LANES))
    m_row = jnp.broadcast_to(m_diag[:, i:i + 1], (1, LANES))
    a_row = rowa_ref[0, h:h + 1, rows]
    cn = cn_ref[0, i]

    dmat = jnp.where(mask, jnp.exp2(a_row - g), 0.0)
    s = jnp.dot(q_pair, kt_ext, preferred_element_type=F32)
    p = (s * dmat).astype(BF16)
    qs = q_pair * jnp.exp2(m_row - g).astype(BF16)
    lhs = jnp.concatenate([p, qs], axis=1)
    zc = jnp.zeros((DQK, 2 * DV), BF16)
    cnb = cn.astype(BF16)
    cn_ext = jnp.concatenate([cnb, zc] if h % 2 == 0 else [zc, cnb], axis=0)
    rhs = jnp.concatenate([vaug, cn_ext], axis=0)
    out = jnp.dot(lhs, rhs, preferred_element_type=F32)
    num, den = out[:, :DV], out[:, DV:]
    hval = num / jnp.maximum(jnp.abs(den), jnp.exp2(c))
    hout_ref[0, rows, h * DV:(h + 1) * DV] = hval.astype(hout_ref.dtype)

    kts = kt_h * jnp.exp2(a_row - g_end).astype(BF16)
    upd = jnp.dot(kts, vaug, preferred_element_type=F32)
    decay = jnp.exp2(m_row - g_end)
    cn_ref[0, i] = jnp.concatenate([decay, decay], axis=1) * cn + upd


MLSTM_SUB = 8


def _mlstm_kernel(qf_ref, ktf_ref, vf_ref, caf_ref, cbf_ref, raf_ref,
                  qb_ref, ktb_ref, vb_ref, cab_ref, cbb_ref, rab_ref,
                  cn0_ref, m0_ref, hf_ref, hb_ref, cn_ref, m_ref, *, n_sub):
    @pl.when(pl.program_id(1) == 0)
    def _():
        cn_ref[...] = cn0_ref[...]
        m_ref[...] = m0_ref[...]

    t_idx = jax.lax.broadcasted_iota(jnp.int32, (CHUNK, CHUNK), 0)
    s_idx = jax.lax.broadcasted_iota(jnp.int32, (CHUNK, CHUNK), 1)
    unit_row = jax.lax.broadcasted_iota(jnp.int32, (UNITS, LANES), 0)
    unit_lane = jax.lax.broadcasted_iota(jnp.int32, (UNITS, LANES), 1)
    m_diag = jnp.sum(jnp.where(unit_row == unit_lane, m_ref[0], 0.0), axis=0, keepdims=True)
    fwd_lane = jax.lax.broadcasted_iota(jnp.int32, (1, LANES), 1) < HEADS

    def columns(ca_ref, cb_ref, sub, end):
        rows = slice(sub * CHUNK, (sub + 1) * CHUNK)
        g_cols = jnp.maximum(m_diag, ca_ref[0, rows, :])
        g_ends = g_cols[end:end + 1, :]
        m_new = cb_ref[0, sub * CHUNK + end:sub * CHUNK + end + 1, :] + g_ends
        return (g_cols, -(cb_ref[0, rows, :] + g_cols), g_ends), m_new

    for step in range(n_sub):
        sub_f, sub_b = step, n_sub - 1 - step
        cols_f, m_new_f = columns(caf_ref, cbf_ref, sub_f, CHUNK - 1)
        cols_b, m_new_b = columns(cab_ref, cbb_ref, sub_b, 0)
        for h in range(HEADS):
            _mlstm_unit(h, 0, sub_f, qf_ref, ktf_ref, vf_ref, raf_ref, m_diag, *cols_f, hf_ref, cn_ref, s_idx <= t_idx)
            _mlstm_unit(h, 1, sub_b, qb_ref, ktb_ref, vb_ref, rab_ref, m_diag, *cols_b, hb_ref, cn_ref, s_idx >= t_idx)
        m_diag = jnp.where(fwd_lane, m_new_f, m_new_b)
    m_ref[0] = jnp.where(unit_row == unit_lane, jnp.broadcast_to(m_diag, (UNITS, LANES)), 0.0)


def _mlstm(q, kt, v, cola, colb, rowa, cn0, m0):
    bsz, t, _ = q.shape
    n_sub = min(MLSTM_SUB, t // CHUNK)
    blk = n_sub * CHUNK
    nc = t // blk

    def specs(rev):
        cj = (lambda j: nc - 1 - j) if rev else (lambda j: j)
        d = 1 if rev else 0
        return [pl.BlockSpec((1, blk, QK_W), lambda b, j: (b, cj(j), 0)),
                pl.BlockSpec((1, QK_W, blk), lambda b, j: (b, 0, cj(j))),
                pl.BlockSpec((1, blk, M_WIDTH), lambda b, j: (b, cj(j), 0)),
                pl.BlockSpec((1, blk, LANES), lambda b, j: (b, cj(j), 0)),
                pl.BlockSpec((1, blk, LANES), lambda b, j: (b, cj(j), 0)),
                pl.BlockSpec((1, HEADS, blk), lambda b, j: (b, d, cj(j)))]

    cn_spec = pl.BlockSpec((1, UNITS, DQK, 2 * DV), lambda b, j: (b, 0, 0, 0))
    m_spec = pl.BlockSpec((1, UNITS, LANES), lambda b, j: (b, 0, 0))
    args = (q, kt, v, cola, colb, rowa)
    return pl.pallas_call(
        functools.partial(_mlstm_kernel, n_sub=n_sub),
        grid=(bsz, nc),
        in_specs=specs(False) + specs(True) + [cn_spec, m_spec],
        out_specs=[pl.BlockSpec((1, blk, M_WIDTH), lambda b, j: (b, j, 0)),
                   pl.BlockSpec((1, blk, M_WIDTH), lambda b, j: (b, nc - 1 - j, 0)),
                   cn_spec, m_spec],
        out_shape=[jax.ShapeDtypeStruct((bsz, t, M_WIDTH), BF16),
                   jax.ShapeDtypeStruct((bsz, t, M_WIDTH), BF16),
                   jax.ShapeDtypeStruct((bsz, UNITS, DQK, 2 * DV), F32),
                   jax.ShapeDtypeStruct((bsz, UNITS, LANES), F32)],
        compiler_params=_cparams("parallel", "arbitrary"),
        name="mlstm",
    )(*args, *args, cn0, m0)


OUT_PITCH = _pitch(CHUNK)


def _fourier_dense_kernel(xr_ref, xi_ref, cst_ref, bf_ref, o_ref):
    xx = jnp.concatenate([xr_ref[0], xi_ref[0]], axis=0)
    yr = jnp.dot(cst_ref[...], xx, preferred_element_type=F32)
    o_ref[0] = (yr + bf_ref[...]).astype(BF16)


def _fourier_dense(xr, xi, cst, bfno):
    bsz, t, w = xr.shape
    blk = pl.BlockSpec((1, t, w), lambda b: (b, 0, 0))
    return pl.pallas_call(
        _fourier_dense_kernel,
        grid=(bsz,),
        in_specs=[blk, blk,
                  pl.BlockSpec((t, 2 * t), lambda b: (0, 0)),
                  pl.BlockSpec((1, w), lambda b: (0, 0))],
        out_specs=blk,
        out_shape=jax.ShapeDtypeStruct((bsz, t, w), BF16),
        compiler_params=_cparams("parallel"),
        name="fourier_dense",
    )(xr, xi, cst, bfno.reshape(1, w))


def _fourier_fused_kernel(xr_ref, xi_ref, f1_ref, m_ref, bf_ref, o_ref, gr_scr, gi_scr, y_scr, *, n1, n2):
    pin = _pitch(n2)

    def stage1(j, carry):
        t2 = 2 * j

        def ld(ref, s):
            return ref[0, 0, pl.ds(s, n1, stride=pin), :].astype(BF16)

        top = jnp.concatenate([ld(xr_ref, t2), ld(xr_ref, t2 + 1)], axis=1)
        bot = jnp.concatenate([ld(xi_ref, t2), ld(xi_ref, t2 + 1)], axis=1)
        g = jnp.dot(f1_ref[...], jnp.concatenate([top, bot], axis=0), preferred_element_type=F32)
        gr_scr[pl.ds(t2, n1, stride=pin), :] = g[:n1, :GC]
        gr_scr[pl.ds(t2 + 1, n1, stride=pin), :] = g[:n1, GC:]
        gi_scr[pl.ds(t2, n1, stride=pin), :] = g[n1:, :GC]
        gi_scr[pl.ds(t2 + 1, n1, stride=pin), :] = g[n1:, GC:]
        return carry

    jax.lax.fori_loop(0, n2 // 2, stage1, 0, unroll=True)

    def stage2(k1, carry):
        base = pl.multiple_of(k1 * pin, 8)
        gg = jnp.concatenate([gr_scr[pl.ds(base, n2), :], gi_scr[pl.ds(base, n2), :]], axis=0).astype(BF16)
        yr = jnp.dot(m_ref[k1], gg, preferred_element_type=F32) + bf_ref[0]
        y_scr[pl.ds(k1, n2, stride=OUT_PITCH), :] = yr
        return carry

    jax.lax.fori_loop(0, n1, stage2, 0, unroll=True)
    for k2 in range(n2):
        o_ref[0, k2 * n1:(k2 + 1) * n1, :] = y_scr[k2 * OUT_PITCH:k2 * OUT_PITCH + n1, :].astype(BF16)


def _fourier_fused(xr, xi, f1, mtab, bfno, *, n1):
    bsz, _, rows_in, _ = xr.shape
    n2 = mtab.shape[1]
    pin = _pitch(n2)
    assert rows_in == n1 * pin and n1 == CHUNK
    in_blk = pl.BlockSpec((1, 1, rows_in, GC), lambda b, g: (b, g, 0, 0))
    return pl.pallas_call(
        functools.partial(_fourier_fused_kernel, n1=n1, n2=n2),
        grid=(bsz, GROUPS),
        in_specs=[in_blk, in_blk,
                  pl.BlockSpec((2 * n1, 2 * n1), lambda b, g: (0, 0)),
                  pl.BlockSpec((n1, n2, 2 * n2), lambda b, g: (0, 0, 0)),
                  pl.BlockSpec((1, 1, GC), lambda b, g: (g, 0, 0))],
        out_specs=pl.BlockSpec((1, n1 * n2, GC), lambda b, g: (b, 0, g)),
        out_shape=jax.ShapeDtypeStruct((bsz, n1 * n2, GROUPS * GC), BF16),
        scratch_shapes=[pltpu.VMEM((n1 * pin, GC), F32), pltpu.VMEM((n1 * pin, GC), F32),
                        pltpu.VMEM((n2 * OUT_PITCH, GC), F32)],
        compiler_params=_cparams("parallel", "parallel"),
        name="fourier_fused",
    )(xr, xi, f1, mtab, bfno.reshape(GROUPS, 1, GC))


def _dft_consts(t):
    ang = 2.0 * np.pi / GC * np.outer(np.arange(GC), np.arange(GC))
    cs = np.concatenate([np.cos(ang), -np.sin(ang)], axis=0) / np.sqrt(GC)
    out = {"cs": cs.astype(np.float32)}
    if t <= 2 * CHUNK:
        ang = 2.0 * np.pi / t * np.mod(np.outer(np.arange(t), np.arange(t)), t)
        out["dense"] = (np.concatenate([np.cos(ang), np.sin(ang)], axis=1) / np.sqrt(t)).astype(np.float32)
    else:
        n1 = CHUNK
        n2 = t // n1
        ang = 2.0 * np.pi / n1 * np.mod(np.outer(np.arange(n1), np.arange(n1)), n1)
        c, s = np.cos(ang), np.sin(ang)
        out["f1"] = (np.block([[c, s], [-s, c]]) / np.sqrt(n1)).astype(np.float32)
        k = np.arange(n1)[:, None, None] + n1 * np.arange(n2)[None, :, None]
        ang = 2.0 * np.pi / t * np.mod(k * np.arange(n2)[None, None, :], t)
        out["mtab"] = (np.concatenate([np.cos(ang), np.sin(ang)], axis=2) / np.sqrt(n2)).astype(np.float32)
    return out


def _outproj_kernel(hf_ref, hb_ref, ga_ref, vs_ref, gb_ref, ym_ref, szc_ref, x_ref, gt_ref,
                    wout_ref, ghn_ref, gsgu_ref, gpost_ref, wsp_ref, bsp_ref, xo_ref, y_scr):
    tm = x_ref.shape[1]
    for h in range(HEADS):
        sl = slice(h * DV, (h + 1) * DV)
        hh = (hf_ref[0, :, sl] + hb_ref[0, :, sl]).astype(F32)
        hn = hh * jax.lax.rsqrt(jnp.mean(hh * hh, axis=-1, keepdims=True) + EPS) * ghn_ref[:, sl]
        y_scr[:, sl] = hn.astype(BF16) * ga_ref[0, :, sl]
    vs = vs_ref[0].astype(F32)
    vn = (vs * jax.lax.rsqrt(jnp.mean(vs * vs, axis=-1, keepdims=True) + EPS) * gsgu_ref[...]).astype(BF16)
    for c in range(tm // CHUNK):
        rows = slice(c * CHUNK, (c + 1) * CHUNK)
        for g in range(GROUPS):
            cols = slice(g * GC, (g + 1) * GC)
            mixed = jnp.dot(wsp_ref[g], vn[rows, cols], preferred_element_type=F32) + bsp_ref[:, cols]
            y_scr[rows, M_WIDTH + g * GC:M_WIDTH + (g + 1) * GC] = gb_ref[0, rows, cols] * mixed.astype(BF16)
    for c in range(tm // CHUNK):
        rows = slice(c * CHUNK, (c + 1) * CHUNK)
        for g in range(GROUPS):
            cols = slice(g * GC, (g + 1) * GC)
            y_scr[rows, M_WIDTH + S_WIDTH + g * GC:M_WIDTH + S_WIDTH + (g + 1) * GC] = (
                ym_ref[0, rows, cols] * szc_ref[0, rows, cols])
    out = jnp.dot(y_scr[...], wout_ref[...], preferred_element_type=F32)
    on = out * jax.lax.rsqrt(jnp.mean(out * out, axis=-1, keepdims=True) + EPS) * gpost_ref[...]
    xo_ref[0] = x_ref[0] + gt_ref[0] * on


def _outproj(hf, hb, ga, vs, gb, ym, szc, x, gt, w_out, ghn, gsgu, gpost, wsp, bsp, *, tm):
    bsz, t, d = x.shape
    wide = pl.BlockSpec((1, tm, M_WIDTH), lambda b, i: (b, i, 0))
    half = pl.BlockSpec((1, tm, S_WIDTH), lambda b, i: (b, i, 0))
    const2 = lambda shape: pl.BlockSpec(shape, lambda b, i: (0,) * len(shape))
    return pl.pallas_call(
        _outproj_kernel,
        grid=(bsz, t // tm),
        in_specs=[wide, wide, wide, half, half, half, half, wide,
                  pl.BlockSpec((1, 1, d), lambda b, i: (b, 0, 0)),
                  const2((2 * D_MODEL, d)), const2((1, M_WIDTH)), const2((1, S_WIDTH)), const2((1, d)),
                  const2((GROUPS, CHUNK, CHUNK)), const2((CHUNK, S_WIDTH))],
        out_specs=wide,
        out_shape=jax.ShapeDtypeStruct((bsz, t, d), F32),
        scratch_shapes=[pltpu.VMEM((tm, 2 * D_MODEL), BF16)],
        compiler_params=_cparams("parallel", "parallel"),
        name="outproj",
    )(hf, hb, ga, vs, gb, ym, szc, x, gt, w_out, ghn, gsgu, gpost, wsp, bsp)


_W_IN_SPLITS = np.cumsum([0, QK_W, QK_W, M_WIDTH, 4 * HEADS, M_WIDTH, M_WIDTH, S_WIDTH, S_WIDTH, S_WIDTH, F_WIDTH,
                          F_WIDTH])
_N_NAT = sum(_wrows(w) for _, w in _FULL_PIECES)
_GATE_ROWS = 4 * HEADS


def _wprep_plan():
    c = _W_IN_SPLITS
    starts, q_blocks, gate_block, f_block = [], None, None, None
    for i in (1, 0, 2, 3, 4, 5, 6, 7, 8, 9, 10):
        if i == 3:
            gate_block = len(starts)
            starts.append(0)
            continue
        first = len(starts)
        if i == 9:
            f_block = first
            starts += list(range(int(c[i]), int(c[i + 1]), _WT_BLK // 2))
            continue
        starts += list(range(int(c[i]), int(c[i + 1]), _WT_BLK))
        if i == 0:
            q_blocks = (first, len(starts))
    return starts, q_blocks, gate_block, f_block


F_STEP_GROUPS = _WT_BLK // (2 * GC)


def _wprep_kernel(starts_ref, w_ref, g_ref, cst_ref, wf_ref, o_ref, *, q_blocks, gate_block, f_block):
    del starts_ref
    s = pl.program_id(1)
    n_f = GROUPS // F_STEP_GROUPS

    @pl.when((s != gate_block) & ((s < f_block) | (s >= f_block + n_f)))
    def _():
        scale = jnp.where((s >= q_blocks[0]) & (s < q_blocks[1]), DQK ** -0.5, 1.0)
        o_ref[0] = (w_ref[0] * scale).astype(BF16)

    for k in range(n_f):
        @pl.when(s == f_block + k)
        def _(k=k):
            outs = []
            for gg in range(F_STEP_GROUPS):
                a = w_ref[0, gg * GC:(gg + 1) * GC, :].astype(BF16)
                wc = jnp.dot(cst_ref[...], wf_ref[0, k * F_STEP_GROUPS + gg].astype(BF16),
                             preferred_element_type=F32)
                for half in range(2):
                    cw_t = wc[half * GC:(half + 1) * GC].T.astype(BF16)
                    outs.append(jnp.dot(cw_t, a, preferred_element_type=F32))
            o_ref[0] = jnp.concatenate(outs, axis=0).astype(BF16)

    @pl.when(s == gate_block)
    def _():
        g = g_ref[0]
        zero = jnp.zeros((LANES - UNITS, g.shape[1]), F32)
        tail = jnp.zeros((_WT_BLK - 2 * LANES, g.shape[1]), F32)
        o_ref[0] = jnp.concatenate([g[0:8], g[16:24], zero,
                                    g[8:16], g[24:32], zero,
                                    tail], axis=0).astype(BF16)


def _wprep(w_in, cst, w_fno):
    depth, d, p_in = w_in.shape
    w_t = jnp.swapaxes(w_in, 1, 2)
    starts, q_blocks, gate_block, f_block = _wprep_plan()
    assert _W_IN_SPLITS[3] % _GATE_ROWS == 0 and len(starts) * _WT_BLK == QK_W + _N_NAT
    grid_spec = pltpu.PrefetchScalarGridSpec(
        num_scalar_prefetch=1,
        grid=(depth, len(starts)),
        in_specs=[pl.BlockSpec((pl.Element(1), pl.Element(_WT_BLK), pl.Element(d)),
                               lambda l, s, st: (l, st[s] * 8, 0)),
                  pl.BlockSpec((1, _GATE_ROWS, d), lambda l, s, st: (l, int(_W_IN_SPLITS[3]) // _GATE_ROWS, 0)),
                  pl.BlockSpec((2 * GC, GC), lambda l, s, st: (0, 0)),
                  pl.BlockSpec((1, GROUPS, GC, GC), lambda l, s, st: (l, 0, 0, 0))],
        out_specs=pl.BlockSpec((1, _WT_BLK, d), lambda l, s, st: (l, s, 0)),
    )
    return pl.pallas_call(
        functools.partial(_wprep_kernel, q_blocks=q_blocks, gate_block=gate_block, f_block=f_block),
        grid_spec=grid_spec,
        out_shape=jax.ShapeDtypeStruct((depth, QK_W + _N_NAT, d), BF16),
        compiler_params=_cparams("parallel", "arbitrary"),
        name="wprep",
    )(jnp.asarray(starts, jnp.int32) // 8, w_t, w_t, cst, w_fno)


def _gate_bias(b_gate_l):
    b_i = jnp.tile(jnp.concatenate([b_gate_l[0:8], b_gate_l[16:24]]), LANES // UNITS).reshape(1, LANES)
    b_f = jnp.tile(jnp.concatenate([b_gate_l[8:16], b_gate_l[24:32]]), LANES // UNITS).reshape(1, LANES)
    return b_i, b_f


def kernel(x, c, ctx, c_ctx, w_mod, b_mod, g_pre, g_post, w_in, b_gate, g_hnorm, g_sgu, w_sp, b_sp, w_fno, b_fno, w_out):
    bsz, t_lat, d = x.shape
    t_ctx = ctx.shape[1]
    depth = w_mod.shape[0]
    tm_in, tm_out = min(TM_IN_LATENT, t_lat), min(TM_OUT_LATENT, t_lat)
    assert d == D_MODEL and t_lat % tm_in == 0 and t_lat % tm_out == 0 and t_ctx % TM_CONTEXT == 0 and bsz + 1 <= 8

    cc = jnp.concatenate([c, c_ctx[None, :], jnp.zeros((8 - bsz - 1, d), c.dtype)], axis=0)
    mod = _modulation(cc, w_mod, b_mod)
    consts_lat, consts_ctx = _dft_consts(t_lat), _dft_consts(t_ctx)
    w_t_all = _wprep(w_in, jnp.asarray(consts_lat["cs"]).astype(BF16), w_fno)

    def fourier(xr, xi, consts, bfno):
        if "dense" in consts:
            return _fourier_dense(xr, xi, jnp.asarray(consts["dense"]).astype(BF16), bfno)
        return _fourier_fused(xr, xi, jnp.asarray(consts["f1"]).astype(BF16),
                              jnp.asarray(consts["mtab"]).astype(BF16), bfno, n1=CHUNK)

    cn_zero = jnp.zeros((bsz, UNITS, DQK, 2 * DV), F32)
    m_zero = jnp.zeros((bsz, UNITS, LANES), F32)
    xc = ctx
    for l in range(depth):
        sh_l, sc_l, gt_l = (mod[l, :bsz, i * d:(i + 1) * d].reshape(bsz, 1, d) for i in range(3))
        sh_c, sc_c, gt_c = (jnp.broadcast_to(mod[l, bsz, i * d:(i + 1) * d].reshape(1, 1, d), (bsz, 1, d))
                            for i in range(3))
        b_i, b_f = _gate_bias(b_gate[l])
        gpre = g_pre[l].reshape(1, d)
        wsp = w_sp[l].astype(BF16)
        bsp = jnp.broadcast_to(b_sp[l].T[:, :, None], (CHUNK, GROUPS, GC)).reshape(CHUNK, S_WIDTH)
        wo = w_out[l].astype(BF16)
        tail = (wo, g_hnorm[l].reshape(1, M_WIDTH), g_sgu[l].reshape(1, S_WIDTH), g_post[l].reshape(1, d), wsp, bsp)
        front = functools.partial(_inproj, g_pre=gpre, w_t=w_t_all, b_i=b_i, b_f=b_f, layer=l)

        last = l == depth - 1
        pc = front(xc, sc_c, sh_c, full=not last, tm=TM_CONTEXT)
        hf_c, hb_c, cn_c, m_c = _mlstm(pc["q"], pc["kT"], pc["v"], pc["cola"], pc["colb"], pc["rowa"], cn_zero, m_zero)
        p = front(x, sc_l, sh_l, full=True, tm=tm_in)
        hf, hb, _, _ = _mlstm(p["q"], p["kT"], p["v"], p["cola"], p["colb"], p["rowa"], cn_c, m_c)
        ym = fourier(p["xr"], p["xi"], consts_lat, b_fno[l])
        x = _outproj(hf, hb, p["ga"], p["vs"], p["gb"], ym, p["szc"], x, gt_l, *tail, tm=tm_out)
        if not last:
            ymc = fourier(pc["xr"], pc["xi"], consts_ctx, b_fno[l])
            xc = _outproj(hf_c, hb_c, pc["ga"], pc["vs"], pc["gb"], ymc, pc["szc"], xc, gt_c, *tail, tm=TM_CONTEXT)
    return x
```
